```python
import jax
import jax.numpy as jnp
from jax import lax
import numpy as np

D_MODEL = 4096
BATCH = 4
SEQ = 2048
DEPTH = 4
DEC_BATCH = 128
DEC_SEQ = 8
PAST_LEN = 16384
PAGE_SIZE = 128

N_EVEN = (DEPTH + 1) // 2
N_ODD = DEPTH // 2
MIX_WIDTH = D_MODEL
GROUP_WIDTH = MIX_WIDTH // 2
EPS = 1e-6
F_MIN = 1e-30
CHUNK = 64

HG_HEADS = 16
HG_DK = 128
HG_DV = GROUP_WIDTH // HG_HEADS
HG_F = HG_HEADS * HG_DK

SSM_HEADDIM = 64
SSM_HEADS = GROUP_WIDTH // SSM_HEADDIM
SSM_STATE = 128
SSM_GROUPS = 4
SSM_HPG = SSM_HEADS // SSM_GROUPS
SSM_CONV = 4
SSM_CONV_DIM = GROUP_WIDTH + 2 * SSM_GROUPS * SSM_STATE

GLA_HEADS = 4
GLA_KEY = GROUP_WIDTH // 2
GLA_DK = GLA_KEY // GLA_HEADS
GLA_DV = GROUP_WIDTH // GLA_HEADS
GLA_GATE_RANK = 16
GLA_GATE_NORMALIZER = 16.0

LRU_WIDTH = GROUP_WIDTH
LRU_BLOCKS = 16
LRU_BLOCK = LRU_WIDTH // LRU_BLOCKS
LRU_CONV = 4
LRU_C = 8.0

D_FF = ((8 * D_MODEL + 3 * 256 - 1) // (3 * 256)) * 256

EVEN_SIZES = (HG_F, HG_F, GROUP_WIDTH, GROUP_WIDTH,
              GROUP_WIDTH, SSM_CONV_DIM, SSM_HEADS)
ODD_SIZES = (GLA_KEY, GLA_KEY, GROUP_WIDTH, GROUP_WIDTH, GLA_GATE_RANK,
             LRU_WIDTH, LRU_WIDTH)
EVEN_IN = sum(EVEN_SIZES)
ODD_IN = sum(ODD_SIZES)

kernel_name = "hybrid_hgrn2_mamba2_gla_rglru_decode_step"


def split_cols(p, sizes):
    out, start = [], 0
    for s in sizes:
        out.append(p[..., start:start + s])
        start += s
    return out


def rmsnorm(x, w):
    xf = x.astype(jnp.float32)
    y = xf * lax.rsqrt(jnp.mean(xf * xf, axis=-1, keepdims=True) + EPS)
    return (y * w.astype(jnp.float32)).astype(x.dtype)


def gated_rmsnorm(o, g, w):
    y = o * lax.rsqrt(jnp.mean(o * o, axis=-1, keepdims=True) + EPS)
    return y * w.astype(jnp.float32) * jax.nn.silu(g.astype(jnp.float32))


def causal_conv(x, buf, w, b):
    K = w.shape[0]
    T = x.shape[1]
    xp = jnp.concatenate([buf.astype(x.dtype), x], axis=1)
    y = b
    for k in range(K):
        y = y + xp[:, k:k + T] * w[k]
    return y, xp[:, T:]


def _chunk(a, L):
    B, T = a.shape[:2]
    n = -(-T // L)
    a = jnp.pad(a, [(0, 0), (0, n * L - T)] + [(0, 0)] * (a.ndim - 2))
    return jnp.moveaxis(a.reshape((B, n, L) + a.shape[2:]), 1, 0)


def _unchunk(a, T):
    a = jnp.moveaxis(a, 0, 1)
    return a.reshape((a.shape[0], -1) + a.shape[3:])[:, :T]


def _masked_decay(rel, causal):
    return jnp.where(causal, jnp.exp(jnp.where(causal, rel, 0.0)), 0.0)


def chunked_gated_linear_attn(q, k, v, log_a, s0):
    T = q.shape[1]
    L = min(CHUNK, T)
    causal = jnp.tril(jnp.ones((L, L), dtype=bool))[None, :, :, None, None]

    def step(S, blk):
        qc, kc, vc, gc = blk
        b = jnp.cumsum(gc, axis=1)
        dec = _masked_decay(b[:, :, None] - b[:, None, :], causal)
        scores = jnp.einsum('bthd,bshd,btshd->bhts', qc, kc, dec)
        o = (jnp.einsum('bhts,bshv->bthv', scores, vc)
             + jnp.einsum('bthd,bhdv->bthv', qc * jnp.exp(b), S))
        b_last = b[:, -1]
        S = (jnp.exp(b_last)[..., None] * S
             + jnp.einsum('bshd,bshv->bhdv', kc * jnp.exp(b_last[:, None] - b), vc))
        return S, o

    blocks = (_chunk(q, L), _chunk(k, L), _chunk(v, L), _chunk(log_a, L))
    S, o = lax.scan(step, s0, blocks)
    return _unchunk(o, T), S


def chunked_ssd(x, dt, a, bm, cm, h0):
    T = x.shape[1]
    L = min(CHUNK, T)
    causal = jnp.tril(jnp.ones((L, L), dtype=bool))[None, :, :, None, None]

    def step(h, blk):
        xc, dtc, bc, cc = blk
        cum = jnp.cumsum(dtc * a, axis=1)
        decay = _masked_decay(cum[:, :, None] - cum[:, None, :], causal)
        cb = jnp.einsum('btgn,bsgn->btsg', cc, bc)
        u = xc * dtc[..., None]
        y = jnp.einsum('btsg,btsge,bsgep->btgep', cb, decay, u)
        y = y + jnp.einsum('btgn,bgepn->btgep', cc, h) * jnp.exp(cum)[..., None]
        last = cum[:, -1]
        h = (jnp.exp(last)[..., None, None] * h
             + jnp.einsum('bsgn,bsgep->bgepn', bc, u * jnp.exp(last[:, None] - cum)[..., None]))
        return h, y

    blocks = (_chunk(x, L), _chunk(dt, L), _chunk(bm, L), _chunk(cm, L))
    h, y = lax.scan(step, h0, blocks)
    return _unchunk(y, T), h


def _linear_combine(left, right):
    a1, b1 = left
    a2, b2 = right
    return a1 * a2, a2 * b1 + b2


def hgrn2_mixer(q, f, i, g, lb, norm_w, s0):
    B, T, _ = q.shape
    zf = f.astype(jnp.float32)
    fg = lb + (1.0 - lb) * jax.nn.sigmoid(zf)
    log_f = jnp.log(jnp.maximum(fg, F_MIN))
    k = 1.0 - fg
    qf = jax.nn.silu(q.astype(jnp.float32)) * (HG_DK ** -0.5)
    hk = lambda t: t.reshape(B, T, HG_HEADS, HG_DK)
    hv = lambda t: t.reshape(B, T, HG_HEADS, HG_DV)
    o, s = chunked_gated_linear_attn(hk(qf), hk(k), hv(i.astype(jnp.float32)), hk(log_f),
                                     s0.astype(jnp.float32))
    o = gated_rmsnorm(o, hv(g), norm_w)
    return o.reshape(B, T, GROUP_WIDTH), s


def mamba2_mixer(z, xbc, dt_raw, conv_w, conv_b, dt_bias, a_log, d_skip, norm_w, h0, conv_buf):
    B, T, _ = z.shape
    xbc, new_buf = causal_conv(xbc, conv_buf, conv_w, conv_b)
    xbc = jax.nn.silu(xbc.astype(jnp.float32))
    xs, bm, cm = split_cols(xbc, (GROUP_WIDTH, SSM_GROUPS * SSM_STATE, SSM_GROUPS * SSM_STATE))
    xs = xs.reshape(B, T, SSM_GROUPS, SSM_HPG, SSM_HEADDIM)
    bm = bm.reshape(B, T, SSM_GROUPS, SSM_STATE)
    cm = cm.reshape(B, T, SSM_GROUPS, SSM_STATE)
    dt = jax.nn.softplus(dt_raw.astype(jnp.float32) + dt_bias.astype(jnp.float32))
    dt = dt.reshape(B, T, SSM_GROUPS, SSM_HPG)
    a = -jnp.exp(a_log.astype(jnp.float32)).reshape(SSM_GROUPS, SSM_HPG)
    h0 = h0.astype(jnp.float32).reshape(B, SSM_GROUPS, SSM_HPG, SSM_HEADDIM, SSM_STATE)
    y, h = chunked_ssd(xs, dt, a, bm, cm, h0)
    y = y + d_skip.astype(jnp.float32).reshape(SSM_GROUPS, SSM_HPG, 1) * xs
    y = y.reshape(B, T, GROUP_WIDTH) * jax.nn.silu(z.astype(jnp.float32))
    yg = y.reshape(B, T, SSM_GROUPS, GROUP_WIDTH // SSM_GROUPS)
    yg = yg * lax.rsqrt(jnp.mean(yg * yg, axis=-1, keepdims=True) + EPS)
    y = yg.reshape(B, T, GROUP_WIDTH) * norm_w.astype(jnp.float32)
    return y, h.reshape(B, SSM_HEADS, SSM_HEADDIM, SSM_STATE), new_buf


def gla_mixer(q, k, v, g, gate_low, gate_w2, gate_b, norm_w, s0):
    B, T, _ = q.shape
    log_a = jax.nn.log_sigmoid(gate_low.astype(jnp.float32) @ gate_w2.astype(jnp.float32)
                               + gate_b.astype(jnp.float32)) / GLA_GATE_NORMALIZER
    hk = lambda t: t.reshape(B, T, GLA_HEADS, GLA_DK)
    hv = lambda t: t.reshape(B, T, GLA_HEADS, GLA_DV)
    o, s = chunked_gated_linear_attn(hk(q.astype(jnp.float32) * (GLA_DK ** -0.5)), hk(k.astype(jnp.float32)),
                                     hv(v.astype(jnp.float32)), hk(log_a), s0.astype(jnp.float32))
    o = gated_rmsnorm(o, hv(g), norm_w)
    return o.reshape(B, T, GROUP_WIDTH), s


def rglru_mixer(xb, yb, conv_w, conv_b, wa, ba, wx, bx, lam, h0, conv_buf):
    B, T, _ = xb.shape
    xc, new_buf = causal_conv(xb, conv_buf, conv_w, conv_b)
    xc = xc.astype(jnp.float32)
    xblk = xc.reshape(B, T, LRU_BLOCKS, LRU_BLOCK)
    gate_r = jnp.einsum('btni,nij->btnj', xblk, wa.astype(jnp.float32)).reshape(B, T, LRU_WIDTH)
    gate_i = jnp.einsum('btni,nij->btnj', xblk, wx.astype(jnp.float32)).reshape(B, T, LRU_WIDTH)
    r = jax.nn.sigmoid(gate_r + ba.astype(jnp.float32))
    ig = jax.nn.sigmoid(gate_i + bx.astype(jnp.float32))
    log_a = -LRU_C * r * jax.nn.softplus(-lam.astype(jnp.float32))
    a = jnp.exp(log_a)
    u = jnp.sqrt(-jnp.expm1(2.0 * log_a)) * (ig * xc)
    u = u.at[:, 0].add(a[:, 0] * h0.astype(jnp.float32))
    _, h = lax.associative_scan(_linear_combine, (a, u), axis=1)
    out = h * jax.nn.gelu(yb.astype(jnp.float32))
    return out, h[:, -1], new_buf


def setup_inputs(seed: int = 0) -> dict:
    key = jax.random.key(seed)
    ks = iter(jax.random.split(key, 64))

    def nrm(shape, scale):
        return jax.random.normal(next(ks), shape, jnp.float32) * scale

    def gain(shape):
        return 1.0 + nrm(shape, 0.02)

    def unif(shape, lo, hi):
        return jax.random.uniform(next(ks), shape, jnp.float32, lo, hi)

    dt0 = jnp.exp(unif((N_EVEN, SSM_HEADS), float(np.log(1e-3)), float(np.log(1e-1))))
    a0 = unif((N_ODD, LRU_WIDTH), 0.9, 0.999) ** (1.0 / LRU_C)
    return {
        "x_prompt": nrm((BATCH, SEQ, D_MODEL), 1.0),
        "x_sample": nrm((DEC_BATCH, DEC_SEQ, D_MODEL), 1.0),
        "state_hgrn": nrm((N_EVEN, DEC_BATCH, HG_HEADS, HG_DK, HG_DV), 0.5),
        "state_ssm": nrm((N_EVEN, DEC_BATCH, SSM_HEADS, SSM_HEADDIM, SSM_STATE), 0.5),
        "state_ssm_conv": nrm((N_EVEN, DEC_BATCH, SSM_CONV - 1, SSM_CONV_DIM), 1.0),
        "state_gla": nrm((N_ODD, DEC_BATCH, GLA_HEADS, GLA_DK, GLA_DV), 1.0),
        "state_lru": nrm((N_ODD, DEC_BATCH, LRU_WIDTH), 0.5),
        "state_lru_conv": nrm((N_ODD, DEC_BATCH, LRU_CONV - 1, LRU_WIDTH), 1.0),
        "norm_mix_pre": gain((DEPTH, D_MODEL)),
        "norm_mix_post": gain((DEPTH, D_MODEL)),
        "norm_ffn_pre": gain((DEPTH, D_MODEL)),
        "norm_ffn_post": gain((DEPTH, D_MODEL)),
        "w_in_even": nrm((N_EVEN, D_MODEL, EVEN_IN), D_MODEL ** -0.5),
        "w_out_even": nrm((N_EVEN, MIX_WIDTH, D_MODEL), MIX_WIDTH ** -0.5),
        "hgrn_lb_logits": nrm((N_EVEN, HG_F), 0.5),
        "hgrn_norm_w": gain((N_EVEN, HG_DV)),
        "ssm_conv_w": nrm((N_EVEN, SSM_CONV, SSM_CONV_DIM), SSM_CONV ** -0.5),
        "ssm_conv_b": nrm((N_EVEN, SSM_CONV_DIM), 0.02),
        "ssm_dt_bias": dt0 + jnp.log(-jnp.expm1(-dt0)),
        "ssm_a_log": jnp.log(unif((N_EVEN, SSM_HEADS), 1.0, 16.0)),
        "ssm_d": gain((N_EVEN, SSM_HEADS)),
        "ssm_norm_w": gain((N_EVEN, GROUP_WIDTH)),
        "w_in_odd": nrm((N_ODD, D_MODEL, ODD_IN), D_MODEL ** -0.5),
        "w_out_odd": nrm((N_ODD, MIX_WIDTH, D_MODEL), MIX_WIDTH ** -0.5),
        "gla_gate_w2": nrm((N_ODD, GLA_GATE_RANK, GLA_KEY), GLA_GATE_RANK ** -0.5),
        "gla_gate_b": nrm((N_ODD, GLA_KEY), 0.1),
        "gla_norm_w": gain((N_ODD, GLA_DV)),
        "lru_conv_w": nrm((N_ODD, LRU_CONV, LRU_WIDTH), LRU_CONV ** -0.5),
        "lru_conv_b": nrm((N_ODD, LRU_WIDTH), 0.02),
        "lru_wa": nrm((N_ODD, LRU_BLOCKS, LRU_BLOCK, LRU_BLOCK), LRU_BLOCK ** -0.5),
        "lru_ba": nrm((N_ODD, LRU_WIDTH), 0.02),
        "lru_wx": nrm((N_ODD, LRU_BLOCKS, LRU_BLOCK, LRU_BLOCK), LRU_BLOCK ** -0.5),
        "lru_bx": nrm((N_ODD, LRU_WIDTH), 0.02),
        "lru_lambda": jnp.log(a0) - jnp.log1p(-a0),
        "ffn_w_gate": nrm((DEPTH, D_MODEL, D_FF), D_MODEL ** -0.5),
        "ffn_w_up": nrm((DEPTH, D_MODEL, D_FF), D_MODEL ** -0.5),
        "ffn_w_down": nrm((DEPTH, D_FF, D_MODEL), D_FF ** -0.5),
    }


def reference(x_prompt, x_sample, state_hgrn, state_ssm, state_ssm_conv, state_gla, state_lru, state_lru_conv,
              norm_mix_pre, norm_mix_post, norm_ffn_pre, norm_ffn_post,
              w_in_even, w_out_even, hgrn_lb_logits, hgrn_norm_w,
              ssm_conv_w, ssm_conv_b, ssm_dt_bias, ssm_a_log, ssm_d, ssm_norm_w,
              w_in_odd, w_out_odd, gla_gate_w2, gla_gate_b, gla_norm_w,
              lru_conv_w, lru_conv_b, lru_wa, lru_ba, lru_wx, lru_bx, lru_lambda,
              ffn_w_gate, ffn_w_up, ffn_w_down):
    lb_soft = jax.nn.softmax(hgrn_lb_logits.astype(jnp.float32), axis=0)
    hgrn_lb = jnp.maximum(jnp.cumsum(lb_soft, axis=0) - lb_soft[0], 0.0)

    def trunk(x, s_hg, s_ssm, s_ssm_cb, s_gla, s_lru, s_lru_cb):
        n_hg, n_ssm, n_ssm_cb, n_gla, n_lru, n_lru_cb = [], [], [], [], [], []
        for l in range(DEPTH):
            j = l // 2
            h = rmsnorm(x, norm_mix_pre[l])
            if l % 2 == 0:
                hq, hf, hi, hgt, sz, sxbc, sdt = split_cols(h @ w_in_even[j], EVEN_SIZES)
                o_a, st_a = hgrn2_mixer(hq, hf, hi, hgt, hgrn_lb[j], hgrn_norm_w[j], s_hg[j])
                o_b, st_b, cb_b = mamba2_mixer(sz, sxbc, sdt, ssm_conv_w[j], ssm_conv_b[j], ssm_dt_bias[j],
                                               ssm_a_log[j], ssm_d[j], ssm_norm_w[j], s_ssm[j], s_ssm_cb[j])
                n_hg.append(st_a.astype(s_hg.dtype))
                n_ssm.append(st_b.astype(s_ssm.dtype))
                n_ssm_cb.append(cb_b.astype(s_ssm_cb.dtype))
                o = jnp.concatenate([o_a, o_b], axis=-1).astype(x.dtype) @ w_out_even[j]
            else:
                gq, gk, gv, ggt, glow, rx, ry = split_cols(h @ w_in_odd[j], ODD_SIZES)
                o_c, st_c = gla_mixer(gq, gk, gv, ggt, glow, gla_gate_w2[j], gla_gate_b[j], gla_norm_w[j], s_gla[j])
                o_d, st_d, cb_d = rglru_mixer(rx, ry, lru_conv_w[j], lru_conv_b[j], lru_wa[j], lru_ba[j],
                                              lru_wx[j], lru_bx[j], lru_lambda[j], s_lru[j], s_lru_cb[j])
                n_gla.append(st_c.astype(s_gla.dtype))
                n_lru.append(st_d.astype(s_lru.dtype))
                n_lru_cb.append(cb_d.astype(s_lru_cb.dtype))
                o = jnp.concatenate([o_c, o_d], axis=-1).astype(x.dtype) @ w_out_odd[j]
            x = x + rmsnorm(o, norm_mix_post[l])
            h = rmsnorm(x, norm_ffn_pre[l])
            f = (jax.nn.silu(h @ ffn_w_gate[l]) * (h @ ffn_w_up[l])) @ ffn_w_down[l]
            x = x + rmsnorm(f, norm_ffn_post[l])
        return (x, jnp.stack(n_hg), jnp.stack(n_ssm), jnp.stack(n_ssm_cb),
                jnp.stack(n_gla), jnp.stack(n_lru), jnp.stack(n_lru_cb))

    bp = x_prompt.shape[0]
    zeros_like_b = lambda s: jnp.zeros((s.shape[0], bp) + s.shape[2:], s.dtype)
    y_prompt, hg_p, ssm_p, ssmcb_p, gla_p, lru_p, lrucb_p = trunk(
        x_prompt, zeros_like_b(state_hgrn), zeros_like_b(state_ssm), zeros_like_b(state_ssm_conv),
        zeros_like_b(state_gla), zeros_like_b(state_lru), zeros_like_b(state_lru_conv))
    y_sample, hg_s, ssm_s, ssmcb_s, gla_s, lru_s, lrucb_s = trunk(
        x_sample, state_hgrn, state_ssm, state_ssm_conv, state_gla, state_lru, state_lru_conv)
    return (y_prompt, y_sample, hg_p, hg_s, ssm_p, ssm_s, ssmcb_p, ssmcb_s,
            gla_p, gla_s, lru_p, lru_s, lrucb_p, lrucb_s)
```

```python
import functools
import math

import numpy as np
import jax
import jax.numpy as jnp
from jax import lax
from jax.experimental import pallas as pl
from jax.experimental.pallas import tpu as pltpu

F32 = jnp.float32
BF16 = jnp.bfloat16
LANE = 128
VMEM_LIMIT = 56 * 1024 * 1024

DOWN_STRIP = 512

EPS = 1e-6
F_MIN = 1e-30
CHUNK = 64
GLA_GATE_NORMALIZER = 16.0
LRU_C = 8.0
SSM_GROUPS = 4
SSM_HPG = 8
SSM_HEADDIM = 64


def _params(n_axes):
    return pltpu.CompilerParams(dimension_semantics=("arbitrary",) * n_axes,
                                vmem_limit_bytes=VMEM_LIMIT)


def _softplus(x):
    return jnp.maximum(x, 0.0) + jnp.log1p(jnp.exp(-jnp.abs(x)))


def _log_sigmoid(x):
    return jnp.minimum(x, 0.0) - jnp.log1p(jnp.exp(-jnp.abs(x)))


def _split3(x):
    hi = x.astype(BF16)
    r1 = x - hi.astype(F32)
    mid = r1.astype(BF16)
    lo = (r1 - mid.astype(F32)).astype(BF16)
    return jnp.concatenate([hi, mid, lo], axis=1)


def _sum3(y, w):
    return y[:, :w] + y[:, w:2 * w] + y[:, 2 * w:3 * w]


def _cat(ref, base, n):
    if n == 1:
        return ref[base]
    return jnp.concatenate([ref[base + j] for j in range(n)], axis=1)


_NT = (((1,), (1,)), ((), ()))
_TN = (((0,), (0,)), ((), ()))


def _prenorm_rows(x_ref, w_ref, h_scr, tm, rows=64):
    w = w_ref[...]

    def body(i, c):
        r0 = pl.multiple_of(i * rows, rows)
        x = x_ref[pl.ds(r0, rows), :]
        ms = jnp.mean(x * x, axis=1, keepdims=True)
        h_scr[pl.ds(r0, rows), :] = (x * lax.rsqrt(ms + EPS) * w).astype(BF16)
        return c

    lax.fori_loop(0, tm // rows, body, 0)


def _postnorm_rows(o_ref, x_ref, w_ref, tm, rows=64):
    w = w_ref[...]

    def body(i, c):
        r0 = pl.multiple_of(i * rows, rows)
        y = o_ref[pl.ds(r0, rows), :]
        ms = jnp.mean(y * y, axis=1, keepdims=True)
        o_ref[pl.ds(r0, rows), :] = x_ref[pl.ds(r0, rows), :] + y * lax.rsqrt(ms + EPS) * w
        return c

    lax.fori_loop(0, tm // rows, body, 0)


def _inproj_kernel(x_ref, nw_ref, w_ref, o_ref, h_scr, *, tm, cb):
    @pl.when(pl.program_id(1) == 0)
    def _():
        _prenorm_rows(x_ref, nw_ref, h_scr, tm)

    r = jnp.dot(h_scr[...], w_ref[...], preferred_element_type=F32)
    for c in range(cb):
        o_ref[c] = r[:, c * LANE:(c + 1) * LANE]


def inproj(x, norm_w, w, *, tm=512, tn=512):
    n, d = x.shape
    cols = w.shape[1]
    cb = tn // LANE
    return pl.pallas_call(
        functools.partial(_inproj_kernel, tm=tm, cb=cb),
        grid=(n // tm, cols // tn),
        in_specs=[pl.BlockSpec((tm, d), lambda i, j: (i, 0)),
                  pl.BlockSpec((1, d), lambda i, j: (0, 0)),
                  pl.BlockSpec((d, tn), lambda i, j: (0, j))],
        out_specs=pl.BlockSpec((cb, tm, LANE), lambda i, j: (j, i, 0)),
        out_shape=jax.ShapeDtypeStruct((cols // LANE, n, LANE), F32),
        scratch_shapes=[pltpu.VMEM((tm, d), BF16)],
        compiler_params=_params(2),
        name="inproj",
    )(x, norm_w.reshape(1, d), w)


def _outproj_kernel(a_ref, w_ref, x_ref, nw_ref, o_ref, *, tm, kb, nk):
    k = pl.program_id(1)
    a = _cat(a_ref, 0, kb).astype(BF16)

    @pl.when(k == 0)
    def _():
        o_ref[...] = jnp.zeros_like(o_ref)

    for n0 in range(0, o_ref.shape[1], DOWN_STRIP):
        o_ref[:, n0:n0 + DOWN_STRIP] += jnp.dot(a, w_ref[:, n0:n0 + DOWN_STRIP],
                                                preferred_element_type=F32)

    @pl.when(k == nk - 1)
    def _():
        _postnorm_rows(o_ref, x_ref, nw_ref, tm)


def outproj(a3, w, x, norm_w, *, tm=512, tk=512):
    n, d = x.shape
    kdim = w.shape[0]
    kb = tk // LANE
    nk = kdim // tk
    return pl.pallas_call(
        functools.partial(_outproj_kernel, tm=tm, kb=kb, nk=nk),
        grid=(n // tm, nk),
        in_specs=[pl.BlockSpec((kb, tm, LANE), lambda i, k: (k, i, 0)),
                  pl.BlockSpec((tk, d), lambda i, k: (k, 0)),
                  pl.BlockSpec((tm, d), lambda i, k: (i, 0)),
                  pl.BlockSpec((1, d), lambda i, k: (0, 0))],
        out_specs=pl.BlockSpec((tm, d), lambda i, k: (i, 0)),
        out_shape=jax.ShapeDtypeStruct((n, d), F32),
        compiler_params=_params(2),
        name="outproj",
    )(a3, w, x, norm_w.reshape(1, d))


def _ffn_kernel(x_ref, pre_ref, wg_ref, wu_ref, wd_ref, post_ref, o_ref, h_scr, *, tm, nf):
    f = pl.program_id(1)

    @pl.when(f == 0)
    def _():
        _prenorm_rows(x_ref, pre_ref, h_scr, tm)

    h = h_scr[...]
    g = jnp.dot(h, wg_ref[...], preferred_element_type=F32)
    u = jnp.dot(h, wu_ref[...], preferred_element_type=F32)
    a = (jax.nn.silu(g) * u).astype(BF16)

    @pl.when(f == 0)
    def _():
        o_ref[...] = jnp.zeros_like(o_ref)

    for n0 in range(0, o_ref.shape[1], DOWN_STRIP):
        o_ref[:, n0:n0 + DOWN_STRIP] += jnp.dot(a, wd_ref[:, n0:n0 + DOWN_STRIP],
                                                preferred_element_type=F32)

    @pl.when(f == nf - 1)
    def _():
        _postnorm_rows(o_ref, x_ref, post_ref, tm)


def ffn(x, pre_w, wg, wu, wd, post_w, *, tm=512, tf=256):
    n, d = x.shape
    dff = wg.shape[1]
    nf = dff // tf
    return pl.pallas_call(
        functools.partial(_ffn_kernel, tm=tm, nf=nf),
        grid=(n // tm, nf),
        in_specs=[pl.BlockSpec((tm, d), lambda i, f: (i, 0)),
                  pl.BlockSpec((1, d), lambda i, f: (0, 0)),
                  pl.BlockSpec((d, tf), lambda i, f: (0, f)),
                  pl.BlockSpec((d, tf), lambda i, f: (0, f)),
                  pl.BlockSpec((tf, d), lambda i, f: (f, 0)),
                  pl.BlockSpec((1, d), lambda i, f: (0, 0))],
        out_specs=pl.BlockSpec((tm, d), lambda i, f: (i, 0)),
        out_shape=jax.ShapeDtypeStruct((n, d), F32),
        scratch_shapes=[pltpu.VMEM((tm, d), BF16)],
        compiler_params=_params(2),
        name="ffn",
    )(x, pre_w.reshape(1, d), wg, wu, wd, post_w.reshape(1, d))


def _gla_consts(nseq, seq_len):
    r = nseq * seq_len
    t = np.arange(r)[:, None]
    s = np.arange(r)[None, :]
    same = (t // seq_len) == (s // seq_len)
    slabs = [same & (s <= t), same & (s > t)]
    masks = []
    c = seq_len // 2
    while c >= 1:
        bound = (t // (2 * c)) * 2 * c + c - 1
        right = (t % (2 * c)) >= c
        slabs.append(np.where(right, (s > bound) & (s <= t), (s > t) & (s <= bound)))
        masks.append(((t // (2 * c)) == (s // (2 * c))) & right & ((s % (2 * c)) < c))
        c //= 2
    a = np.concatenate(slabs, axis=0).astype(np.float32)
    m = np.stack(masks).astype(np.float32)
    return jnp.asarray(a, BF16), jnp.asarray(m, F32)


def _gla_kernel(*refs, mode, nseq, seq_len, hb, dk, dv, nchunks):
    if mode == "hgrn":
        (q_ref, f_ref, v_ref, gt_ref, lb_ref, nw_ref, a_ref, m_ref, s0_ref, _,
         o_ref, so_ref, s_scr) = refs
    else:
        (q_ref, k_ref, v_ref, gt_ref, gl_ref, w2_ref, gb_ref, nw_ref, a_ref, m_ref, s0_ref, _,
         o_ref, so_ref, s_scr) = refs
    dkb, dvb = dk // LANE, dv // LANE
    r = nseq * seq_len
    nlev = int(math.log2(seq_len))
    c = pl.program_id(2)

    @pl.when(c == 0)
    def _():
        s_scr[...] = s0_ref[...]

    a_mat = a_ref[...]
    seq_shift = int(math.log2(seq_len))
    rowk = lax.broadcasted_iota(jnp.int32, (r, dk), 0)
    seq_of_row = lax.broadcasted_iota(jnp.int32, (r, LANE), 0) >> seq_shift
    lane_id = lax.broadcasted_iota(jnp.int32, (r, LANE), 1)
    seqsel = (seq_of_row == lane_id).astype(BF16)
    nw = _cat(nw_ref, 0, dvb)

    def head(h, carry):
        qr = _cat(q_ref, h * dkb, dkb)
        v = _cat(v_ref, h * dvb, dvb)
        gate = _cat(gt_ref, h * dvb, dvb)
        if mode == "hgrn":
            lb = _cat(lb_ref, h * dkb, dkb)
            fg = lb + (1.0 - lb) * jax.nn.sigmoid(_cat(f_ref, h * dkb, dkb))
            g = jnp.log(jnp.maximum(fg, F_MIN))
            k = 1.0 - fg
            q = jax.nn.silu(qr) * (dk ** -0.5)
        else:
            k = _cat(k_ref, h * dkb, dkb)
            q = qr * (dk ** -0.5)
            lin = jnp.dot(gl_ref[0].astype(BF16), w2_ref[h], preferred_element_type=F32)
            g = _log_sigmoid(lin + _cat(gb_ref, h * dkb, dkb)) / GLA_GATE_NORMALIZER

        g3 = _split3(g)
        e = _sum3(jnp.dot(a_mat, g3, preferred_element_type=F32), dk)

        scores = jnp.zeros((r, r), F32)
        for l in range(nlev):
            half = seq_len >> (l + 1)
            right = (rowk & (2 * half - 1)) >= half
            x = (jnp.where(right, q, k) * jnp.exp(e[(2 + l) * r:(3 + l) * r])).astype(BF16)
            scores = scores + m_ref[l] * lax.dot_general(x, x, _NT, preferred_element_type=F32)

        vb = v.astype(BF16)
        o = jnp.dot(scores.astype(BF16), vb, preferred_element_type=F32)
        o = o + jnp.sum(q * k, axis=1, keepdims=True) * v

        qb = q * jnp.exp(e[0:r])
        kb = k * jnp.exp(e[r:2 * r])
        d3 = lax.dot_general(g3, seqsel, _TN, preferred_element_type=F32)
        dcol = d3[0:dk] + d3[dk:2 * dk] + d3[2 * dk:3 * dk]
        for s in range(nseq):
            if nseq == 1:
                qs, ks = qb, kb
            else:
                in_seq = (rowk >> seq_shift) == s
                qs = jnp.where(in_seq, qb, 0.0)
                ks = jnp.where(in_seq, kb, 0.0)
            st = s_scr[s, h]
            o = o + jnp.dot(qs.astype(BF16), st.astype(BF16), preferred_element_type=F32)
            dec = jnp.exp(jnp.broadcast_to(dcol[:, s:s + 1], (dk, dv)))
            s_scr[s, h] = dec * st + lax.dot_general(ks.astype(BF16), vb, _TN,
                                                     preferred_element_type=F32)

        ms = jnp.mean(o * o, axis=1, keepdims=True)
        y = o * lax.rsqrt(ms + EPS) * nw * jax.nn.silu(gate)
        for j in range(dvb):
            o_ref[h * dvb + j] = y[:, j * LANE:(j + 1) * LANE]
        return carry

    lax.fori_loop(0, hb, head, 0)

    @pl.when(c == nchunks - 1)
    def _():
        so_ref[...] = s_scr[...]


def gla_mixer(mode, p3, o3, s0, params, *, row0, nbatch, t_len, nseq, hb, heads, dk, dv, offs, out_off):
    seq_len = min(CHUNK, t_len)
    nchunks = t_len // seq_len
    r = nseq * seq_len
    dkb, dvb = dk // LANE, dv // LANE
    nrb = nbatch // nseq
    rb0 = row0 // r
    a_mat, masks = _gla_consts(nseq, seq_len)
    n_rows = p3.shape[1]

    def slab(nblk, off):
        return pl.BlockSpec((nblk, r, LANE),
                            lambda i, hg, c: (off // nblk + hg, rb0 + i * nchunks + c, 0))

    def const(arr):
        nd = arr.ndim
        return pl.BlockSpec(arr.shape, lambda i, hg, c: (0,) * nd)

    def per_head(arr, nblk):
        return pl.BlockSpec((nblk, 1, LANE), lambda i, hg, c: (hg, 0, 0))

    if mode == "hgrn":
        lb, nw = params
        ins = [p3, p3, p3, p3, lb, nw, a_mat, masks, s0]
        specs = [slab(hb * dkb, offs[0]), slab(hb * dkb, offs[1]), slab(hb * dvb, offs[2]),
                 slab(hb * dvb, offs[3]), per_head(lb, hb * dkb), const(nw), const(a_mat),
                 const(masks)]
    else:
        w2, gb, nw = params
        ins = [p3, p3, p3, p3, p3, w2, gb, nw, a_mat, masks, s0]
        specs = [slab(hb * dkb, offs[0]), slab(hb * dkb, offs[1]), slab(hb * dvb, offs[2]),
                 slab(hb * dvb, offs[3]),
                 pl.BlockSpec((1, r, LANE), lambda i, hg, c: (offs[4], rb0 + i * nchunks + c, 0)),
                 pl.BlockSpec((hb, LANE, dk), lambda i, hg, c: (hg, 0, 0)),
                 per_head(gb, hb * dkb), const(nw), const(a_mat), const(masks)]
    specs.append(pl.BlockSpec((nseq, hb, dk, dv), lambda i, hg, c: (i, hg, 0, 0)))
    aliases = {}
    if o3 is not None:
        ins.append(o3)
        specs.append(pl.BlockSpec(memory_space=pl.ANY))
        aliases = {len(ins) - 1: 0}
    else:
        ins.append(jnp.zeros((8, LANE), F32))
        specs.append(pl.BlockSpec(memory_space=pl.ANY))

    out = pl.pallas_call(
        functools.partial(_gla_kernel, mode=mode, nseq=nseq, seq_len=seq_len, hb=hb, dk=dk, dv=dv,
                          nchunks=nchunks),
        grid=(nrb, heads // hb, nchunks),
        in_specs=specs,
        out_specs=[pl.BlockSpec((hb * dvb, r, LANE),
                                lambda i, hg, c: (out_off // (hb * dvb) + hg, rb0 + i * nchunks + c, 0)),
                   pl.BlockSpec((nseq, hb, dk, dv), lambda i, hg, c: (i, hg, 0, 0))],
        out_shape=[jax.ShapeDtypeStruct((32, n_rows, LANE), F32),
                   jax.ShapeDtypeStruct((nbatch, heads, dk, dv), F32)],
        scratch_shapes=[pltpu.VMEM((nseq, hb, dk, dv), F32)],
        input_output_aliases=aliases,
        compiler_params=_params(3),
        name=mode + ("_seq" if nchunks > 1 else "_step"),
    )(*ins)
    return out[0], out[1]


def _conv_block(ref, i, cbi, cw_ref, cb_ref, tail, ext, xcs, nseq, seq_len, act):
    w = cw_ref[cbi]
    b = cb_ref[cbi]
    for s in range(nseq):
        ext[0:8, :] = tail[s, cbi]
        ext[8:8 + seq_len, :] = ref[i, s * seq_len:(s + 1) * seq_len, :]
        y = (b + w[3:4] * ext[8:8 + seq_len, :] + w[2:3] * ext[7:7 + seq_len, :]
             + w[1:2] * ext[6:6 + seq_len, :] + w[0:1] * ext[5:5 + seq_len, :])
        xcs[cbi, s * seq_len:(s + 1) * seq_len, :] = act(y)
        tail[s, cbi] = ext[seq_len:seq_len + 8, :]


def _ssd_kernel(z_ref, xa_ref, xb_ref, xc_ref, dt_ref, cw_ref, cb_ref, dtb_ref, alog_ref, dsk_ref,
                nw_ref, t_ref, tt_ref, cbuf0_ref, h0_ref, _, o_ref, hout_ref, cbout_ref,
                h_scr, tail, xcs, ext, *, nseq, seq_len, nchunks):
    r = nseq * seq_len
    c = pl.program_id(1)

    @pl.when(c == 0)
    def _():
        h_scr[...] = h0_ref[...]
        tail[...] = cbuf0_ref[...]

    for part, ref in enumerate((xa_ref, xb_ref, xc_ref)):
        def body(i, carry, part=part, ref=ref):
            _conv_block(ref, i, part * 8 + i, cw_ref, cb_ref, tail, ext, xcs, nseq, seq_len,
                        jax.nn.silu)
            return carry
        lax.fori_loop(0, 8, body, 0)

    lane = lax.broadcasted_iota(jnp.int32, (r, LANE), 1)
    lo = lane < SSM_HEADDIM
    row_lo = lax.broadcasted_iota(jnp.int32, (LANE, LANE), 0) < SSM_HEADDIM
    seq_of_row = lax.broadcasted_iota(jnp.int32, (r, LANE), 0) >> int(math.log2(seq_len))
    tmat = t_ref[...]
    ttmat = tt_ref[...]
    causal = tmat.astype(F32) > 0.0

    def bcast_col(arr, j):
        return jnp.broadcast_to(arr[:, j:j + 1], (r, LANE))

    def group(g, carry):
        dt = _softplus(dt_ref[g] + dtb_ref[g])
        dta = dt * (-jnp.exp(alog_ref[g]))
        d3 = _split3(dta)
        cum = _sum3(jnp.dot(tmat, d3, preferred_element_type=F32), LANE)
        ct3 = lax.dot_general(d3, ttmat, _TN, preferred_element_type=F32)
        cum_t = ct3[0:LANE] + ct3[LANE:2 * LANE] + ct3[2 * LANE:3 * LANE]
        b_g = xcs[16 + g]
        c_g = xcs[20 + g]
        b_b = b_g.astype(BF16)
        cb_m = lax.dot_general(c_g.astype(BF16), b_b, _NT, preferred_element_type=F32)

        def decay_mat(j):
            rel = (jnp.broadcast_to(cum[:, j:j + 1], (r, r))
                   - jnp.broadcast_to(cum_t[j:j + 1, :], (r, r)))
            dec = jnp.where(causal, jnp.exp(jnp.where(causal, rel, 0.0)), 0.0)
            return (cb_m * dec).astype(BF16)

        ys = []
        for jj in range(4):
            j0, j1 = 2 * jj, 2 * jj + 1
            cbi = g * 4 + jj
            x_cb = xcs[cbi]
            dt_e = jnp.where(lo, bcast_col(dt, j0), bcast_col(dt, j1))
            cum_e = jnp.where(lo, bcast_col(cum, j0), bcast_col(cum, j1))
            u = x_cb * dt_e
            u_b = u.astype(BF16)
            y = jnp.where(lo,
                          jnp.dot(decay_mat(j0), u_b, preferred_element_type=F32),
                          jnp.dot(decay_mat(j1), u_b, preferred_element_type=F32))
            y_in = jnp.zeros((r, LANE), F32)
            for s in range(nseq):
                rl = s * seq_len + seq_len - 1
                h_cb = h_scr[s, cbi]
                if nseq == 1:
                    c_s = c_g
                    rel = cum_e[rl:rl + 1, :] - cum_e
                    uw = u * jnp.exp(rel)
                else:
                    in_seq = seq_of_row == s
                    c_s = jnp.where(in_seq, c_g, 0.0)
                    rel = jnp.where(in_seq, cum_e[rl:rl + 1, :] - cum_e, 0.0)
                    uw = jnp.where(in_seq, u * jnp.exp(rel), 0.0)
                y_in = y_in + lax.dot_general(c_s.astype(BF16), h_cb.astype(BF16), _NT,
                                              preferred_element_type=F32)
                last = jnp.where(row_lo,
                                 jnp.broadcast_to(cum[rl:rl + 1, j0:j0 + 1], (LANE, LANE)),
                                 jnp.broadcast_to(cum[rl:rl + 1, j1:j1 + 1], (LANE, LANE)))
                h_scr[s, cbi] = jnp.exp(last) * h_cb + lax.dot_general(
                    uw.astype(BF16), b_b, _TN, preferred_element_type=F32)
            y = y + y_in * jnp.exp(cum_e) + dsk_ref[cbi] * x_cb
            ys.append(y * jax.nn.silu(z_ref[cbi]))
        ms = sum(jnp.sum(y * y, axis=1, keepdims=True) for y in ys) * (1.0 / (4 * LANE))
        rinv = lax.rsqrt(ms + EPS)
        for jj in range(4):
            o_ref[g * 4 + jj] = ys[jj] * rinv * nw_ref[g * 4 + jj]
        return carry

    lax.fori_loop(0, SSM_GROUPS, group, 0)

    @pl.when(c == nchunks - 1)
    def _():
        hout_ref[...] = h_scr[...]
        cbout_ref[...] = tail[...]


def _tri_consts(nseq, seq_len):
    r = nseq * seq_len
    t = np.arange(r)[:, None]
    s = np.arange(r)[None, :]
    m = (((t // seq_len) == (s // seq_len)) & (s <= t)).astype(np.float32)
    return jnp.asarray(m, BF16), jnp.asarray(m.T, BF16)


def ssd_mixer(p3, o3, h0, cbuf0, params, *, row0, nbatch, t_len, nseq):
    seq_len = min(CHUNK, t_len)
    nchunks = t_len // seq_len
    r = nseq * seq_len
    nrb = nbatch // nseq
    rb0 = row0 // r
    cw, cbias, dtb, alog, dsk, nw = params
    tmat, ttmat = _tri_consts(nseq, seq_len)
    n_rows = p3.shape[1]

    def slab(nblk, blk_idx):
        return pl.BlockSpec((nblk, r, LANE), lambda i, c: (blk_idx, rb0 + i * nchunks + c, 0))

    def const(arr):
        nd = arr.ndim
        return pl.BlockSpec(arr.shape, lambda i, c: (0,) * nd)

    ins = [p3, p3, p3, p3, p3, cw, cbias, dtb, alog, dsk, nw, tmat, ttmat, cbuf0, h0, o3]
    specs = [slab(16, 4), slab(8, 10), slab(8, 11), slab(8, 12), slab(4, 26),
             const(cw), const(cbias), const(dtb), const(alog), const(dsk), const(nw),
             const(tmat), const(ttmat),
             pl.BlockSpec((nseq, 24, 8, LANE), lambda i, c: (i, 0, 0, 0)),
             pl.BlockSpec((nseq, 16, LANE, LANE), lambda i, c: (i, 0, 0, 0)),
             pl.BlockSpec(memory_space=pl.ANY)]
    out = pl.pallas_call(
        functools.partial(_ssd_kernel, nseq=nseq, seq_len=seq_len, nchunks=nchunks),
        grid=(nrb, nchunks),
        in_specs=specs,
        out_specs=[pl.BlockSpec((16, r, LANE), lambda i, c: (1, rb0 + i * nchunks + c, 0)),
                   pl.BlockSpec((nseq, 16, LANE, LANE), lambda i, c: (i, 0, 0, 0)),
                   pl.BlockSpec((nseq, 24, 8, LANE), lambda i, c: (i, 0, 0, 0))],
        out_shape=[jax.ShapeDtypeStruct((32, n_rows, LANE), F32),
                   jax.ShapeDtypeStruct((nbatch, 16, LANE, LANE), F32),
                   jax.ShapeDtypeStruct((nbatch, 24, 8, LANE), F32)],
        scratch_shapes=[pltpu.VMEM((nseq, 16, LANE, LANE), F32),
                        pltpu.VMEM((nseq, 24, 8, LANE), F32),
                        pltpu.VMEM((24, r, LANE), F32),
                        pltpu.VMEM((seq_len + 8, LANE), F32)],
        input_output_aliases={len(ins) - 1: 0},
        compiler_params=_params(2),
        name="ssd_seq" if nchunks > 1 else "ssd_step",
    )(*ins)
    return out[0], out[1], out[2]


def _lru_kernel(x_ref, y_ref, cw_ref, cb_ref, wa_ref, ba_ref, wx_ref, bx_ref, lam_ref, cbuf0_ref,
                h0_ref, _, o_ref, hout_ref, cbout_ref, h_scr, tail, xcs, ext,
                *, nseq, seq_len, nchunks):
    r = nseq * seq_len
    sub = min(seq_len, CHUNK)
    c = pl.program_id(2)

    @pl.when(c == 0)
    def _():
        h_scr[...] = h0_ref[0]
        tail[...] = cbuf0_ref[...]

    _conv_block(x_ref, 0, 0, cw_ref, cb_ref, tail, ext, xcs, nseq, seq_len, lambda v: v)
    xc = xcs[0]
    xc_b = xc.astype(BF16)
    rg = jax.nn.sigmoid(jnp.dot(xc_b, wa_ref[0], preferred_element_type=F32) + ba_ref[0])
    ig = jax.nn.sigmoid(jnp.dot(xc_b, wx_ref[0], preferred_element_type=F32) + bx_ref[0])
    log_a = -LRU_C * rg * _softplus(-lam_ref[0])
    a_cum = jnp.exp(log_a)
    th = jnp.tanh(log_a)
    u_cum = jnp.sqrt(-2.0 * th / (1.0 - th)) * (ig * xc)

    pos = lax.broadcasted_iota(jnp.int32, (r, LANE), 0) & (sub - 1)
    d = 1
    while d < sub:
        take = pos >= d
        a_prev = pltpu.roll(a_cum, d, axis=0)
        u_prev = pltpu.roll(u_cum, d, axis=0)
        u_cum = jnp.where(take, a_cum * u_prev + u_cum, u_cum)
        a_cum = jnp.where(take, a_cum * a_prev, a_cum)
        d *= 2

    gel = jax.nn.gelu(y_ref[0])
    for s in range(nseq):
        h_prev = h_scr[s]
        for sb in range(seq_len // sub):
            r0 = s * seq_len + sb * sub
            hs = a_cum[r0:r0 + sub] * h_prev + u_cum[r0:r0 + sub]
            o_ref[0, r0:r0 + sub, :] = hs * gel[r0:r0 + sub]
            h_prev = hs[sub - 1:sub]
        h_scr[s] = h_prev

    @pl.when(c == nchunks - 1)
    def _():
        hout_ref[0] = h_scr[...]
        cbout_ref[...] = tail[...]


def lru_mixer(p3, o3, h0, cbuf0, params, *, row0, nbatch, t_len, nseq, rows):
    seq_len = min(rows, t_len)
    nchunks = t_len // seq_len
    r = nseq * seq_len
    nrb = nbatch // nseq
    rb0 = row0 // r
    cw, cbias, wa, ba, wx, bx, lam = params
    n_rows = p3.shape[1]

    def slab(off):
        return pl.BlockSpec((1, r, LANE), lambda i, n, c: (off + n, rb0 + i * nchunks + c, 0))

    def per_blk(arr):
        shp = (1,) + arr.shape[1:]
        nd = arr.ndim
        return pl.BlockSpec(shp, lambda i, n, c: (n,) + (0,) * (nd - 1))

    ins = [p3, p3, cw, cbias, wa, ba, wx, bx, lam, cbuf0, h0, o3]
    specs = [slab(48), slab(64), per_blk(cw), per_blk(cbias), per_blk(wa), per_blk(ba),
             per_blk(wx), per_blk(bx), per_blk(lam),
             pl.BlockSpec((nseq, 1, 8, LANE), lambda i, n, c: (i, n, 0, 0)),
             pl.BlockSpec((1, nseq, 1, LANE), lambda i, n, c: (n, i, 0, 0)),
             pl.BlockSpec(memory_space=pl.ANY)]
    out = pl.pallas_call(
        functools.partial(_lru_kernel, nseq=nseq, seq_len=seq_len, nchunks=nchunks),
        grid=(nrb, 16, nchunks),
        in_specs=specs,
        out_specs=[pl.BlockSpec((1, r, LANE), lambda i, n, c: (16 + n, rb0 + i * nchunks + c, 0)),
                   pl.BlockSpec((1, nseq, 1, LANE), lambda i, n, c: (n, i, 0, 0)),
                   pl.BlockSpec((nseq, 1, 8, LANE), lambda i, n, c: (i, n, 0, 0))],
        out_shape=[jax.ShapeDtypeStruct((32, n_rows, LANE), F32),
                   jax.ShapeDtypeStruct((16, nbatch, 1, LANE), F32),
                   jax.ShapeDtypeStruct((nbatch, 16, 8, LANE), F32)],
        scratch_shapes=[pltpu.VMEM((nseq, 1, LANE), F32),
                        pltpu.VMEM((nseq, 1, 8, LANE), F32),
                        pltpu.VMEM((1, r, LANE), F32),
                        pltpu.VMEM((seq_len + 8, LANE), F32)],
        input_output_aliases={len(ins) - 1: 0},
        compiler_params=_params(3),
        name="lru_seq" if nchunks > 1 else "lru_step",
    )(*ins)
    return out[0], out[1], out[2]


def _slabs(v, nblk):
    return v.astype(F32).reshape(nblk, 1, LANE)


def _pad_lanes(a, width=LANE):
    return jnp.pad(a, [(0, 0)] * (a.ndim - 1) + [(0, width - a.shape[-1])])


def _conv_state_in(buf, nblk):
    b = buf.shape[0]
    t = jnp.transpose(buf.astype(F32).reshape(b, 3, nblk, LANE), (0, 2, 1, 3))
    return jnp.pad(t, ((0, 0), (0, 0), (5, 0), (0, 0)))


def _conv_state_out(t):
    b, nblk = t.shape[:2]
    return jnp.transpose(t[:, :, 5:8, :], (0, 2, 1, 3)).reshape(b, 3, nblk * LANE)


def _prep_w_in_even(w):
    d = w.shape[0]
    main = w[:, :13312]
    dt = _pad_lanes(w[:, 13312:13344].reshape(d, SSM_GROUPS, SSM_HPG)).reshape(d, SSM_GROUPS * LANE)
    return jnp.concatenate([main, dt], axis=1).astype(BF16)


def _prep_w_in_odd(w):
    d = w.shape[0]
    return jnp.concatenate([w[:, :6144], w[:, 6160:10256], _pad_lanes(w[:, 6144:6160]),
                            jnp.zeros((d, 3 * LANE), w.dtype)], axis=1).astype(BF16)


def kernel(x_prompt, x_sample, state_hgrn, state_ssm, state_ssm_conv, state_gla, state_lru,
           state_lru_conv, norm_mix_pre, norm_mix_post, norm_ffn_pre, norm_ffn_post,
           w_in_even, w_out_even, hgrn_lb_logits, hgrn_norm_w,
           ssm_conv_w, ssm_conv_b, ssm_dt_bias, ssm_a_log, ssm_d, ssm_norm_w,
           w_in_odd, w_out_odd, gla_gate_w2, gla_gate_b, gla_norm_w,
           lru_conv_w, lru_conv_b, lru_wa, lru_ba, lru_wx, lru_bx, lru_lambda,
           ffn_w_gate, ffn_w_up, ffn_w_down):
    bp, tp, d = x_prompt.shape
    bs, ts, _ = x_sample.shape
    n_p, n_s = bp * tp, bs * ts
    depth = norm_mix_pre.shape[0]
    x = jnp.concatenate([x_prompt.reshape(n_p, d), x_sample.reshape(n_s, d)], axis=0)

    lb_soft = jax.nn.softmax(hgrn_lb_logits.astype(F32), axis=0)
    hgrn_lb = jnp.maximum(jnp.cumsum(lb_soft, axis=0) - lb_soft[0], 0.0)

    paths = (dict(row0=0, nbatch=bp, t_len=tp), dict(row0=n_p, nbatch=bs, t_len=ts))
    outs = {k: ([], []) for k in ("hg", "ssm", "ssm_cb", "gla", "lru", "lru_cb")}

    for l in range(depth):
        j = l // 2
        if l % 2 == 0:
            p3 = inproj(x, norm_mix_pre[l], _prep_w_in_even(w_in_even[j]))
            hg_params = (_slabs(hgrn_lb[j], 16), _slabs(hgrn_norm_w[j], 1))
            dsk = jnp.repeat(ssm_d[j].astype(F32), SSM_HEADDIM)
            ssd_params = (
                jnp.transpose(ssm_conv_w[j].astype(F32).reshape(4, 24, LANE), (1, 0, 2)),
                _slabs(ssm_conv_b[j], 24),
                _pad_lanes(ssm_dt_bias[j].astype(F32).reshape(SSM_GROUPS, 1, SSM_HPG)),
                _pad_lanes(ssm_a_log[j].astype(F32).reshape(SSM_GROUPS, 1, SSM_HPG)),
                _slabs(dsk, 16), _slabs(ssm_norm_w[j], 16))
            o3 = None
            for pi, path in enumerate(paths):
                nb = path["nbatch"]
                if pi == 0:
                    s_hg = jnp.zeros((nb,) + state_hgrn.shape[2:], F32)
                    s_ssm = jnp.zeros((nb, 16, LANE, LANE), F32)
                    s_cb = jnp.zeros((nb, 24, 8, LANE), F32)
                    nseq_a, hb_a, nseq_b = 1, 16, 1
                else:
                    s_hg = state_hgrn[j].astype(F32)
                    s_ssm = state_ssm[j].astype(F32).reshape(nb, 16, LANE, LANE)
                    s_cb = _conv_state_in(state_ssm_conv[j], 24)
                    nseq_a, hb_a, nseq_b = 8, 4, 2
                o3, st_a = gla_mixer("hgrn", p3, o3, s_hg, hg_params, nseq=nseq_a, hb=hb_a,
                                     heads=16, dk=128, dv=128, offs=(0, 16, 32, 48), out_off=0,
                                     **path)
                o3, st_b, cb_b = ssd_mixer(p3, o3, s_ssm, s_cb, ssd_params, nseq=nseq_b, **path)
                outs["hg"][pi].append(st_a.astype(state_hgrn.dtype))
                outs["ssm"][pi].append(st_b.reshape((nb,) + state_ssm.shape[2:]).astype(state_ssm.dtype))
                outs["ssm_cb"][pi].append(_conv_state_out(cb_b).astype(state_ssm_conv.dtype))
            x = outproj(o3, w_out_even[j].astype(BF16), x, norm_mix_post[l])
        else:
            p3 = inproj(x, norm_mix_pre[l], _prep_w_in_odd(w_in_odd[j]))
            w2 = jnp.pad(gla_gate_w2[j], ((0, LANE - gla_gate_w2.shape[1]), (0, 0)))
            w2 = jnp.transpose(w2.reshape(LANE, 4, 256), (1, 0, 2)).astype(BF16)
            gla_params = (w2, _slabs(gla_gate_b[j], 8), _slabs(gla_norm_w[j], 4))
            lru_params = (
                jnp.transpose(lru_conv_w[j].astype(F32).reshape(4, 16, LANE), (1, 0, 2)),
                _slabs(lru_conv_b[j], 16), lru_wa[j].astype(BF16), _slabs(lru_ba[j], 16),
                lru_wx[j].astype(BF16), _slabs(lru_bx[j], 16), _slabs(lru_lambda[j], 16))
            o3 = None
            for pi, path in enumerate(paths):
                nb = path["nbatch"]
                if pi == 0:
                    s_gla = jnp.zeros((nb,) + state_gla.shape[2:], F32)
                    s_lru = jnp.zeros((16, nb, 1, LANE), F32)
                    s_cb = jnp.zeros((nb, 16, 8, LANE), F32)
                    nseq_c, hb_c, nseq_d, rows_d = 1, 4, 1, 256
                else:
                    s_gla = state_gla[j].astype(F32)
                    s_lru = jnp.transpose(state_lru[j].astype(F32).reshape(nb, 16, 1, LANE),
                                          (1, 0, 2, 3))
                    s_cb = _conv_state_in(state_lru_conv[j], 16)
                    nseq_c, hb_c, nseq_d, rows_d = 8, 1, 16, 8
                o3, st_c = gla_mixer("gla", p3, o3, s_gla, gla_params, nseq=nseq_c, hb=hb_c,
                                     heads=4, dk=256, dv=512, offs=(0, 8, 16, 32, 80), out_off=0,
                                     **path)
                o3, st_d, cb_d = lru_mixer(p3, o3, s_lru, s_cb, lru_params, nseq=nseq_d,
                                           rows=rows_d, **path)
                outs["gla"][pi].append(st_c.astype(state_gla.dtype))
                outs["lru"][pi].append(jnp.transpose(st_d, (1, 0, 2, 3)).reshape(nb, 16 * LANE)
                                       .astype(state_lru.dtype))
                outs["lru_cb"][pi].append(_conv_state_out(cb_d).astype(state_lru_conv.dtype))
            x = outproj(o3, w_out_odd[j].astype(BF16), x, norm_mix_post[l])
        x = ffn(x, norm_ffn_pre[l], ffn_w_gate[l].astype(BF16), ffn_w_up[l].astype(BF16),
                ffn_w_down[l].astype(BF16), norm_ffn_post[l])

    y_prompt = x[:n_p].reshape(bp, tp, d).astype(x_prompt.dtype)
    y_sample = x[n_p:].reshape(bs, ts, d).astype(x_sample.dtype)
    res = [y_prompt, y_sample]
    for key in ("hg", "ssm", "ssm_cb", "gla", "lru", "lru_cb"):
        res.append(jnp.stack(outs[key][0]))
        res.append(jnp.stack(outs[key][1]))
    return tuple(res)
```

```python
import functools
import math

import numpy as np
import jax
import jax.numpy as jnp
from jax import lax
from jax.experimental import pallas as pl
from jax.experimental.pallas import tpu as pltpu

F32 = jnp.float32
BF16 = jnp.bfloat16
LANE = 128
VMEM_LIMIT = 56 * 1024 * 1024

DOWN_STRIP = 512

EPS = 1e-6
F_MIN = 1e-30
CHUNK = 64
GLA_GATE_NORMALIZER = 16.0
LRU_C = 8.0
SSM_GROUPS = 4
SSM_HPG = 8
SSM_HEADDIM = 64


def _params(n_axes):
    return pltpu.CompilerParams(dimension_semantics=("arbitrary",) * n_axes,
                                vmem_limit_bytes=VMEM_LIMIT)


def _softplus(x):
    return jnp.maximum(x, 0.0) + jnp.log1p(jnp.exp(-jnp.abs(x)))


def _log_sigmoid(x):
    return jnp.minimum(x, 0.0) - jnp.log1p(jnp.exp(-jnp.abs(x)))


def _split3(x):
    hi = x.astype(BF16)
    r1 = x - hi.astype(F32)
    mid = r1.astype(BF16)
    lo = (r1 - mid.astype(F32)).astype(BF16)
    return jnp.concatenate([hi, mid, lo], axis=1)


def _sum3(y, w):
    return y[:, :w] + y[:, w:2 * w] + y[:, 2 * w:3 * w]


def _cat(ref, base, n):
    if n == 1:
        return ref[base]
    return jnp.concatenate([ref[base + j] for j in range(n)], axis=1)


_NT = (((1,), (1,)), ((), ()))
_TN = (((0,), (0,)), ((), ()))


def _prenorm_rows(x_ref, w_ref, h_scr, tm, rows=64):
    w = w_ref[...]

    def body(i, c):
        r0 = pl.multiple_of(i * rows, rows)
        x = x_ref[pl.ds(r0, rows), :]
        ms = jnp.mean(x * x, axis=1, keepdims=True)
        h_scr[pl.ds(r0, rows), :] = (x * lax.rsqrt(ms + EPS) * w).astype(BF16)
        return c

    lax.fori_loop(0, tm // rows, body, 0)


def _postnorm_rows(o_ref, x_ref, w_ref, tm, rows=64):
    w = w_ref[...]

    def body(i, c):
        r0 = pl.multiple_of(i * rows, rows)
        y = o_ref[pl.ds(r0, rows), :]
        ms = jnp.mean(y * y, axis=1, keepdims=True)
        o_ref[pl.ds(r0, rows), :] = x_ref[pl.ds(r0, rows), :] + y * lax.rsqrt(ms + EPS) * w
        return c

    lax.fori_loop(0, tm // rows, body, 0)


def _resident(shape, index_map):
    return pl.BlockSpec(shape, index_map, pipeline_mode=pl.Buffered(1))


def _inproj_kernel(x_ref, nw_ref, *rest, tm, segs):
    w_refs, o_ref, h_scr = rest[:len(segs)], rest[len(segs)], rest[len(segs) + 1]
    j = pl.program_id(1)

    @pl.when(j == 0)
    def _():
        _prenorm_rows(x_ref, nw_ref, h_scr, tm)

    for w_ref, (start, ntiles, width) in zip(w_refs, segs):
        @pl.when((j >= start) & (j < start + ntiles))
        def _(w_ref=w_ref, width=width):
            r = jnp.dot(h_scr[...], w_ref[0], preferred_element_type=F32)
            for c in range(width // LANE):
                o_ref[c] = r[:, c * LANE:(c + 1) * LANE]


def inproj(x, norm_w, segments, *, tm=512, tn=1024):
    n, d = x.shape
    cb = tn // LANE
    segs, specs, start = [], [], 0
    for w, layer, ntiles, width in segments:
        segs.append((start, ntiles, width))
        specs.append(pl.BlockSpec(
            (1, d, width),
            lambda i, j, layer=layer, start=start, ntiles=ntiles: (layer, 0, jnp.clip(j - start, 0, ntiles - 1))))
        start += ntiles
    return pl.pallas_call(
        functools.partial(_inproj_kernel, tm=tm, segs=tuple(segs)),
        grid=(n // tm, start),
        in_specs=[_resident((tm, d), lambda i, j: (i, 0)),
                  pl.BlockSpec((1, d), lambda i, j: (0, 0))] + specs,
        out_specs=pl.BlockSpec((cb, tm, LANE), lambda i, j: (j, i, 0)),
        out_shape=jax.ShapeDtypeStruct((start * cb, n, LANE), F32),
        scratch_shapes=[pltpu.VMEM((tm, d), BF16)],
        compiler_params=_params(2),
        name="inproj",
    )(x, norm_w.reshape(1, d), *[s[0] for s in segments])


def _outproj_kernel(a_ref, w3_ref, x_ref, nw_ref, o_ref, *, tm, kb, nk):
    k = pl.program_id(1)
    w_ref = w3_ref.at[0]
    a = _cat(a_ref, 0, kb).astype(BF16)

    @pl.when(k == 0)
    def _():
        o_ref[...] = jnp.zeros_like(o_ref)

    for n0 in range(0, o_ref.shape[1], DOWN_STRIP):
        o_ref[:, n0:n0 + DOWN_STRIP] += jnp.dot(a, w_ref[:, n0:n0 + DOWN_STRIP],
                                                preferred_element_type=F32)

    @pl.when(k == nk - 1)
    def _():
        _postnorm_rows(o_ref, x_ref, nw_ref, tm)


def outproj(a3, w, layer, x, norm_w, *, tm=512, tk=512):
    n, d = x.shape
    kdim = w.shape[1]
    kb = tk // LANE
    nk = kdim // tk
    return pl.pallas_call(
        functools.partial(_outproj_kernel, tm=tm, kb=kb, nk=nk),
        grid=(n // tm, nk),
        in_specs=[pl.BlockSpec((kb, tm, LANE), lambda i, k: (k, i, 0)),
                  pl.BlockSpec((1, tk, d), lambda i, k: (layer, k, 0)),
                  _resident((tm, d), lambda i, k: (i, 0)),
                  pl.BlockSpec((1, d), lambda i, k: (0, 0))],
        out_specs=pl.BlockSpec((tm, d), lambda i, k: (i, 0)),
        out_shape=jax.ShapeDtypeStruct((n, d), F32),
        compiler_params=_params(2),
        name="outproj",
    )(a3, w, x, norm_w.reshape(1, d))


def _ffn_kernel(x_ref, pre_ref, wg_ref, wu_ref, wd3_ref, post_ref, o_ref, h_scr, *, tm, nf):
    f = pl.program_id(1)
    wd_ref = wd3_ref.at[0]

    @pl.when(f == 0)
    def _():
        _prenorm_rows(x_ref, pre_ref, h_scr, tm)

    h = h_scr[...]
    g = jnp.dot(h, wg_ref[0], preferred_element_type=F32)
    u = jnp.dot(h, wu_ref[0], preferred_element_type=F32)
    a = (jax.nn.silu(g) * u).astype(BF16)

    @pl.when(f == 0)
    def _():
        o_ref[...] = jnp.zeros_like(o_ref)

    for n0 in range(0, o_ref.shape[1], DOWN_STRIP):
        o_ref[:, n0:n0 + DOWN_STRIP] += jnp.dot(a, wd_ref[:, n0:n0 + DOWN_STRIP],
                                                preferred_element_type=F32)

    @pl.when(f == nf - 1)
    def _():
        _postnorm_rows(o_ref, x_ref, post_ref, tm)


def ffn(x, pre_w, wg, wu, wd, layer, post_w, *, tm=512, tf=256):
    n, d = x.shape
    dff = wg.shape[2]
    nf = dff // tf
    return pl.pallas_call(
        functools.partial(_ffn_kernel, tm=tm, nf=nf),
        grid=(n // tm, nf),
        in_specs=[_resident((tm, d), lambda i, f: (i, 0)),
                  pl.BlockSpec((1, d), lambda i, f: (0, 0)),
                  pl.BlockSpec((1, d, tf), lambda i, f: (layer, 0, f)),
                  pl.BlockSpec((1, d, tf), lambda i, f: (layer, 0, f)),
                  pl.BlockSpec((1, tf, d), lambda i, f: (layer, f, 0)),
                  pl.BlockSpec((1, d), lambda i, f: (0, 0))],
        out_specs=pl.BlockSpec((tm, d), lambda i, f: (i, 0)),
        out_shape=jax.ShapeDtypeStruct((n, d), F32),
        scratch_shapes=[pltpu.VMEM((tm, d), BF16)],
        compiler_params=_params(2),
        name="ffn",
    )(x, pre_w.reshape(1, d), wg, wu, wd, post_w.reshape(1, d))


def _gla_consts(nseq, seq_len):
    r = nseq * seq_len
    t = np.arange(r)[:, None]
    s = np.arange(r)[None, :]
    same = (t // seq_len) == (s // seq_len)
    slabs = [same & (s <= t), same & (s > t)]
    masks = []
    c = seq_len // 2
    while c >= 1:
        bound = (t // (2 * c)) * 2 * c + c - 1
        right = (t % (2 * c)) >= c
        slabs.append(np.where(right, (s > bound) & (s <= t), (s > t) & (s <= bound)))
        masks.append(((t // (2 * c)) == (s // (2 * c))) & right & ((s % (2 * c)) < c))
        c //= 2
    a = np.concatenate(slabs, axis=0).astype(np.float32)
    m = np.stack(masks).astype(np.float32)
    return jnp.asarray(a, BF16), jnp.asarray(m, F32)


def _gla_kernel(*refs, mode, nseq, seq_len, hb, dk, dv, nchunks, unroll):
    if mode == "hgrn":
        (q_ref, f_ref, v_ref, gt_ref, lb_ref, nw_ref, a_ref, m_ref, s0_ref, _, _,
         o_ref, so_ref, s_scr) = refs
    else:
        (q_ref, k_ref, v_ref, gt_ref, gl_ref, w2_ref, gb_ref, nw_ref, a_ref, m_ref, s0_ref, _, _,
         o_ref, so_ref, s_scr) = refs
    dkb, dvb = dk // LANE, dv // LANE
    r = nseq * seq_len
    nlev = int(math.log2(seq_len))
    c = pl.program_id(2)

    @pl.when(c == 0)
    def _():
        s_scr[...] = s0_ref[0]

    a_mat = a_ref[...]
    seq_shift = int(math.log2(seq_len))
    rowk = lax.broadcasted_iota(jnp.int32, (r, dk), 0)
    seq_of_row = lax.broadcasted_iota(jnp.int32, (r, LANE), 0) >> seq_shift
    lane_id = lax.broadcasted_iota(jnp.int32, (r, LANE), 1)
    seqsel = (seq_of_row == lane_id).astype(BF16)
    nw = _cat(nw_ref, 0, dvb)

    def head(h, carry):
        qr = _cat(q_ref, h * dkb, dkb)
        v = _cat(v_ref, h * dvb, dvb)
        gate = _cat(gt_ref, h * dvb, dvb)
        if mode == "hgrn":
            lb = _cat(lb_ref, h * dkb, dkb)
            fg = lb + (1.0 - lb) * jax.nn.sigmoid(_cat(f_ref, h * dkb, dkb))
            g = jnp.log(jnp.maximum(fg, F_MIN))
            k = 1.0 - fg
            q = jax.nn.silu(qr) * (dk ** -0.5)
        else:
            k = _cat(k_ref, h * dkb, dkb)
            q = qr * (dk ** -0.5)
            lin = jnp.dot(gl_ref[0].astype(BF16), w2_ref[h], preferred_element_type=F32)
            g = _log_sigmoid(lin + _cat(gb_ref, h * dkb, dkb)) / GLA_GATE_NORMALIZER

        g3 = _split3(g)
        e = _sum3(jnp.dot(a_mat, g3, preferred_element_type=F32), dk)

        scores = jnp.zeros((r, r), F32)
        for l in range(nlev):
            half = seq_len >> (l + 1)
            right = (rowk & (2 * half - 1)) >= half
            x = (jnp.where(right, q, k) * jnp.exp(e[(2 + l) * r:(3 + l) * r])).astype(BF16)
            scores = scores + m_ref[l] * lax.dot_general(x, x, _NT, preferred_element_type=F32)

        vb = v.astype(BF16)
        o = jnp.dot(scores.astype(BF16), vb, preferred_element_type=F32)
        o = o + jnp.sum(q * k, axis=1, keepdims=True) * v

        qb = q * jnp.exp(e[0:r])
        kb = k * jnp.exp(e[r:2 * r])
        d3 = lax.dot_general(g3, seqsel, _TN, preferred_element_type=F32)
        dcol = d3[0:dk] + d3[dk:2 * dk] + d3[2 * dk:3 * dk]
        for s in range(nseq):
            if nseq == 1:
                qs, ks = qb, kb
            else:
                in_seq = (rowk >> seq_shift) == s
                qs = jnp.where(in_seq, qb, 0.0)
                ks = jnp.where(in_seq, kb, 0.0)
            st = s_scr[s, h]
            o = o + jnp.dot(qs.astype(BF16), st.astype(BF16), preferred_element_type=F32)
            dec = jnp.exp(jnp.broadcast_to(dcol[:, s:s + 1], (dk, dv)))
            s_scr[s, h] = dec * st + lax.dot_general(ks.astype(BF16), vb, _TN,
                                                     preferred_element_type=F32)

        ms = jnp.mean(o * o, axis=1, keepdims=True)
        y = o * lax.rsqrt(ms + EPS) * nw * jax.nn.silu(gate)
        for j in range(dvb):
            o_ref[h * dvb + j] = y[:, j * LANE:(j + 1) * LANE]
        return carry

    lax.fori_loop(0, hb, head, 0, unroll=unroll)

    @pl.when(c == nchunks - 1)
    def _():
        so_ref[0] = s_scr[...]


def _alias_or_dummy(arr, ins, specs, aliases, out_idx):
    if arr is None:
        ins.append(jnp.zeros((8, LANE), F32))
    else:
        ins.append(arr)
        aliases[len(ins) - 1] = out_idx
    specs.append(pl.BlockSpec(memory_space=pl.ANY))


def gla_mixer(mode, p3, o3, s0, s0_layer, st_all, layer, n_layers, params, *, row0, nbatch, t_len,
              nseq, hb, heads, dk, dv, offs, out_off, unroll):
    seq_len = min(CHUNK, t_len)
    nchunks = t_len // seq_len
    r = nseq * seq_len
    dkb, dvb = dk // LANE, dv // LANE
    nrb = nbatch // nseq
    rb0 = row0 // r
    a_mat, masks = _gla_consts(nseq, seq_len)
    n_rows = p3.shape[1]

    def slab(nblk, off):
        return pl.BlockSpec((nblk, r, LANE),
                            lambda i, hg, c: (off // nblk + hg, rb0 + i * nchunks + c, 0))

    def const(arr):
        nd = arr.ndim
        return pl.BlockSpec(arr.shape, lambda i, hg, c: (0,) * nd)

    def per_head(arr, nblk):
        return pl.BlockSpec((nblk, 1, LANE), lambda i, hg, c: (hg, 0, 0))

    if mode == "hgrn":
        lb, nw = params
        ins = [p3, p3, p3, p3, lb, nw, a_mat, masks, s0]
        specs = [slab(hb * dkb, offs[0]), slab(hb * dkb, offs[1]), slab(hb * dvb, offs[2]),
                 slab(hb * dvb, offs[3]), per_head(lb, hb * dkb), const(nw), const(a_mat),
                 const(masks)]
    else:
        w2, gb, nw = params
        ins = [p3, p3, p3, p3, p3, w2, gb, nw, a_mat, masks, s0]
        specs = [slab(hb * dkb, offs[0]), slab(hb * dkb, offs[1]), slab(hb * dvb, offs[2]),
                 slab(hb * dvb, offs[3]),
                 pl.BlockSpec((1, r, LANE), lambda i, hg, c: (offs[4], rb0 + i * nchunks + c, 0)),
                 pl.BlockSpec((hb, LANE, dk), lambda i, hg, c: (hg, 0, 0)),
                 per_head(gb, hb * dkb), const(nw), const(a_mat), const(masks)]
    specs.append(pl.BlockSpec((1, nseq, hb, dk, dv), lambda i, hg, c: (s0_layer, i, hg, 0, 0)))
    aliases = {}
    _alias_or_dummy(o3, ins, specs, aliases, 0)
    _alias_or_dummy(st_all, ins, specs, aliases, 1)

    out = pl.pallas_call(
        functools.partial(_gla_kernel, mode=mode, nseq=nseq, seq_len=seq_len, hb=hb, dk=dk, dv=dv,
                          nchunks=nchunks, unroll=unroll),
        grid=(nrb, heads // hb, nchunks),
        in_specs=specs,
        out_specs=[pl.BlockSpec((hb * dvb, r, LANE),
                                lambda i, hg, c: (out_off // (hb * dvb) + hg, rb0 + i * nchunks + c, 0)),
                   pl.BlockSpec((1, nseq, hb, dk, dv), lambda i, hg, c: (layer, i, hg, 0, 0))],
        out_shape=[jax.ShapeDtypeStruct((32, n_rows, LANE), F32),
                   jax.ShapeDtypeStruct((n_layers, nbatch, heads, dk, dv), F32)],
        scratch_shapes=[pltpu.VMEM((nseq, hb, dk, dv), F32)],
        input_output_aliases=aliases,
        compiler_params=_params(3),
        name=mode + ("_seq" if nchunks > 1 else "_step"),
    )(*ins)
    return out[0], out[1]


def _conv_block(ref, i, cbi, cw_ref, cb_ref, tail, ext, xcs, nseq, seq_len, act):
    w = cw_ref[cbi]
    b = cb_ref[cbi]
    for s in range(nseq):
        ext[0:8, :] = tail[s, cbi]
        ext[8:8 + seq_len, :] = ref[i, s * seq_len:(s + 1) * seq_len, :]
        y = (b + w[3:4] * ext[8:8 + seq_len, :] + w[2:3] * ext[7:7 + seq_len, :]
             + w[1:2] * ext[6:6 + seq_len, :] + w[0:1] * ext[5:5 + seq_len, :])
        xcs[cbi, s * seq_len:(s + 1) * seq_len, :] = act(y)
        tail[s, cbi] = ext[seq_len:seq_len + 8, :]


def _ssd_kernel(z_ref, xa_ref, xb_ref, xc_ref, dt_ref, cw_ref, cb_ref, dtb_ref, alog_ref, dsk_ref,
                nw_ref, t_ref, tt_ref, cbuf0_ref, h0_ref, _o3_any, _h_any, o_ref, hout_ref, cbout_ref,
                h_scr, tail, xcs, ext, *, nseq, seq_len, nchunks):
    r = nseq * seq_len
    c = pl.program_id(1)

    @pl.when(c == 0)
    def _():
        h_scr[...] = h0_ref[0]
        tail[...] = cbuf0_ref[...]

    for part, ref in enumerate((xa_ref, xb_ref, xc_ref)):
        def body(i, carry, part=part, ref=ref):
            _conv_block(ref, i, part * 8 + i, cw_ref, cb_ref, tail, ext, xcs, nseq, seq_len,
                        jax.nn.silu)
            return carry
        lax.fori_loop(0, 8, body, 0)

    lane = lax.broadcasted_iota(jnp.int32, (r, LANE), 1)
    lo = lane < SSM_HEADDIM
    row_lo = lax.broadcasted_iota(jnp.int32, (LANE, LANE), 0) < SSM_HEADDIM
    seq_of_row = lax.broadcasted_iota(jnp.int32, (r, LANE), 0) >> int(math.log2(seq_len))
    tmat = t_ref[...]
    ttmat = tt_ref[...]
    causal = tmat.astype(F32) > 0.0

    def bcast_col(arr, j):
        return jnp.broadcast_to(arr[:, j:j + 1], (r, LANE))

    def group(g, carry):
        dt = _softplus(dt_ref[g] + dtb_ref[g])
        dta = dt * (-jnp.exp(alog_ref[g]))
        d3 = _split3(dta)
        cum = _sum3(jnp.dot(tmat, d3, preferred_element_type=F32), LANE)
        ct3 = lax.dot_general(d3, ttmat, _TN, preferred_element_type=F32)
        cum_t = ct3[0:LANE] + ct3[LANE:2 * LANE] + ct3[2 * LANE:3 * LANE]
        b_g = xcs[16 + g]
        c_g = xcs[20 + g]
        b_b = b_g.astype(BF16)
        cb_m = lax.dot_general(c_g.astype(BF16), b_b, _NT, preferred_element_type=F32)

        def decay_mat(j):
            rel = (jnp.broadcast_to(cum[:, j:j + 1], (r, r))
                   - jnp.broadcast_to(cum_t[j:j + 1, :], (r, r)))
            dec = jnp.where(causal, jnp.exp(jnp.where(causal, rel, 0.0)), 0.0)
            return (cb_m * dec).astype(BF16)

        ys = []
        for jj in range(4):
            j0, j1 = 2 * jj, 2 * jj + 1
            cbi = g * 4 + jj
            x_cb = xcs[cbi]
            dt_e = jnp.where(lo, bcast_col(dt, j0), bcast_col(dt, j1))
            cum_e = jnp.where(lo, bcast_col(cum, j0), bcast_col(cum, j1))
            u = x_cb * dt_e
            u_b = u.astype(BF16)
            y = jnp.where(lo,
                          jnp.dot(decay_mat(j0), u_b, preferred_element_type=F32),
                          jnp.dot(decay_mat(j1), u_b, preferred_element_type=F32))
            y_in = jnp.zeros((r, LANE), F32)
            for s in range(nseq):
                rl = s * seq_len + seq_len - 1
                h_cb = h_scr[s, cbi]
                if nseq == 1:
                    c_s = c_g
                    rel = cum_e[rl:rl + 1, :] - cum_e
                    uw = u * jnp.exp(rel)
                else:
                    in_seq = seq_of_row == s
                    c_s = jnp.where(in_seq, c_g, 0.0)
                    rel = jnp.where(in_seq, cum_e[rl:rl + 1, :] - cum_e, 0.0)
                    uw = jnp.where(in_seq, u * jnp.exp(rel), 0.0)
                y_in = y_in + lax.dot_general(c_s.astype(BF16), h_cb.astype(BF16), _NT,
                                              preferred_element_type=F32)
                last = jnp.where(row_lo,
                                 jnp.broadcast_to(cum[rl:rl + 1, j0:j0 + 1], (LANE, LANE)),
                                 jnp.broadcast_to(cum[rl:rl + 1, j1:j1 + 1], (LANE, LANE)))
                h_scr[s, cbi] = jnp.exp(last) * h_cb + lax.dot_general(
                    uw.astype(BF16), b_b, _TN, preferred_element_type=F32)
            y = y + y_in * jnp.exp(cum_e) + dsk_ref[cbi] * x_cb
            ys.append(y * jax.nn.silu(z_ref[cbi]))
        ms = sum(jnp.sum(y * y, axis=1, keepdims=True) for y in ys) * (1.0 / (4 * LANE))
        rinv = lax.rsqrt(ms + EPS)
        for jj in range(4):
            o_ref[g * 4 + jj] = ys[jj] * rinv * nw_ref[g * 4 + jj]
        return carry

    lax.fori_loop(0, SSM_GROUPS, group, 0)

    @pl.when(c == nchunks - 1)
    def _():
        hout_ref[0] = h_scr[...]
        cbout_ref[...] = tail[...]


def _tri_consts(nseq, seq_len):
    r = nseq * seq_len
    t = np.arange(r)[:, None]
    s = np.arange(r)[None, :]
    m = (((t // seq_len) == (s // seq_len)) & (s <= t)).astype(np.float32)
    return jnp.asarray(m, BF16), jnp.asarray(m.T, BF16)


def ssd_mixer(p3, o3, h0, h0_layer, h_all, layer, n_layers, cbuf0, params, *, row0, nbatch, t_len,
              nseq):
    seq_len = min(CHUNK, t_len)
    nchunks = t_len // seq_len
    r = nseq * seq_len
    nrb = nbatch // nseq
    rb0 = row0 // r
    cw, cbias, dtb, alog, dsk, nw = params
    tmat, ttmat = _tri_consts(nseq, seq_len)
    n_rows = p3.shape[1]

    def slab(nblk, blk_idx):
        return pl.BlockSpec((nblk, r, LANE), lambda i, c: (blk_idx, rb0 + i * nchunks + c, 0))

    def const(arr):
        nd = arr.ndim
        return pl.BlockSpec(arr.shape, lambda i, c: (0,) * nd)

    ins = [p3, p3, p3, p3, p3, cw, cbias, dtb, alog, dsk, nw, tmat, ttmat, cbuf0, h0]
    specs = [slab(16, 4), slab(8, 10), slab(8, 11), slab(8, 12), slab(4, 26),
             const(cw), const(cbias), const(dtb), const(alog), const(dsk), const(nw),
             const(tmat), const(ttmat),
             pl.BlockSpec((nseq, 24, 8, LANE), lambda i, c: (i, 0, 0, 0)),
             pl.BlockSpec((1, nseq, 16, LANE, LANE), lambda i, c: (h0_layer, i, 0, 0, 0))]
    aliases = {}
    _alias_or_dummy(o3, ins, specs, aliases, 0)
    _alias_or_dummy(h_all, ins, specs, aliases, 1)
    out = pl.pallas_call(
        functools.partial(_ssd_kernel, nseq=nseq, seq_len=seq_len, nchunks=nchunks),
        grid=(nrb, nchunks),
        in_specs=specs,
        out_specs=[pl.BlockSpec((16, r, LANE), lambda i, c: (1, rb0 + i * nchunks + c, 0)),
                   pl.BlockSpec((1, nseq, 16, LANE, LANE), lambda i, c: (layer, i, 0, 0, 0)),
                   pl.BlockSpec((nseq, 24, 8, LANE), lambda i, c: (i, 0, 0, 0))],
        out_shape=[jax.ShapeDtypeStruct((32, n_rows, LANE), F32),
                   jax.ShapeDtypeStruct((n_layers, nbatch, 16, LANE, LANE), F32),
                   jax.ShapeDtypeStruct((nbatch, 24, 8, LANE), F32)],
        scratch_shapes=[pltpu.VMEM((nseq, 16, LANE, LANE), F32),
                        pltpu.VMEM((nseq, 24, 8, LANE), F32),
                        pltpu.VMEM((24, r, LANE), F32),
                        pltpu.VMEM((seq_len + 8, LANE), F32)],
        input_output_aliases=aliases,
        compiler_params=_params(2),
        name="ssd_seq" if nchunks > 1 else "ssd_step",
    )(*ins)
    return out[0], out[1], out[2]


def _lru_kernel(x_ref, y_ref, cw_ref, cb_ref, wa_ref, ba_ref, wx_ref, bx_ref, lam_ref, cbuf0_ref,
                h0_ref, _, o_ref, hout_ref, cbout_ref, h_scr, tail, xcs, ext,
                *, nseq, seq_len, nchunks):
    r = nseq * seq_len
    sub = min(seq_len, CHUNK)
    c = pl.program_id(2)

    @pl.when(c == 0)
    def _():
        h_scr[...] = h0_ref[0]
        tail[...] = cbuf0_ref[...]

    _conv_block(x_ref, 0, 0, cw_ref, cb_ref, tail, ext, xcs, nseq, seq_len, lambda v: v)
    xc = xcs[0]
    xc_b = xc.astype(BF16)
    rg = jax.nn.sigmoid(jnp.dot(xc_b, wa_ref[0], preferred_element_type=F32) + ba_ref[0])
    ig = jax.nn.sigmoid(jnp.dot(xc_b, wx_ref[0], preferred_element_type=F32) + bx_ref[0])
    log_a = -LRU_C * rg * _softplus(-lam_ref[0])
    a_cum = jnp.exp(log_a)
    th = jnp.tanh(log_a)
    u_cum = jnp.sqrt(-2.0 * th / (1.0 - th)) * (ig * xc)

    pos = lax.broadcasted_iota(jnp.int32, (r, LANE), 0) & (sub - 1)
    d = 1
    while d < sub:
        take = pos >= d
        a_prev = pltpu.roll(a_cum, d, axis=0)
        u_prev = pltpu.roll(u_cum, d, axis=0)
        u_cum = jnp.where(take, a_cum * u_prev + u_cum, u_cum)
        a_cum = jnp.where(take, a_cum * a_prev, a_cum)
        d *= 2

    gel = jax.nn.gelu(y_ref[0])
    for s in range(nseq):
        h_prev = h_scr[s]
        for sb in range(seq_len // sub):
            r0 = s * seq_len + sb * sub
            hs = a_cum[r0:r0 + sub] * h_prev + u_cum[r0:r0 + sub]
            o_ref[0, r0:r0 + sub, :] = hs * gel[r0:r0 + sub]
            h_prev = hs[sub - 1:sub]
        h_scr[s] = h_prev

    @pl.when(c == nchunks - 1)
    def _():
        hout_ref[0] = h_scr[...]
        cbout_ref[...] = tail[...]


def lru_mixer(p3, o3, h0, cbuf0, params, *, row0, nbatch, t_len, nseq, rows):
    seq_len = min(rows, t_len)
    nchunks = t_len // seq_len
    r = nseq * seq_len
    nrb = nbatch // nseq
    rb0 = row0 // r
    cw, cbias, wa, ba, wx, bx, lam = params
    n_rows = p3.shape[1]

    def slab(off):
        return pl.BlockSpec((1, r, LANE), lambda i, n, c: (off + n, rb0 + i * nchunks + c, 0))

    def per_blk(arr):
        shp = (1,) + arr.shape[1:]
        nd = arr.ndim
        return pl.BlockSpec(shp, lambda i, n, c: (n,) + (0,) * (nd - 1))

    ins = [p3, p3, cw, cbias, wa, ba, wx, bx, lam, cbuf0, h0, o3]
    specs = [slab(48), slab(64), per_blk(cw), per_blk(cbias), per_blk(wa), per_blk(ba),
             per_blk(wx), per_blk(bx), per_blk(lam),
             pl.BlockSpec((nseq, 1, 8, LANE), lambda i, n, c: (i, n, 0, 0)),
             pl.BlockSpec((1, nseq, 1, LANE), lambda i, n, c: (n, i, 0, 0)),
             pl.BlockSpec(memory_space=pl.ANY)]
    out = pl.pallas_call(
        functools.partial(_lru_kernel, nseq=nseq, seq_len=seq_len, nchunks=nchunks),
        grid=(nrb, 16, nchunks),
        in_specs=specs,
        out_specs=[pl.BlockSpec((1, r, LANE), lambda i, n, c: (16 + n, rb0 + i * nchunks + c, 0)),
                   pl.BlockSpec((1, nseq, 1, LANE), lambda i, n, c: (n, i, 0, 0)),
                   pl.BlockSpec((nseq, 1, 8, LANE), lambda i, n, c: (i, n, 0, 0))],
        out_shape=[jax.ShapeDtypeStruct((32, n_rows, LANE), F32),
                   jax.ShapeDtypeStruct((16, nbatch, 1, LANE), F32),
                   jax.ShapeDtypeStruct((nbatch, 16, 8, LANE), F32)],
        scratch_shapes=[pltpu.VMEM((nseq, 1, LANE), F32),
                        pltpu.VMEM((nseq, 1, 8, LANE), F32),
                        pltpu.VMEM((1, r, LANE), F32),
                        pltpu.VMEM((seq_len + 8, LANE), F32)],
        input_output_aliases={len(ins) - 1: 0},
        compiler_params=_params(3),
        name="lru_seq" if nchunks > 1 else "lru_step",
    )(*ins)
    return out[0], out[1], out[2]


def _slabs(v, nblk):
    return v.astype(F32).reshape(nblk, 1, LANE)


def _pad_lanes(a, width=LANE):
    return jnp.pad(a, [(0, 0)] * (a.ndim - 1) + [(0, width - a.shape[-1])])


def _conv_state_in(buf, nblk):
    b = buf.shape[0]
    t = jnp.transpose(buf.astype(F32).reshape(b, 3, nblk, LANE), (0, 2, 1, 3))
    return jnp.pad(t, ((0, 0), (0, 0), (5, 0), (0, 0)))


def _conv_state_out(t):
    b, nblk = t.shape[:2]
    return jnp.transpose(t[:, :, 5:8, :], (0, 2, 1, 3)).reshape(b, 3, nblk * LANE)


EVEN_MAIN = 13312
ODD_GLOW = (6144, 6160)
IN_TN = 1024


def _prep_w_in_even(w):
    nl, d, _ = w.shape
    dt = _pad_lanes(w[:, :, EVEN_MAIN:].reshape(nl, d, SSM_GROUPS, SSM_HPG))
    return w[:, :, :EVEN_MAIN].astype(BF16), dt.reshape(nl, d, SSM_GROUPS * LANE).astype(BF16)


def _prep_w_in_odd(w):
    g0, g1 = ODD_GLOW
    main = jnp.concatenate([w[:, :, :g0], w[:, :, g1:]], axis=2).astype(BF16)
    return main, _pad_lanes(w[:, :, g0:g1]).astype(BF16)


def kernel(x_prompt, x_sample, state_hgrn, state_ssm, state_ssm_conv, state_gla, state_lru,
           state_lru_conv, norm_mix_pre, norm_mix_post, norm_ffn_pre, norm_ffn_post,
           w_in_even, w_out_even, hgrn_lb_logits, hgrn_norm_w,
           ssm_conv_w, ssm_conv_b, ssm_dt_bias, ssm_a_log, ssm_d, ssm_norm_w,
           w_in_odd, w_out_odd, gla_gate_w2, gla_gate_b, gla_norm_w,
           lru_conv_w, lru_conv_b, lru_wa, lru_ba, lru_wx, lru_bx, lru_lambda,
           ffn_w_gate, ffn_w_up, ffn_w_down):
    bp, tp, d = x_prompt.shape
    bs, ts, _ = x_sample.shape
    n_p, n_s = bp * tp, bs * ts
    depth = norm_mix_pre.shape[0]
    x = jnp.concatenate([x_prompt.reshape(n_p, d), x_sample.reshape(n_s, d)], axis=0)

    lb_soft = jax.nn.softmax(hgrn_lb_logits.astype(F32), axis=0)
    hgrn_lb = jnp.maximum(jnp.cumsum(lb_soft, axis=0) - lb_soft[0], 0.0)

    paths = (dict(row0=0, nbatch=bp, t_len=tp), dict(row0=n_p, nbatch=bs, t_len=ts))
    outs = {k: ([], []) for k in ("ssm_cb", "lru", "lru_cb")}
    big = {k: [None, None] for k in ("hg", "ssm", "gla")}
    n_even, n_odd = w_in_even.shape[0], w_in_odd.shape[0]

    we_main, we_dt = _prep_w_in_even(w_in_even)
    wo_main, wo_glow = _prep_w_in_odd(w_in_odd)
    w_out_e, w_out_o = w_out_even.astype(BF16), w_out_odd.astype(BF16)
    wg, wu, wd = ffn_w_gate.astype(BF16), ffn_w_up.astype(BF16), ffn_w_down.astype(BF16)
    st_hgrn = state_hgrn.astype(F32)
    st_ssm = state_ssm.astype(F32).reshape(n_even, bs, 16, LANE, LANE)
    st_gla = state_gla.astype(F32)
    zero_hg = jnp.zeros((1, bp) + state_hgrn.shape[2:], F32)
    zero_ssm = jnp.zeros((1, bp, 16, LANE, LANE), F32)
    zero_gla = jnp.zeros((1, bp) + state_gla.shape[2:], F32)

    for l in range(depth):
        j = l // 2
        if l % 2 == 0:
            p3 = inproj(x, norm_mix_pre[l],
                        [(we_main, j, EVEN_MAIN // IN_TN, IN_TN), (we_dt, j, 1, SSM_GROUPS * LANE)],
                        tn=IN_TN)
            hg_params = (_slabs(hgrn_lb[j], 16), _slabs(hgrn_norm_w[j], 1))
            dsk = jnp.repeat(ssm_d[j].astype(F32), SSM_HEADDIM)
            ssd_params = (
                jnp.transpose(ssm_conv_w[j].astype(F32).reshape(4, 24, LANE), (1, 0, 2)),
                _slabs(ssm_conv_b[j], 24),
                _pad_lanes(ssm_dt_bias[j].astype(F32).reshape(SSM_GROUPS, 1, SSM_HPG)),
                _pad_lanes(ssm_a_log[j].astype(F32).reshape(SSM_GROUPS, 1, SSM_HPG)),
                _slabs(dsk, 16), _slabs(ssm_norm_w[j], 16))
            o3 = None
            for pi, path in enumerate(paths):
                nb = path["nbatch"]
                if pi == 0:
                    s_hg, s_ssm, s_layer = zero_hg, zero_ssm, 0
                    s_cb = jnp.zeros((nb, 24, 8, LANE), F32)
                    nseq_a, hb_a, nseq_b = 1, 16, 1
                else:
                    s_hg, s_ssm, s_layer = st_hgrn, st_ssm, j
                    s_cb = _conv_state_in(state_ssm_conv[j], 24)
                    nseq_a, hb_a, nseq_b = 8, 4, 2
                o3, big["hg"][pi] = gla_mixer(
                    "hgrn", p3, o3, s_hg, s_layer, big["hg"][pi], j, n_even, hg_params,
                    nseq=nseq_a, hb=hb_a, heads=16, dk=128, dv=128, offs=(0, 16, 32, 48), out_off=0,
                    unroll=2, **path)
                o3, big["ssm"][pi], cb_b = ssd_mixer(p3, o3, s_ssm, s_layer, big["ssm"][pi], j,
                                                     n_even, s_cb, ssd_params, nseq=nseq_b, **path)
                outs["ssm_cb"][pi].append(_conv_state_out(cb_b).astype(state_ssm_conv.dtype))
            x = outproj(o3, w_out_e, j, x, norm_mix_post[l])
        else:
            p3 = inproj(x, norm_mix_pre[l],
                        [(wo_main, j, wo_main.shape[2] // IN_TN, IN_TN), (wo_glow, j, 1, LANE)],
                        tn=IN_TN)
            w2 = jnp.pad(gla_gate_w2[j], ((0, LANE - gla_gate_w2.shape[1]), (0, 0)))
            w2 = jnp.transpose(w2.reshape(LANE, 4, 256), (1, 0, 2)).astype(BF16)
            gla_params = (w2, _slabs(gla_gate_b[j], 8), _slabs(gla_norm_w[j], 4))
            lru_params = (
                jnp.transpose(lru_conv_w[j].astype(F32).reshape(4, 16, LANE), (1, 0, 2)),
                _slabs(lru_conv_b[j], 16), lru_wa[j].astype(BF16), _slabs(lru_ba[j], 16),
                lru_wx[j].astype(BF16), _slabs(lru_bx[j], 16), _slabs(lru_lambda[j], 16))
            o3 = None
            for pi, path in enumerate(paths):
                nb = path["nbatch"]
                if pi == 0:
                    s_gla, s_layer = zero_gla, 0
                    s_lru = jnp.zeros((16, nb, 1, LANE), F32)
                    s_cb = jnp.zeros((nb, 16, 8, LANE), F32)
                    nseq_c, hb_c, nseq_d, rows_d = 1, 4, 1, 256
                else:
                    s_gla, s_layer = st_gla, j
                    s_lru = jnp.transpose(state_lru[j].astype(F32).reshape(nb, 16, 1, LANE),
                                          (1, 0, 2, 3))
                    s_cb = _conv_state_in(state_lru_conv[j], 16)
                    nseq_c, hb_c, nseq_d, rows_d = 8, 1, 16, 8
                o3, big["gla"][pi] = gla_mixer(
                    "gla", p3, o3, s_gla, s_layer, big["gla"][pi], j, n_odd, gla_params,
                    nseq=nseq_c, hb=hb_c, heads=4, dk=256, dv=512, offs=(0, 8, 16, 32, 80),
                    out_off=0, unroll=min(2, hb_c), **path)
                o3, st_d, cb_d = lru_mixer(p3, o3, s_lru, s_cb, lru_params, nseq=nseq_d,
                                           rows=rows_d, **path)
                outs["lru"][pi].append(jnp.transpose(st_d, (1, 0, 2, 3)).reshape(nb, 16 * LANE)
                                       .astype(state_lru.dtype))
                outs["lru_cb"][pi].append(_conv_state_out(cb_d).astype(state_lru_conv.dtype))
            x = outproj(o3, w_out_o, j, x, norm_mix_post[l])
        x = ffn(x, norm_ffn_pre[l], wg, wu, wd, l, norm_ffn_post[l])

    y_prompt = x[:n_p].reshape(bp, tp, d).astype(x_prompt.dtype)
    y_sample = x[n_p:].reshape(bs, ts, d).astype(x_sample.dtype)
    ssm_shape = state_ssm.shape[2:]
    res = [y_prompt, y_sample]
    stacked = lambda key, pi: jnp.stack(outs[key][pi])
    res += [big["hg"][0].astype(state_hgrn.dtype), big["hg"][1].astype(state_hgrn.dtype),
            big["ssm"][0].reshape((n_even, bp) + ssm_shape).astype(state_ssm.dtype),
            big["ssm"][1].reshape((n_even, bs) + ssm_shape).astype(state_ssm.dtype),
            stacked("ssm_cb", 0), stacked("ssm_cb", 1),
            big["gla"][0].astype(state_gla.dtype), big["gla"][1].astype(state_gla.dtype),
            stacked("lru", 0), stacked("lru", 1), stacked("lru_cb", 0), stacked("lru_cb", 1)]
    return tuple(res)
```

```python
import functools
import math

import numpy as np
import jax
import jax.numpy as jnp
from jax import lax
from jax.experimental import pallas as pl
from jax.experimental.pallas import tpu as pltpu

F32 = jnp.float32
BF16 = jnp.bfloat16
LANE = 128
VMEM_LIMIT = 56 * 1024 * 1024

DOWN_STRIP = 512

EPS = 1e-6
F_MIN = 1e-30
CHUNK = 64
GLA_GATE_NORMALIZER = 16.0
LRU_C = 8.0
SSM_GROUPS = 4
SSM_HPG = 8
SSM_HEADDIM = 64


def _params(n_axes):
    return pltpu.CompilerParams(dimension_semantics=("arbitrary",) * n_axes,
                                vmem_limit_bytes=VMEM_LIMIT)


def _softplus(x):
    return jnp.maximum(x, 0.0) + jnp.log1p(jnp.exp(-jnp.abs(x)))


def _log_sigmoid(x):
    return jnp.minimum(x, 0.0) - jnp.log1p(jnp.exp(-jnp.abs(x)))


def _split3(x):
    hi = x.astype(BF16)
    r1 = x - hi.astype(F32)
    mid = r1.astype(BF16)
    lo = (r1 - mid.astype(F32)).astype(BF16)
    return jnp.concatenate([hi, mid, lo], axis=1)


def _sum3(y, w):
    return y[:, :w] + y[:, w:2 * w] + y[:, 2 * w:3 * w]


def _cat(ref, base, n):
    if n == 1:
        return ref[base]
    return jnp.concatenate([ref[base + j] for j in range(n)], axis=1)


_NT = (((1,), (1,)), ((), ()))
_TN = (((0,), (0,)), ((), ()))


def _prenorm_rows(x_ref, w_ref, h_scr, tm, rows=64):
    w = w_ref[...]

    def body(i, c):
        r0 = pl.multiple_of(i * rows, rows)
        x = x_ref[pl.ds(r0, rows), :]
        ms = jnp.mean(x * x, axis=1, keepdims=True)
        h_scr[pl.ds(r0, rows), :] = (x * lax.rsqrt(ms + EPS) * w).astype(BF16)
        return c

    lax.fori_loop(0, tm // rows, body, 0)


def _postnorm_rows(o_ref, x_ref, w_ref, tm, rows=64):
    w = w_ref[...]

    def body(i, c):
        r0 = pl.multiple_of(i * rows, rows)
        y = o_ref[pl.ds(r0, rows), :]
        ms = jnp.mean(y * y, axis=1, keepdims=True)
        o_ref[pl.ds(r0, rows), :] = x_ref[pl.ds(r0, rows), :] + y * lax.rsqrt(ms + EPS) * w
        return c

    lax.fori_loop(0, tm // rows, body, 0)


def _resident(shape, index_map):
    return pl.BlockSpec(shape, index_map, pipeline_mode=pl.Buffered(1))


def _inproj_kernel(x_ref, nw_ref, *rest, tm, segs):
    w_refs, o_ref, h_scr = rest[:len(segs)], rest[len(segs)], rest[len(segs) + 1]
    j = pl.program_id(1)

    @pl.when(j == 0)
    def _():
        _prenorm_rows(x_ref, nw_ref, h_scr, tm)

    for w_ref, (start, ntiles, width) in zip(w_refs, segs):
        @pl.when((j >= start) & (j < start + ntiles))
        def _(w_ref=w_ref, width=width):
            r = jnp.dot(h_scr[...], w_ref[0], preferred_element_type=F32)
            for c in range(width // LANE):
                o_ref[c] = r[:, c * LANE:(c + 1) * LANE]


def inproj(x, norm_w, segments, *, tm=512, tn=1024):
    n, d = x.shape
    cb = tn // LANE
    segs, specs, start = [], [], 0
    for w, layer, ntiles, width in segments:
        segs.append((start, ntiles, width))
        specs.append(pl.BlockSpec(
            (1, d, width),
            lambda i, j, layer=layer, start=start, ntiles=ntiles: (layer, 0, jnp.clip(j - start, 0, ntiles - 1))))
        start += ntiles
    return pl.pallas_call(
        functools.partial(_inproj_kernel, tm=tm, segs=tuple(segs)),
        grid=(n // tm, start),
        in_specs=[_resident((tm, d), lambda i, j: (i, 0)),
                  pl.BlockSpec((1, d), lambda i, j: (0, 0))] + specs,
        out_specs=pl.BlockSpec((cb, tm, LANE), lambda i, j: (j, i, 0)),
        out_shape=jax.ShapeDtypeStruct((start * cb, n, LANE), F32),
        scratch_shapes=[pltpu.VMEM((tm, d), BF16)],
        compiler_params=_params(2),
        name="inproj",
    )(x, norm_w.reshape(1, d), *[s[0] for s in segments])


def _outproj_kernel(a_ref, w3_ref, x_ref, nw_ref, o_ref, *, tm, kb, nk):
    k = pl.program_id(1)
    w_ref = w3_ref.at[0]
    a = _cat(a_ref, 0, kb).astype(BF16)

    @pl.when(k == 0)
    def _():
        o_ref[...] = jnp.zeros_like(o_ref)

    for n0 in range(0, o_ref.shape[1], DOWN_STRIP):
        o_ref[:, n0:n0 + DOWN_STRIP] += jnp.dot(a, w_ref[:, n0:n0 + DOWN_STRIP],
                                                preferred_element_type=F32)

    @pl.when(k == nk - 1)
    def _():
        _postnorm_rows(o_ref, x_ref, nw_ref, tm)


def outproj(a3, w, layer, x, norm_w, *, tm=512, tk=512):
    n, d = x.shape
    kdim = w.shape[1]
    kb = tk // LANE
    nk = kdim // tk
    return pl.pallas_call(
        functools.partial(_outproj_kernel, tm=tm, kb=kb, nk=nk),
        grid=(n // tm, nk),
        in_specs=[pl.BlockSpec((kb, tm, LANE), lambda i, k: (k, i, 0)),
                  pl.BlockSpec((1, tk, d), lambda i, k: (layer, k, 0)),
                  pl.BlockSpec((tm, d), lambda i, k: (i, 0)),
                  pl.BlockSpec((1, d), lambda i, k: (0, 0))],
        out_specs=pl.BlockSpec((tm, d), lambda i, k: (i, 0)),
        out_shape=jax.ShapeDtypeStruct((n, d), F32),
        compiler_params=_params(2),
        name="outproj",
    )(a3, w, x, norm_w.reshape(1, d))


def _ffn_kernel(x_ref, pre_ref, wg_ref, wu_ref, wd3_ref, post_ref, o_ref, h_scr, a_scr, *, tm, nf):
    f = pl.program_id(1)
    wd_ref = wd3_ref.at[0]

    @pl.when(f == 0)
    def _():
        _prenorm_rows(x_ref, pre_ref, h_scr, tm)
        o_ref[...] = jnp.zeros_like(o_ref)
        a_scr[...] = jnp.zeros_like(a_scr)

    a_prev = a_scr[...]
    h = h_scr[...]
    g = jnp.dot(h, wg_ref[0], preferred_element_type=F32)
    u = jnp.dot(h, wu_ref[0], preferred_element_type=F32)
    for n0 in range(0, o_ref.shape[1], DOWN_STRIP):
        o_ref[:, n0:n0 + DOWN_STRIP] += jnp.dot(a_prev, wd_ref[:, n0:n0 + DOWN_STRIP],
                                                preferred_element_type=F32)
    a_scr[...] = (jax.nn.silu(g) * u).astype(BF16)

    @pl.when(f == nf)
    def _():
        _postnorm_rows(o_ref, x_ref, post_ref, tm)


def ffn(x, pre_w, wg, wu, wd, layer, post_w, *, tm=512, tf=256):
    n, d = x.shape
    dff = wg.shape[2]
    nf = dff // tf
    return pl.pallas_call(
        functools.partial(_ffn_kernel, tm=tm, nf=nf),
        grid=(n // tm, nf + 1),
        in_specs=[pl.BlockSpec((tm, d), lambda i, f: (i, 0)),
                  pl.BlockSpec((1, d), lambda i, f: (0, 0)),
                  pl.BlockSpec((1, d, tf), lambda i, f: (layer, 0, jnp.minimum(f, nf - 1))),
                  pl.BlockSpec((1, d, tf), lambda i, f: (layer, 0, jnp.minimum(f, nf - 1))),
                  pl.BlockSpec((1, tf, d), lambda i, f: (layer, jnp.maximum(f - 1, 0), 0)),
                  pl.BlockSpec((1, d), lambda i, f: (0, 0))],
        out_specs=pl.BlockSpec((tm, d), lambda i, f: (i, 0)),
        out_shape=jax.ShapeDtypeStruct((n, d), F32),
        scratch_shapes=[pltpu.VMEM((tm, d), BF16), pltpu.VMEM((tm, tf), BF16)],
        compiler_params=_params(2),
        name="ffn",
    )(x, pre_w.reshape(1, d), wg, wu, wd, post_w.reshape(1, d))


def _gla_consts(nseq, seq_len):
    r = nseq * seq_len
    t = np.arange(r)[:, None]
    s = np.arange(r)[None, :]
    prefix = ((t // seq_len) == (s // seq_len)) & (s <= t)
    masks = []
    c = seq_len // 2
    while c >= 1:
        right = (t % (2 * c)) >= c
        masks.append(((t // (2 * c)) == (s // (2 * c))) & right & ((s % (2 * c)) < c))
        c //= 2
    m = np.stack(masks).astype(np.float32)
    return jnp.asarray(prefix.astype(np.float32), BF16), jnp.asarray(m, F32)


def _block_row(b, blk, pick, pos):
    r, w = b.shape
    if blk >= 8:
        return jnp.concatenate(
            [jnp.broadcast_to(b[s + pick:s + pick + 1, :], (blk, w)) for s in range(0, r, blk)], axis=0)
    out = b
    for p in range(blk):
        if p != pick:
            out = jnp.where(pos == p, pltpu.roll(b, (p - pick) % r, axis=0), out)
    return out


def _gla_kernel(*refs, mode, nseq, seq_len, hb, dk, dv, nchunks, unroll):
    if mode == "hgrn":
        (q_ref, f_ref, v_ref, gt_ref, lb_ref, nw_ref, a_ref, m_ref, s0_ref, _, _,
         o_ref, so_ref, s_scr) = refs
    else:
        (q_ref, k_ref, v_ref, gt_ref, gl_ref, w2_ref, gb_ref, nw_ref, a_ref, m_ref, s0_ref, _, _,
         o_ref, so_ref, s_scr) = refs
    dkb, dvb = dk // LANE, dv // LANE
    r = nseq * seq_len
    nlev = int(math.log2(seq_len))
    c = pl.program_id(2)

    @pl.when(c == 0)
    def _():
        s_scr[...] = s0_ref[0]

    a_mat = a_ref[...]
    seq_shift = int(math.log2(seq_len))
    rowk = lax.broadcasted_iota(jnp.int32, (r, dk), 0)
    seq_of_row = lax.broadcasted_iota(jnp.int32, (r, LANE), 0) >> seq_shift
    lane_id = lax.broadcasted_iota(jnp.int32, (r, LANE), 1)
    seqsel = (seq_of_row == lane_id).astype(BF16)
    nw = _cat(nw_ref, 0, dvb)

    def head(h, carry):
        qr = _cat(q_ref, h * dkb, dkb)
        v = _cat(v_ref, h * dvb, dvb)
        gate = _cat(gt_ref, h * dvb, dvb)
        if mode == "hgrn":
            lb = _cat(lb_ref, h * dkb, dkb)
            fg = lb + (1.0 - lb) * jax.nn.sigmoid(_cat(f_ref, h * dkb, dkb))
            g = jnp.log(jnp.maximum(fg, F_MIN))
            k = 1.0 - fg
            q = jax.nn.silu(qr) * (dk ** -0.5)
        else:
            k = _cat(k_ref, h * dkb, dkb)
            q = qr * (dk ** -0.5)
            lin = jnp.dot(gl_ref[0].astype(BF16), w2_ref[h], preferred_element_type=F32)
            g = _log_sigmoid(lin + _cat(gb_ref, h * dkb, dkb)) / GLA_GATE_NORMALIZER

        g3 = _split3(g)
        b = _sum3(jnp.dot(a_mat, g3, preferred_element_type=F32), dk)

        scores = jnp.zeros((r, r), F32)
        for l in range(nlev):
            half = seq_len >> (l + 1)
            pos = rowk & (2 * half - 1)
            right = pos >= half
            b_m = _block_row(b, 2 * half, half - 1, pos)
            x = (jnp.where(right, q, k) * jnp.exp(jnp.where(right, b - b_m, b_m - b))).astype(BF16)
            scores = scores + m_ref[l] * lax.dot_general(x, x, _NT, preferred_element_type=F32)

        vb = v.astype(BF16)
        o = jnp.dot(scores.astype(BF16), vb, preferred_element_type=F32)
        o = o + jnp.sum(q * k, axis=1, keepdims=True) * v

        qb = q * jnp.exp(b)
        kb = k * jnp.exp(_block_row(b, seq_len, seq_len - 1, None) - b)
        d3 = lax.dot_general(g3, seqsel, _TN, preferred_element_type=F32)
        dcol = d3[0:dk] + d3[dk:2 * dk] + d3[2 * dk:3 * dk]
        for s in range(nseq):
            if nseq == 1:
                qs, ks = qb, kb
            else:
                in_seq = (rowk >> seq_shift) == s
                qs = jnp.where(in_seq, qb, 0.0)
                ks = jnp.where(in_seq, kb, 0.0)
            st = s_scr[s, h]
            o = o + jnp.dot(qs.astype(BF16), st.astype(BF16), preferred_element_type=F32)
            dec = jnp.exp(jnp.broadcast_to(dcol[:, s:s + 1], (dk, dv)))
            s_scr[s, h] = dec * st + lax.dot_general(ks.astype(BF16), vb, _TN,
                                                     preferred_element_type=F32)

        ms = jnp.mean(o * o, axis=1, keepdims=True)
        y = o * lax.rsqrt(ms + EPS) * nw * jax.nn.silu(gate)
        for j in range(dvb):
            o_ref[h * dvb + j] = y[:, j * LANE:(j + 1) * LANE]
        return carry

    lax.fori_loop(0, hb, head, 0, unroll=unroll)

    @pl.when(c == nchunks - 1)
    def _():
        so_ref[0] = s_scr[...]


def _alias_or_dummy(arr, ins, specs, aliases, out_idx):
    if arr is None:
        ins.append(jnp.zeros((8, LANE), F32))
    else:
        ins.append(arr)
        aliases[len(ins) - 1] = out_idx
    specs.append(pl.BlockSpec(memory_space=pl.ANY))


def gla_mixer(mode, p3, o3, s0, s0_layer, st_all, layer, n_layers, params, *, row0, nbatch, t_len,
              nseq, hb, heads, dk, dv, offs, out_off, unroll):
    seq_len = min(CHUNK, t_len)
    nchunks = t_len // seq_len
    r = nseq * seq_len
    dkb, dvb = dk // LANE, dv // LANE
    nrb = nbatch // nseq
    rb0 = row0 // r
    a_mat, masks = _gla_consts(nseq, seq_len)
    n_rows = p3.shape[1]

    def slab(nblk, off):
        return pl.BlockSpec((nblk, r, LANE),
                            lambda i, hg, c: (off // nblk + hg, rb0 + i * nchunks + c, 0))

    def const(arr):
        nd = arr.ndim
        return pl.BlockSpec(arr.shape, lambda i, hg, c: (0,) * nd)

    def per_head(arr, nblk):
        return pl.BlockSpec((nblk, 1, LANE), lambda i, hg, c: (hg, 0, 0))

    if mode == "hgrn":
        lb, nw = params
        ins = [p3, p3, p3, p3, lb, nw, a_mat, masks, s0]
        specs = [slab(hb * dkb, offs[0]), slab(hb * dkb, offs[1]), slab(hb * dvb, offs[2]),
                 slab(hb * dvb, offs[3]), per_head(lb, hb * dkb), const(nw), const(a_mat),
                 const(masks)]
    else:
        w2, gb, nw = params
        ins = [p3, p3, p3, p3, p3, w2, gb, nw, a_mat, masks, s0]
        specs = [slab(hb * dkb, offs[0]), slab(hb * dkb, offs[1]), slab(hb * dvb, offs[2]),
                 slab(hb * dvb, offs[3]),
                 pl.BlockSpec((1, r, LANE), lambda i, hg, c: (offs[4], rb0 + i * nchunks + c, 0)),
                 pl.BlockSpec((hb, LANE, dk), lambda i, hg, c: (hg, 0, 0)),
                 per_head(gb, hb * dkb), const(nw), const(a_mat), const(masks)]
    specs.append(pl.BlockSpec((1, nseq, hb, dk, dv), lambda i, hg, c: (s0_layer, i, hg, 0, 0)))
    aliases = {}
    _alias_or_dummy(o3, ins, specs, aliases, 0)
    _alias_or_dummy(st_all, ins, specs, aliases, 1)

    out = pl.pallas_call(
        functools.partial(_gla_kernel, mode=mode, nseq=nseq, seq_len=seq_len, hb=hb, dk=dk, dv=dv,
                          nchunks=nchunks, unroll=unroll),
        grid=(nrb, heads // hb, nchunks),
        in_specs=specs,
        out_specs=[pl.BlockSpec((hb * dvb, r, LANE),
                                lambda i, hg, c: (out_off // (hb * dvb) + hg, rb0 + i * nchunks + c, 0)),
                   pl.BlockSpec((1, nseq, hb, dk, dv), lambda i, hg, c: (layer, i, hg, 0, 0))],
        out_shape=[jax.ShapeDtypeStruct((32, n_rows, LANE), F32),
                   jax.ShapeDtypeStruct((n_layers, nbatch, heads, dk, dv), F32)],
        scratch_shapes=[pltpu.VMEM((nseq, hb, dk, dv), F32)],
        input_output_aliases=aliases,
        compiler_params=_params(3),
        name=mode + ("_seq" if nchunks > 1 else "_step"),
    )(*ins)
    return out[0], out[1]


def _conv_block(ref, i, cbi, cw_ref, cb_ref, tail, ext, xcs, nseq, seq_len, act):
    w = cw_ref[cbi]
    b = cb_ref[cbi]
    for s in range(nseq):
        ext[0:8, :] = tail[s, cbi]
        ext[8:8 + seq_len, :] = ref[i, s * seq_len:(s + 1) * seq_len, :]
        y = (b + w[3:4] * ext[8:8 + seq_len, :] + w[2:3] * ext[7:7 + seq_len, :]
             + w[1:2] * ext[6:6 + seq_len, :] + w[0:1] * ext[5:5 + seq_len, :])
        xcs[cbi, s * seq_len:(s + 1) * seq_len, :] = act(y)
        tail[s, cbi] = ext[seq_len:seq_len + 8, :]


def _ssd_kernel(z_ref, xa_ref, xb_ref, xc_ref, dt_ref, cw_ref, cb_ref, dtb_ref, alog_ref, dsk_ref,
                nw_ref, t_ref, tt_ref, cbuf0_ref, h0_ref, _o3_any, _h_any, o_ref, hout_ref, cbout_ref,
                h_scr, tail, xcs, ext, *, nseq, seq_len, nchunks):
    r = nseq * seq_len
    c = pl.program_id(1)

    @pl.when(c == 0)
    def _():
        h_scr[...] = h0_ref[0]
        tail[...] = cbuf0_ref[...]

    for part, ref in enumerate((xa_ref, xb_ref, xc_ref)):
        def body(i, carry, part=part, ref=ref):
            _conv_block(ref, i, part * 8 + i, cw_ref, cb_ref, tail, ext, xcs, nseq, seq_len,
                        jax.nn.silu)
            return carry
        lax.fori_loop(0, 8, body, 0)

    lane = lax.broadcasted_iota(jnp.int32, (r, LANE), 1)
    lo = lane < SSM_HEADDIM
    row_lo = lax.broadcasted_iota(jnp.int32, (LANE, LANE), 0) < SSM_HEADDIM
    seq_of_row = lax.broadcasted_iota(jnp.int32, (r, LANE), 0) >> int(math.log2(seq_len))
    tmat = t_ref[...]
    ttmat = tt_ref[...]
    causal = tmat.astype(F32) > 0.0

    def bcast_col(arr, j):
        return jnp.broadcast_to(arr[:, j:j + 1], (r, LANE))

    def group(g, carry):
        dt = _softplus(dt_ref[g] + dtb_ref[g])
        dta = dt * (-jnp.exp(alog_ref[g]))
        d3 = _split3(dta)
        cum = _sum3(jnp.dot(tmat, d3, preferred_element_type=F32), LANE)
        ct3 = lax.dot_general(d3, ttmat, _TN, preferred_element_type=F32)
        cum_t = ct3[0:LANE] + ct3[LANE:2 * LANE] + ct3[2 * LANE:3 * LANE]
        b_g = xcs[16 + g]
        c_g = xcs[20 + g]
        b_b = b_g.astype(BF16)
        cb_m = lax.dot_general(c_g.astype(BF16), b_b, _NT, preferred_element_type=F32)

        def decay_mat(j):
            rel = (jnp.broadcast_to(cum[:, j:j + 1], (r, r))
                   - jnp.broadcast_to(cum_t[j:j + 1, :], (r, r)))
            dec = jnp.where(causal, jnp.exp(jnp.where(causal, rel, 0.0)), 0.0)
            return (cb_m * dec).astype(BF16)

        ys = []
        for jj in range(4):
            j0, j1 = 2 * jj, 2 * jj + 1
            cbi = g * 4 + jj
            x_cb = xcs[cbi]
            dt_e = jnp.where(lo, bcast_col(dt, j0), bcast_col(dt, j1))
            cum_e = jnp.where(lo, bcast_col(cum, j0), bcast_col(cum, j1))
            u = x_cb * dt_e
            u_b = u.astype(BF16)
            y = jnp.where(lo,
                          jnp.dot(decay_mat(j0), u_b, preferred_element_type=F32),
                          jnp.dot(decay_mat(j1), u_b, preferred_element_type=F32))
            y_in = jnp.zeros((r, LANE), F32)
            for s in range(nseq):
                rl = s * seq_len + seq_len - 1
                h_cb = h_scr[s, cbi]
                if nseq == 1:
                    c_s = c_g
                    rel = cum_e[rl:rl + 1, :] - cum_e
                    uw = u * jnp.exp(rel)
                else:
                    in_seq = seq_of_row == s
                    c_s = jnp.where(in_seq, c_g, 0.0)
                    rel = jnp.where(in_seq, cum_e[rl:rl + 1, :] - cum_e, 0.0)
                    uw = jnp.where(in_seq, u * jnp.exp(rel), 0.0)
                y_in = y_in + lax.dot_general(c_s.astype(BF16), h_cb.astype(BF16), _NT,
                                              preferred_element_type=F32)
                last = jnp.where(row_lo,
                                 jnp.broadcast_to(cum[rl:rl + 1, j0:j0 + 1], (LANE, LANE)),
                                 jnp.broadcast_to(cum[rl:rl + 1, j1:j1 + 1], (LANE, LANE)))
                h_scr[s, cbi] = jnp.exp(last) * h_cb + lax.dot_general(
                    uw.astype(BF16), b_b, _TN, preferred_element_type=F32)
            y = y + y_in * jnp.exp(cum_e) + dsk_ref[cbi] * x_cb
            ys.append(y * jax.nn.silu(z_ref[cbi]))
        ms = sum(jnp.sum(y * y, axis=1, keepdims=True) for y in ys) * (1.0 / (4 * LANE))
        rinv = lax.rsqrt(ms + EPS)
        for jj in range(4):
            o_ref[g * 4 + jj] = ys[jj] * rinv * nw_ref[g * 4 + jj]
        return carry

    lax.fori_loop(0, SSM_GROUPS, group, 0, unroll=2)

    @pl.when(c == nchunks - 1)
    def _():
        hout_ref[0] = h_scr[...]
        cbout_ref[...] = tail[...]


def _tri_consts(nseq, seq_len):
    r = nseq * seq_len
    t = np.arange(r)[:, None]
    s = np.arange(r)[None, :]
    m = (((t // seq_len) == (s // seq_len)) & (s <= t)).astype(np.float32)
    return jnp.asarray(m, BF16), jnp.asarray(m.T, BF16)


def ssd_mixer(p3, o3, h0, h0_layer, h_all, layer, n_layers, cbuf0, params, *, row0, nbatch, t_len,
              nseq):
    seq_len = min(CHUNK, t_len)
    nchunks = t_len // seq_len
    r = nseq * seq_len
    nrb = nbatch // nseq
    rb0 = row0 // r
    cw, cbias, dtb, alog, dsk, nw = params
    tmat, ttmat = _tri_consts(nseq, seq_len)
    n_rows = p3.shape[1]

    def slab(nblk, blk_idx):
        return pl.BlockSpec((nblk, r, LANE), lambda i, c: (blk_idx, rb0 + i * nchunks + c, 0))

    def const(arr):
        nd = arr.ndim
        return pl.BlockSpec(arr.shape, lambda i, c: (0,) * nd)

    ins = [p3, p3, p3, p3, p3, cw, cbias, dtb, alog, dsk, nw, tmat, ttmat, cbuf0, h0]
    specs = [slab(16, 4), slab(8, 10), slab(8, 11), slab(8, 12), slab(4, 26),
             const(cw), const(cbias), const(dtb), const(alog), const(dsk), const(nw),
             const(tmat), const(ttmat),
             pl.BlockSpec((nseq, 24, 8, LANE), lambda i, c: (i, 0, 0, 0)),
             pl.BlockSpec((1, nseq, 16, LANE, LANE), lambda i, c: (h0_layer, i, 0, 0, 0))]
    aliases = {}
    _alias_or_dummy(o3, ins, specs, aliases, 0)
    _alias_or_dummy(h_all, ins, specs, aliases, 1)
    out = pl.pallas_call(
        functools.partial(_ssd_kernel, nseq=nseq, seq_len=seq_len, nchunks=nchunks),
        grid=(nrb, nchunks),
        in_specs=specs,
        out_specs=[pl.BlockSpec((16, r, LANE), lambda i, c: (1, rb0 + i * nchunks + c, 0)),
                   pl.BlockSpec((1, nseq, 16, LANE, LANE), lambda i, c: (layer, i, 0, 0, 0)),
                   pl.BlockSpec((nseq, 24, 8, LANE), lambda i, c: (i, 0, 0, 0))],
        out_shape=[jax.ShapeDtypeStruct((32, n_rows, LANE), F32),
                   jax.ShapeDtypeStruct((n_layers, nbatch, 16, LANE, LANE), F32),
                   jax.ShapeDtypeStruct((nbatch, 24, 8, LANE), F32)],
        scratch_shapes=[pltpu.VMEM((nseq, 16, LANE, LANE), F32),
                        pltpu.VMEM((nseq, 24, 8, LANE), F32),
                        pltpu.VMEM((24, r, LANE), F32),
                        pltpu.VMEM((seq_len + 8, LANE), F32)],
        input_output_aliases=aliases,
        compiler_params=_params(2),
        name="ssd_seq" if nchunks > 1 else "ssd_step",
    )(*ins)
    return out[0], out[1], out[2]


def _lru_kernel(x_ref, y_ref, cw_ref, cb_ref, wa_ref, ba_ref, wx_ref, bx_ref, lam_ref, cbuf0_ref,
                h0_ref, _, o_ref, hout_ref, cbout_ref, h_scr, tail, xcs, ext,
                *, nseq, seq_len, nchunks):
    r = nseq * seq_len
    sub = min(seq_len, CHUNK)
    c = pl.program_id(2)

    @pl.when(c == 0)
    def _():
        h_scr[...] = h0_ref[0]
        tail[...] = cbuf0_ref[...]

    _conv_block(x_ref, 0, 0, cw_ref, cb_ref, tail, ext, xcs, nseq, seq_len, lambda v: v)
    xc = xcs[0]
    xc_b = xc.astype(BF16)
    rg = jax.nn.sigmoid(jnp.dot(xc_b, wa_ref[0], preferred_element_type=F32) + ba_ref[0])
    ig = jax.nn.sigmoid(jnp.dot(xc_b, wx_ref[0], preferred_element_type=F32) + bx_ref[0])
    log_a = -LRU_C * rg * _softplus(-lam_ref[0])
    a_cum = jnp.exp(log_a)
    th = jnp.tanh(log_a)
    u_cum = jnp.sqrt(-2.0 * th / (1.0 - th)) * (ig * xc)

    pos = lax.broadcasted_iota(jnp.int32, (r, LANE), 0) & (sub - 1)
    d = 1
    while d < sub:
        take = pos >= d
        a_prev = pltpu.roll(a_cum, d, axis=0)
        u_prev = pltpu.roll(u_cum, d, axis=0)
        u_cum = jnp.where(take, a_cum * u_prev + u_cum, u_cum)
        a_cum = jnp.where(take, a_cum * a_prev, a_cum)
        d *= 2

    gel = jax.nn.gelu(y_ref[0])
    for s in range(nseq):
        h_prev = h_scr[s]
        for sb in range(seq_len // sub):
            r0 = s * seq_len + sb * sub
            hs = a_cum[r0:r0 + sub] * h_prev + u_cum[r0:r0 + sub]
            o_ref[0, r0:r0 + sub, :] = hs * gel[r0:r0 + sub]
            h_prev = hs[sub - 1:sub]
        h_scr[s] = h_prev

    @pl.when(c == nchunks - 1)
    def _():
        hout_ref[0] = h_scr[...]
        cbout_ref[...] = tail[...]


def lru_mixer(p3, o3, h0, cbuf0, params, *, row0, nbatch, t_len, nseq, rows):
    seq_len = min(rows, t_len)
    nchunks = t_len // seq_len
    r = nseq * seq_len
    nrb = nbatch // nseq
    rb0 = row0 // r
    cw, cbias, wa, ba, wx, bx, lam = params
    n_rows = p3.shape[1]

    def slab(off):
        return pl.BlockSpec((1, r, LANE), lambda i, n, c: (off + n, rb0 + i * nchunks + c, 0))

    def per_blk(arr):
        shp = (1,) + arr.shape[1:]
        nd = arr.ndim
        return pl.BlockSpec(shp, lambda i, n, c: (n,) + (0,) * (nd - 1))

    ins = [p3, p3, cw, cbias, wa, ba, wx, bx, lam, cbuf0, h0, o3]
    specs = [slab(48), slab(64), per_blk(cw), per_blk(cbias), per_blk(wa), per_blk(ba),
             per_blk(wx), per_blk(bx), per_blk(lam),
             pl.BlockSpec((nseq, 1, 8, LANE), lambda i, n, c: (i, n, 0, 0)),
             pl.BlockSpec((1, nseq, 1, LANE), lambda i, n, c: (n, i, 0, 0)),
             pl.BlockSpec(memory_space=pl.ANY)]
    out = pl.pallas_call(
        functools.partial(_lru_kernel, nseq=nseq, seq_len=seq_len, nchunks=nchunks),
        grid=(nrb, 16, nchunks),
        in_specs=specs,
        out_specs=[pl.BlockSpec((1, r, LANE), lambda i, n, c: (16 + n, rb0 + i * nchunks + c, 0)),
                   pl.BlockSpec((1, nseq, 1, LANE), lambda i, n, c: (n, i, 0, 0)),
                   pl.BlockSpec((nseq, 1, 8, LANE), lambda i, n, c: (i, n, 0, 0))],
        out_shape=[jax.ShapeDtypeStruct((32, n_rows, LANE), F32),
                   jax.ShapeDtypeStruct((16, nbatch, 1, LANE), F32),
                   jax.ShapeDtypeStruct((nbatch, 16, 8, LANE), F32)],
        scratch_shapes=[pltpu.VMEM((nseq, 1, LANE), F32),
                        pltpu.VMEM((nseq, 1, 8, LANE), F32),
                        pltpu.VMEM((1, r, LANE), F32),
                        pltpu.VMEM((seq_len + 8, LANE), F32)],
        input_output_aliases={len(ins) - 1: 0},
        compiler_params=_params(3),
        name="lru_seq" if nchunks > 1 else "lru_step",
    )(*ins)
    return out[0], out[1], out[2]


def _slabs(v, nblk):
    return v.astype(F32).reshape(nblk, 1, LANE)


def _pad_lanes(a, width=LANE):
    return jnp.pad(a, [(0, 0)] * (a.ndim - 1) + [(0, width - a.shape[-1])])


def _conv_state_in(buf, nblk):
    b = buf.shape[0]
    t = jnp.transpose(buf.astype(F32).reshape(b, 3, nblk, LANE), (0, 2, 1, 3))
    return jnp.pad(t, ((0, 0), (0, 0), (5, 0), (0, 0)))


def _conv_state_out(t):
    b, nblk = t.shape[:2]
    return jnp.transpose(t[:, :, 5:8, :], (0, 2, 1, 3)).reshape(b, 3, nblk * LANE)


EVEN_MAIN = 13312
ODD_GLOW = (6144, 6160)
IN_TN = 1024


def _prep_w_in_even(w):
    nl, d, _ = w.shape
    dt = _pad_lanes(w[:, :, EVEN_MAIN:].reshape(nl, d, SSM_GROUPS, SSM_HPG))
    return w.astype(BF16), dt.reshape(nl, d, SSM_GROUPS * LANE).astype(BF16)


def _prep_w_in_odd(w):
    g0, g1 = ODD_GLOW
    main = jnp.concatenate([w[:, :, :g0], w[:, :, g1:]], axis=2).astype(BF16)
    return main, _pad_lanes(w[:, :, g0:g1]).astype(BF16)


def kernel(x_prompt, x_sample, state_hgrn, state_ssm, state_ssm_conv, state_gla, state_lru,
           state_lru_conv, norm_mix_pre, norm_mix_post, norm_ffn_pre, norm_ffn_post,
           w_in_even, w_out_even, hgrn_lb_logits, hgrn_norm_w,
           ssm_conv_w, ssm_conv_b, ssm_dt_bias, ssm_a_log, ssm_d, ssm_norm_w,
           w_in_odd, w_out_odd, gla_gate_w2, gla_gate_b, gla_norm_w,
           lru_conv_w, lru_conv_b, lru_wa, lru_ba, lru_wx, lru_bx, lru_lambda,
           ffn_w_gate, ffn_w_up, ffn_w_down):
    bp, tp, d = x_prompt.shape
    bs, ts, _ = x_sample.shape
    n_p, n_s = bp * tp, bs * ts
    depth = norm_mix_pre.shape[0]
    x = jnp.concatenate([x_prompt.reshape(n_p, d), x_sample.reshape(n_s, d)], axis=0)

    lb_soft = jax.nn.softmax(hgrn_lb_logits.astype(F32), axis=0)
    hgrn_lb = jnp.maximum(jnp.cumsum(lb_soft, axis=0) - lb_soft[0], 0.0)

    paths = (dict(row0=0, nbatch=bp, t_len=tp), dict(row0=n_p, nbatch=bs, t_len=ts))
    outs = {k: ([], []) for k in ("ssm_cb", "lru", "lru_cb")}
    big = {k: [None, None] for k in ("hg", "ssm", "gla")}
    n_even, n_odd = w_in_even.shape[0], w_in_odd.shape[0]

    we_main, we_dt = _prep_w_in_even(w_in_even)
    wo_main, wo_glow = _prep_w_in_odd(w_in_odd)
    w_out_e, w_out_o = w_out_even.astype(BF16), w_out_odd.astype(BF16)
    wg, wu, wd = ffn_w_gate.astype(BF16), ffn_w_up.astype(BF16), ffn_w_down.astype(BF16)
    st_hgrn = state_hgrn.astype(F32)
    st_ssm = state_ssm.astype(F32).reshape(n_even, bs, 16, LANE, LANE)
    st_gla = state_gla.astype(F32)
    zero_hg = jnp.zeros((1, bp) + state_hgrn.shape[2:], F32)
    zero_ssm = jnp.zeros((1, bp, 16, LANE, LANE), F32)
    zero_gla = jnp.zeros((1, bp) + state_gla.shape[2:], F32)

    for l in range(depth):
        j = l // 2
        if l % 2 == 0:
            p3 = inproj(x, norm_mix_pre[l],
                        [(we_main, j, EVEN_MAIN // IN_TN, IN_TN), (we_dt, j, 1, SSM_GROUPS * LANE)],
                        tn=IN_TN)
            hg_params = (_slabs(hgrn_lb[j], 16), _slabs(hgrn_norm_w[j], 1))
            dsk = jnp.repeat(ssm_d[j].astype(F32), SSM_HEADDIM)
            ssd_params = (
                jnp.transpose(ssm_conv_w[j].astype(F32).reshape(4, 24, LANE), (1, 0, 2)),
                _slabs(ssm_conv_b[j], 24),
                _pad_lanes(ssm_dt_bias[j].astype(F32).reshape(SSM_GROUPS, 1, SSM_HPG)),
                _pad_lanes(ssm_a_log[j].astype(F32).reshape(SSM_GROUPS, 1, SSM_HPG)),
                _slabs(dsk, 16), _slabs(ssm_norm_w[j], 16))
            o3 = None
            for pi, path in enumerate(paths):
                nb = path["nbatch"]
                if pi == 0:
                    s_hg, s_ssm, s_layer = zero_hg, zero_ssm, 0
                    s_cb = jnp.zeros((nb, 24, 8, LANE), F32)
                    nseq_a, hb_a, nseq_b = 1, 16, 1
                else:
                    s_hg, s_ssm, s_layer = st_hgrn, st_ssm, j
                    s_cb = _conv_state_in(state_ssm_conv[j], 24)
                    nseq_a, hb_a, nseq_b = 8, 4, 2
                o3, big["hg"][pi] = gla_mixer(
                    "hgrn", p3, o3, s_hg, s_layer, big["hg"][pi], j, n_even, hg_params,
                    nseq=nseq_a, hb=hb_a, heads=16, dk=128, dv=128, offs=(0, 16, 32, 48), out_off=0,
                    unroll=4, **path)
                o3, big["ssm"][pi], cb_b = ssd_mixer(p3, o3, s_ssm, s_layer, big["ssm"][pi], j,
                                                     n_even, s_cb, ssd_params, nseq=nseq_b, **path)
                outs["ssm_cb"][pi].append(_conv_state_out(cb_b).astype(state_ssm_conv.dtype))
            x = outproj(o3, w_out_e, j, x, norm_mix_post[l])
        else:
            p3 = inproj(x, norm_mix_pre[l],
                        [(wo_main, j, wo_main.shape[2] // IN_TN, IN_TN), (wo_glow, j, 1, LANE)],
                        tn=IN_TN)
            w2 = jnp.pad(gla_gate_w2[j], ((0, LANE - gla_gate_w2.shape[1]), (0, 0)))
            w2 = jnp.transpose(w2.reshape(LANE, 4, 256), (1, 0, 2)).astype(BF16)
            gla_params = (w2, _slabs(gla_gate_b[j], 8), _slabs(gla_norm_w[j], 4))
            lru_params = (
                jnp.transpose(lru_conv_w[j].astype(F32).reshape(4, 16, LANE), (1, 0, 2)),
                _slabs(lru_conv_b[j], 16), lru_wa[j].astype(BF16), _slabs(lru_ba[j], 16),
                lru_wx[j].astype(BF16), _slabs(lru_bx[j], 16), _slabs(lru_lambda[j], 16))
            o3 = None
            for pi, path in enumerate(paths):
                nb = path["nbatch"]
                if pi == 0:
                    s_gla, s_layer = zero_gla, 0
                    s_lru = jnp.zeros((16, nb, 1, LANE), F32)
                    s_cb = jnp.zeros((nb, 16, 8, LANE), F32)
                    nseq_c, hb_c, nseq_d, rows_d = 1, 4, 1, 512
                else:
                    s_gla, s_layer = st_gla, j
                    s_lru = jnp.transpose(state_lru[j].astype(F32).reshape(nb, 16, 1, LANE),
                                          (1, 0, 2, 3))
                    s_cb = _conv_state_in(state_lru_conv[j], 16)
                    nseq_c, hb_c, nseq_d, rows_d = 8, 1, 16, 8
                o3, big["gla"][pi] = gla_mixer(
                    "gla", p3, o3, s_gla, s_layer, big["gla"][pi], j, n_odd, gla_params,
                    nseq=nseq_c, hb=hb_c, heads=4, dk=256, dv=512, offs=(0, 8, 16, 32, 80),
                    out_off=0, unroll=min(4, hb_c), **path)
                o3, st_d, cb_d = lru_mixer(p3, o3, s_lru, s_cb, lru_params, nseq=nseq_d,
                                           rows=rows_d, **path)
                outs["lru"][pi].append(jnp.transpose(st_d, (1, 0, 2, 3)).reshape(nb, 16 * LANE)
                                       .astype(state_lru.dtype))
                outs["lru_cb"][pi].append(_conv_state_out(cb_d).astype(state_lru_conv.dtype))
            x = outproj(o3, w_out_o, j, x, norm_mix_post[l])
        x = ffn(x, norm_ffn_pre[l], wg, wu, wd, l, norm_ffn_post[l])

    y_prompt = x[:n_p].reshape(bp, tp, d).astype(x_prompt.dtype)
    y_sample = x[n_p:].reshape(bs, ts, d).astype(x_sample.dtype)
    ssm_shape = state_ssm.shape[2:]
    res = [y_prompt, y_sample]
    stacked = lambda key, pi: jnp.stack(outs[key][pi])
    res += [big["hg"][0].astype(state_hgrn.dtype), big["hg"][1].astype(state_hgrn.dtype),
            big["ssm"][0].reshape((n_even, bp) + ssm_shape).astype(state_ssm.dtype),
            big["ssm"][1].reshape((n_even, bs) + ssm_shape).astype(state_ssm.dtype),
            stacked("ssm_cb", 0), stacked("ssm_cb", 1),
            big["gla"][0].astype(state_gla.dtype), big["gla"][1].astype(state_gla.dtype),
            stacked("lru", 0), stacked("lru", 1), stacked("lru_cb", 0), stacked("lru_cb", 1)]
    return tuple(res)
```

```python
import functools
import math

import numpy as np
import jax
import jax.numpy as jnp
from jax import lax
from jax.experimental import pallas as pl
from jax.experimental.pallas import tpu as pltpu

F32 = jnp.float32
BF16 = jnp.bfloat16
LANE = 128
VMEM_LIMIT = 56 * 1024 * 1024

DOWN_STRIP = 512

EPS = 1e-6
F_MIN = 1e-30
CHUNK = 64
GLA_GATE_NORMALIZER = 16.0
LRU_C = 8.0
SSM_GROUPS = 4
SSM_HPG = 8
SSM_HEADDIM = 64


def _params(n_axes):
    return pltpu.CompilerParams(dimension_semantics=("arbitrary",) * n_axes,
                                vmem_limit_bytes=VMEM_LIMIT)


def _softplus(x):
    return jnp.maximum(x, 0.0) + jnp.log1p(jnp.exp(-jnp.abs(x)))


def _log_sigmoid(x):
    return jnp.minimum(x, 0.0) - jnp.log1p(jnp.exp(-jnp.abs(x)))


def _split3(x):
    hi = x.astype(BF16)
    r1 = x - hi.astype(F32)
    mid = r1.astype(BF16)
    lo = (r1 - mid.astype(F32)).astype(BF16)
    return jnp.concatenate([hi, mid, lo], axis=1)


def _sum3(y, w):
    return y[:, :w] + y[:, w:2 * w] + y[:, 2 * w:3 * w]


def _cat(ref, base, n):
    if n == 1:
        return ref[base]
    return jnp.concatenate([ref[base + j] for j in range(n)], axis=1)


_NT = (((1,), (1,)), ((), ()))
_TN = (((0,), (0,)), ((), ()))


def _prenorm_rows(x_ref, w_ref, h_scr, tm, rows=64):
    w = w_ref[...]

    def body(i, c):
        r0 = pl.multiple_of(i * rows, rows)
        x = x_ref[pl.ds(r0, rows), :]
        ms = jnp.mean(x * x, axis=1, keepdims=True)
        h_scr[pl.ds(r0, rows), :] = (x * lax.rsqrt(ms + EPS) * w).astype(BF16)
        return c

    lax.fori_loop(0, tm // rows, body, 0)


def _postnorm_rows(o_ref, x_ref, w_ref, tm, rows=64):
    w = w_ref[...]

    def body(i, c):
        r0 = pl.multiple_of(i * rows, rows)
        y = o_ref[pl.ds(r0, rows), :]
        ms = jnp.mean(y * y, axis=1, keepdims=True)
        o_ref[pl.ds(r0, rows), :] = x_ref[pl.ds(r0, rows), :] + y * lax.rsqrt(ms + EPS) * w
        return c

    lax.fori_loop(0, tm // rows, body, 0)


def _resident(shape, index_map):
    return pl.BlockSpec(shape, index_map, pipeline_mode=pl.Buffered(1))


def _inproj_kernel(x_ref, nw_ref, *rest, tm, segs):
    w_refs, o_ref, h_scr = rest[:len(segs)], rest[len(segs)], rest[len(segs) + 1]
    j = pl.program_id(1)

    @pl.when(j == 0)
    def _():
        _prenorm_rows(x_ref, nw_ref, h_scr, tm)

    for w_ref, (start, ntiles, width) in zip(w_refs, segs):
        @pl.when((j >= start) & (j < start + ntiles))
        def _(w_ref=w_ref, width=width):
            r = jnp.dot(h_scr[...], w_ref[0], preferred_element_type=F32)
            for c in range(width // LANE):
                o_ref[c] = r[:, c * LANE:(c + 1) * LANE]


def inproj(x, norm_w, segments, *, tm=512, tn=1024):
    n, d = x.shape
    cb = tn // LANE
    segs, specs, start = [], [], 0
    for w, layer, ntiles, width in segments:
        segs.append((start, ntiles, width))
        specs.append(pl.BlockSpec(
            (1, d, width),
            lambda i, j, layer=layer, start=start, ntiles=ntiles: (layer, 0, jnp.clip(j - start, 0, ntiles - 1))))
        start += ntiles
    return pl.pallas_call(
        functools.partial(_inproj_kernel, tm=tm, segs=tuple(segs)),
        grid=(n // tm, start),
        in_specs=[pl.BlockSpec((tm, d), lambda i, j: (i, 0)),
                  pl.BlockSpec((1, d), lambda i, j: (0, 0))] + specs,
        out_specs=pl.BlockSpec((cb, tm, LANE), lambda i, j: (j, i, 0)),
        out_shape=jax.ShapeDtypeStruct((start * cb, n, LANE), F32),
        scratch_shapes=[pltpu.VMEM((tm, d), BF16)],
        compiler_params=_params(2),
        name="inproj",
    )(x, norm_w.reshape(1, d), *[s[0] for s in segments])


def _outproj_kernel(a_ref, w3_ref, x_ref, nw_ref, o_ref, *, tm, kb, nk):
    k = pl.program_id(1)
    w_ref = w3_ref.at[0]
    a = _cat(a_ref, 0, kb).astype(BF16)

    @pl.when(k == 0)
    def _():
        o_ref[...] = jnp.zeros_like(o_ref)

    for n0 in range(0, o_ref.shape[1], DOWN_STRIP):
        o_ref[:, n0:n0 + DOWN_STRIP] += jnp.dot(a, w_ref[:, n0:n0 + DOWN_STRIP],
                                                preferred_element_type=F32)

    @pl.when(k == nk - 1)
    def _():
        _postnorm_rows(o_ref, x_ref, nw_ref, tm)


def outproj(a3, w, layer, x, norm_w, *, tm=512, tk=512):
    n, d = x.shape
    kdim = w.shape[1]
    kb = tk // LANE
    nk = kdim // tk
    return pl.pallas_call(
        functools.partial(_outproj_kernel, tm=tm, kb=kb, nk=nk),
        grid=(n // tm, nk),
        in_specs=[pl.BlockSpec((kb, tm, LANE), lambda i, k: (k, i, 0)),
                  pl.BlockSpec((1, tk, d), lambda i, k: (layer, k, 0)),
                  pl.BlockSpec((tm, d), lambda i, k: (i, 0)),
                  pl.BlockSpec((1, d), lambda i, k: (0, 0))],
        out_specs=pl.BlockSpec((tm, d), lambda i, k: (i, 0)),
        out_shape=jax.ShapeDtypeStruct((n, d), F32),
        compiler_params=_params(2),
        name="outproj",
    )(a3, w, x, norm_w.reshape(1, d))


def _ffn_kernel(x_ref, pre_ref, wg_ref, wu_ref, wd3_ref, post_ref, o_ref, h_scr, a_scr, *, tm, nf):
    f = pl.program_id(1)
    wd_ref = wd3_ref.at[0]

    @pl.when(f == 0)
    def _():
        _prenorm_rows(x_ref, pre_ref, h_scr, tm)
        o_ref[...] = jnp.zeros_like(o_ref)
        a_scr[...] = jnp.zeros_like(a_scr)

    a_prev = a_scr[...]
    h = h_scr[...]
    g = jnp.dot(h, wg_ref[0], preferred_element_type=F32)
    u = jnp.dot(h, wu_ref[0], preferred_element_type=F32)
    for n0 in range(0, o_ref.shape[1], DOWN_STRIP):
        o_ref[:, n0:n0 + DOWN_STRIP] += jnp.dot(a_prev, wd_ref[:, n0:n0 + DOWN_STRIP],
                                                preferred_element_type=F32)
    a_scr[...] = (jax.nn.silu(g) * u).astype(BF16)

    @pl.when(f == nf)
    def _():
        _postnorm_rows(o_ref, x_ref, post_ref, tm)


def ffn(x, pre_w, wg, wu, wd, layer, post_w, *, tm=512, tf=256):
    n, d = x.shape
    dff = wg.shape[2]
    nf = dff // tf
    return pl.pallas_call(
        functools.partial(_ffn_kernel, tm=tm, nf=nf),
        grid=(n // tm, nf + 1),
        in_specs=[pl.BlockSpec((tm, d), lambda i, f: (i, 0)),
                  pl.BlockSpec((1, d), lambda i, f: (0, 0)),
                  pl.BlockSpec((1, d, tf), lambda i, f: (layer, 0, jnp.minimum(f, nf - 1))),
                  pl.BlockSpec((1, d, tf), lambda i, f: (layer, 0, jnp.minimum(f, nf - 1))),
                  pl.BlockSpec((1, tf, d), lambda i, f: (layer, jnp.maximum(f - 1, 0), 0)),
                  pl.BlockSpec((1, d), lambda i, f: (0, 0))],
        out_specs=pl.BlockSpec((tm, d), lambda i, f: (i, 0)),
        out_shape=jax.ShapeDtypeStruct((n, d), F32),
        scratch_shapes=[pltpu.VMEM((tm, d), BF16), pltpu.VMEM((tm, tf), BF16)],
        compiler_params=_params(2),
        name="ffn",
    )(x, pre_w.reshape(1, d), wg, wu, wd, post_w.reshape(1, d))


def _gla_consts(nseq, seq_len):
    r = nseq * seq_len
    t = np.arange(r)[:, None]
    s = np.arange(r)[None, :]
    prefix = ((t // seq_len) == (s // seq_len)) & (s <= t)
    masks = []
    c = seq_len // 2
    while c >= 1:
        right = (t % (2 * c)) >= c
        masks.append(((t // (2 * c)) == (s // (2 * c))) & right & ((s % (2 * c)) < c))
        c //= 2
    m = np.stack(masks).astype(np.float32)
    return jnp.asarray(prefix.astype(np.float32), BF16), jnp.asarray(m, F32)


def _block_row(b, blk, pick, pos):
    r, w = b.shape
    if blk >= 8:
        return jnp.concatenate(
            [jnp.broadcast_to(b[s + pick:s + pick + 1, :], (blk, w)) for s in range(0, r, blk)], axis=0)
    out = b
    for p in range(blk):
        if p != pick:
            out = jnp.where(pos == p, pltpu.roll(b, (p - pick) % r, axis=0), out)
    return out


def _gla_kernel(*refs, mode, nseq, seq_len, hb, dk, dv, nchunks, unroll):
    if mode == "hgrn":
        (q_ref, f_ref, v_ref, gt_ref, lb_ref, nw_ref, a_ref, m_ref, s0_ref, _, _,
         o_ref, so_ref, s_scr) = refs
    else:
        (q_ref, k_ref, v_ref, gt_ref, gl_ref, w2_ref, gb_ref, nw_ref, a_ref, m_ref, s0_ref, _, _,
         o_ref, so_ref, s_scr) = refs
    dkb, dvb = dk // LANE, dv // LANE
    r = nseq * seq_len
    nlev = int(math.log2(seq_len))
    c = pl.program_id(2)

    @pl.when(c == 0)
    def _():
        s_scr[...] = s0_ref[0]

    a_mat = a_ref[...]
    seq_shift = int(math.log2(seq_len))
    rowk = lax.broadcasted_iota(jnp.int32, (r, dk), 0)
    seq_of_row = lax.broadcasted_iota(jnp.int32, (r, LANE), 0) >> seq_shift
    lane_id = lax.broadcasted_iota(jnp.int32, (r, LANE), 1)
    seqsel = (seq_of_row == lane_id).astype(BF16)
    nw = _cat(nw_ref, 0, dvb)

    def head(h, carry):
        qr = _cat(q_ref, h * dkb, dkb)
        v = _cat(v_ref, h * dvb, dvb)
        gate = _cat(gt_ref, h * dvb, dvb)
        if mode == "hgrn":
            lb = _cat(lb_ref, h * dkb, dkb)
            fg = lb + (1.0 - lb) * jax.nn.sigmoid(_cat(f_ref, h * dkb, dkb))
            g = jnp.log(jnp.maximum(fg, F_MIN))
            k = 1.0 - fg
            q = jax.nn.silu(qr) * (dk ** -0.5)
        else:
            k = _cat(k_ref, h * dkb, dkb)
            q = qr * (dk ** -0.5)
            lin = jnp.dot(gl_ref[0].astype(BF16), w2_ref[h], preferred_element_type=F32)
            g = _log_sigmoid(lin + _cat(gb_ref, h * dkb, dkb)) / GLA_GATE_NORMALIZER

        g3 = _split3(g)
        b = _sum3(jnp.dot(a_mat, g3, preferred_element_type=F32), dk)

        scores = jnp.zeros((r, r), F32)
        for l in range(nlev):
            half = seq_len >> (l + 1)
            pos = rowk & (2 * half - 1)
            right = pos >= half
            b_m = _block_row(b, 2 * half, half - 1, pos)
            x = (jnp.where(right, q, k) * jnp.exp(-jnp.abs(b - b_m))).astype(BF16)
            scores = scores + m_ref[l] * lax.dot_general(x, x, _NT, preferred_element_type=F32)

        vb = v.astype(BF16)
        o = jnp.dot(scores.astype(BF16), vb, preferred_element_type=F32)
        o = o + jnp.sum(q * k, axis=1, keepdims=True) * v

        qb = q * jnp.exp(b)
        kb = k * jnp.exp(_block_row(b, seq_len, seq_len - 1, None) - b)
        d3 = lax.dot_general(g3, seqsel, _TN, preferred_element_type=F32)
        dcol = d3[0:dk] + d3[dk:2 * dk] + d3[2 * dk:3 * dk]
        for s in range(nseq):
            if nseq == 1:
                qs, ks = qb, kb
            else:
                in_seq = (rowk >> seq_shift) == s
                qs = jnp.where(in_seq, qb, 0.0)
                ks = jnp.where(in_seq, kb, 0.0)
            st = s_scr[s, h]
            o = o + jnp.dot(qs.astype(BF16), st.astype(BF16), preferred_element_type=F32)
            dec = jnp.exp(jnp.broadcast_to(dcol[:, s:s + 1], (dk, dv)))
            s_scr[s, h] = dec * st + lax.dot_general(ks.astype(BF16), vb, _TN,
                                                     preferred_element_type=F32)

        ms = jnp.mean(o * o, axis=1, keepdims=True)
        y = o * lax.rsqrt(ms + EPS) * nw * jax.nn.silu(gate)
        for j in range(dvb):
            o_ref[h * dvb + j] = y[:, j * LANE:(j + 1) * LANE]
        return carry

    lax.fori_loop(0, hb, head, 0, unroll=unroll)

    @pl.when(c == nchunks - 1)
    def _():
        so_ref[0] = s_scr[...]


def _alias_or_dummy(arr, ins, specs, aliases, out_idx):
    if arr is None:
        ins.append(jnp.zeros((8, LANE), F32))
    else:
        ins.append(arr)
        aliases[len(ins) - 1] = out_idx
    specs.append(pl.BlockSpec(memory_space=pl.ANY))


def gla_mixer(mode, p3, o3, s0, s0_layer, st_all, layer, n_layers, params, *, row0, nbatch, t_len,
              nseq, hb, heads, dk, dv, offs, out_off, unroll):
    seq_len = min(CHUNK, t_len)
    nchunks = t_len // seq_len
    r = nseq * seq_len
    dkb, dvb = dk // LANE, dv // LANE
    nrb = nbatch // nseq
    rb0 = row0 // r
    a_mat, masks = _gla_consts(nseq, seq_len)
    n_rows = p3.shape[1]

    def slab(nblk, off):
        return pl.BlockSpec((nblk, r, LANE),
                            lambda i, hg, c: (off // nblk + hg, rb0 + i * nchunks + c, 0))

    def const(arr):
        nd = arr.ndim
        return pl.BlockSpec(arr.shape, lambda i, hg, c: (0,) * nd)

    def per_head(arr, nblk):
        return pl.BlockSpec((nblk, 1, LANE), lambda i, hg, c: (hg, 0, 0))

    if mode == "hgrn":
        lb, nw = params
        ins = [p3, p3, p3, p3, lb, nw, a_mat, masks, s0]
        specs = [slab(hb * dkb, offs[0]), slab(hb * dkb, offs[1]), slab(hb * dvb, offs[2]),
                 slab(hb * dvb, offs[3]), per_head(lb, hb * dkb), const(nw), const(a_mat),
                 const(masks)]
    else:
        w2, gb, nw = params
        ins = [p3, p3, p3, p3, p3, w2, gb, nw, a_mat, masks, s0]
        specs = [slab(hb * dkb, offs[0]), slab(hb * dkb, offs[1]), slab(hb * dvb, offs[2]),
                 slab(hb * dvb, offs[3]),
                 pl.BlockSpec((1, r, LANE), lambda i, hg, c: (offs[4], rb0 + i * nchunks + c, 0)),
                 pl.BlockSpec((hb, LANE, dk), lambda i, hg, c: (hg, 0, 0)),
                 per_head(gb, hb * dkb), const(nw), const(a_mat), const(masks)]
    specs.append(pl.BlockSpec((1, nseq, hb, dk, dv), lambda i, hg, c: (s0_layer, i, hg, 0, 0)))
    aliases = {}
    _alias_or_dummy(o3, ins, specs, aliases, 0)
    _alias_or_dummy(st_all, ins, specs, aliases, 1)

    out = pl.pallas_call(
        functools.partial(_gla_kernel, mode=mode, nseq=nseq, seq_len=seq_len, hb=hb, dk=dk, dv=dv,
                          nchunks=nchunks, unroll=unroll),
        grid=(nrb, heads // hb, nchunks),
        in_specs=specs,
        out_specs=[pl.BlockSpec((hb * dvb, r, LANE),
                                lambda i, hg, c: (out_off // (hb * dvb) + hg, rb0 + i * nchunks + c, 0)),
                   pl.BlockSpec((1, nseq, hb, dk, dv), lambda i, hg, c: (layer, i, hg, 0, 0))],
        out_shape=[jax.ShapeDtypeStruct((32, n_rows, LANE), F32),
                   jax.ShapeDtypeStruct((n_layers, nbatch, heads, dk, dv), F32)],
        scratch_shapes=[pltpu.VMEM((nseq, hb, dk, dv), F32)],
        input_output_aliases=aliases,
        compiler_params=_params(3),
        name=mode + ("_seq" if nchunks > 1 else "_step"),
    )(*ins)
    return out[0], out[1]


def _conv_block(ref, i, cbi, cw_ref, cb_ref, tail, ext, xcs, nseq, seq_len, act):
    w = cw_ref[cbi]
    b = cb_ref[cbi]
    for s in range(nseq):
        ext[0:8, :] = tail[s, cbi]
        ext[8:8 + seq_len, :] = ref[i, s * seq_len:(s + 1) * seq_len, :]
        y = (b + w[3:4] * ext[8:8 + seq_len, :] + w[2:3] * ext[7:7 + seq_len, :]
             + w[1:2] * ext[6:6 + seq_len, :] + w[0:1] * ext[5:5 + seq_len, :])
        xcs[cbi, s * seq_len:(s + 1) * seq_len, :] = act(y)
        tail[s, cbi] = ext[seq_len:seq_len + 8, :]


def _ssd_kernel(z_ref, xa_ref, xb_ref, xc_ref, dt_ref, cw_ref, cb_ref, dtb_ref, alog_ref, dsk_ref,
                nw_ref, t_ref, tt_ref, cbuf0_ref, h0_ref, _o3_any, _h_any, o_ref, hout_ref, cbout_ref,
                h_scr, tail, xcs, ext, *, nseq, seq_len, nchunks):
    r = nseq * seq_len
    c = pl.program_id(1)

    @pl.when(c == 0)
    def _():
        h_scr[...] = h0_ref[0]
        tail[...] = cbuf0_ref[...]

    for part, ref in enumerate((xa_ref, xb_ref, xc_ref)):
        def body(i, carry, part=part, ref=ref):
            _conv_block(ref, i, part * 8 + i, cw_ref, cb_ref, tail, ext, xcs, nseq, seq_len,
                        jax.nn.silu)
            return carry
        lax.fori_loop(0, 8, body, 0)

    lane = lax.broadcasted_iota(jnp.int32, (r, LANE), 1)
    lo = lane < SSM_HEADDIM
    row_lo = lax.broadcasted_iota(jnp.int32, (LANE, LANE), 0) < SSM_HEADDIM
    seq_of_row = lax.broadcasted_iota(jnp.int32, (r, LANE), 0) >> int(math.log2(seq_len))
    tmat = t_ref[...]
    ttmat = tt_ref[...]
    causal = tmat.astype(F32) > 0.0

    def bcast_col(arr, j):
        return jnp.broadcast_to(arr[:, j:j + 1], (r, LANE))

    def group(g, carry):
        dt = _softplus(dt_ref[g] + dtb_ref[g])
        dta = dt * (-jnp.exp(alog_ref[g]))
        d3 = _split3(dta)
        cum = _sum3(jnp.dot(tmat, d3, preferred_element_type=F32), LANE)
        ct3 = lax.dot_general(d3, ttmat, _TN, preferred_element_type=F32)
        cum_t = ct3[0:LANE] + ct3[LANE:2 * LANE] + ct3[2 * LANE:3 * LANE]
        b_g = xcs[16 + g]
        c_g = xcs[20 + g]
        b_b = b_g.astype(BF16)
        cb_m = lax.dot_general(c_g.astype(BF16), b_b, _NT, preferred_element_type=F32)

        def decay_mat(j):
            rel = (jnp.broadcast_to(cum[:, j:j + 1], (r, r))
                   - jnp.broadcast_to(cum_t[j:j + 1, :], (r, r)))
            dec = jnp.where(causal, jnp.exp(jnp.where(causal, rel, 0.0)), 0.0)
            return (cb_m * dec).astype(BF16)

        ys = []
        for jj in range(4):
            j0, j1 = 2 * jj, 2 * jj + 1
            cbi = g * 4 + jj
            x_cb = xcs[cbi]
            dt_e = jnp.where(lo, bcast_col(dt, j0), bcast_col(dt, j1))
            cum_e = jnp.where(lo, bcast_col(cum, j0), bcast_col(cum, j1))
            u = x_cb * dt_e
            u_b = u.astype(BF16)
            y = jnp.where(lo,
                          jnp.dot(decay_mat(j0), u_b, preferred_element_type=F32),
                          jnp.dot(decay_mat(j1), u_b, preferred_element_type=F32))
            y_in = jnp.zeros((r, LANE), F32)
            for s in range(nseq):
                rl = s * seq_len + seq_len - 1
                h_cb = h_scr[s, cbi]
                if nseq == 1:
                    c_s = c_g
                    rel = cum_e[rl:rl + 1, :] - cum_e
                    uw = u * jnp.exp(rel)
                else:
                    in_seq = seq_of_row == s
                    c_s = jnp.where(in_seq, c_g, 0.0)
                    rel = jnp.where(in_seq, cum_e[rl:rl + 1, :] - cum_e, 0.0)
                    uw = jnp.where(in_seq, u * jnp.exp(rel), 0.0)
                y_in = y_in + lax.dot_general(c_s.astype(BF16), h_cb.astype(BF16), _NT,
                                              preferred_element_type=F32)
                last = jnp.where(row_lo,
                                 jnp.broadcast_to(cum[rl:rl + 1, j0:j0 + 1], (LANE, LANE)),
                                 jnp.broadcast_to(cum[rl:rl + 1, j1:j1 + 1], (LANE, LANE)))
                h_scr[s, cbi] = jnp.exp(last) * h_cb + lax.dot_general(
                    uw.astype(BF16), b_b, _TN, preferred_element_type=F32)
            y = y + y_in * jnp.exp(cum_e) + dsk_ref[cbi] * x_cb
            ys.append(y * jax.nn.silu(z_ref[cbi]))
        ms = sum(jnp.sum(y * y, axis=1, keepdims=True) for y in ys) * (1.0 / (4 * LANE))
        rinv = lax.rsqrt(ms + EPS)
        for jj in range(4):
            o_ref[g * 4 + jj] = ys[jj] * rinv * nw_ref[g * 4 + jj]
        return carry

    lax.fori_loop(0, SSM_GROUPS, group, 0, unroll=True)

    @pl.when(c == nchunks - 1)
    def _():
        hout_ref[0] = h_scr[...]
        cbout_ref[...] = tail[...]


def _tri_consts(nseq, seq_len):
    r = nseq * seq_len
    t = np.arange(r)[:, None]
    s = np.arange(r)[None, :]
    m = (((t // seq_len) == (s // seq_len)) & (s <= t)).astype(np.float32)
    return jnp.asarray(m, BF16), jnp.asarray(m.T, BF16)


def ssd_mixer(p3, o3, h0, h0_layer, h_all, layer, n_layers, cbuf0, params, *, row0, nbatch, t_len,
              nseq):
    seq_len = min(CHUNK, t_len)
    nchunks = t_len // seq_len
    r = nseq * seq_len
    nrb = nbatch // nseq
    rb0 = row0 // r
    cw, cbias, dtb, alog, dsk, nw = params
    tmat, ttmat = _tri_consts(nseq, seq_len)
    n_rows = p3.shape[1]

    def slab(nblk, blk_idx):
        return pl.BlockSpec((nblk, r, LANE), lambda i, c: (blk_idx, rb0 + i * nchunks + c, 0))

    def const(arr):
        nd = arr.ndim
        return pl.BlockSpec(arr.shape, lambda i, c: (0,) * nd)

    ins = [p3, p3, p3, p3, p3, cw, cbias, dtb, alog, dsk, nw, tmat, ttmat, cbuf0, h0]
    specs = [slab(16, 4), slab(8, 10), slab(8, 11), slab(8, 12), slab(4, 26),
             const(cw), const(cbias), const(dtb), const(alog), const(dsk), const(nw),
             const(tmat), const(ttmat),
             pl.BlockSpec((nseq, 24, 8, LANE), lambda i, c: (i, 0, 0, 0)),
             pl.BlockSpec((1, nseq, 16, LANE, LANE), lambda i, c: (h0_layer, i, 0, 0, 0))]
    aliases = {}
    _alias_or_dummy(o3, ins, specs, aliases, 0)
    _alias_or_dummy(h_all, ins, specs, aliases, 1)
    out = pl.pallas_call(
        functools.partial(_ssd_kernel, nseq=nseq, seq_len=seq_len, nchunks=nchunks),
        grid=(nrb, nchunks),
        in_specs=specs,
        out_specs=[pl.BlockSpec((16, r, LANE), lambda i, c: (1, rb0 + i * nchunks + c, 0)),
                   pl.BlockSpec((1, nseq, 16, LANE, LANE), lambda i, c: (layer, i, 0, 0, 0)),
                   pl.BlockSpec((nseq, 24, 8, LANE), lambda i, c: (i, 0, 0, 0))],
        out_shape=[jax.ShapeDtypeStruct((32, n_rows, LANE), F32),
                   jax.ShapeDtypeStruct((n_layers, nbatch, 16, LANE, LANE), F32),
                   jax.ShapeDtypeStruct((nbatch, 24, 8, LANE), F32)],
        scratch_shapes=[pltpu.VMEM((nseq, 16, LANE, LANE), F32),
                        pltpu.VMEM((nseq, 24, 8, LANE), F32),
                        pltpu.VMEM((24, r, LANE), F32),
                        pltpu.VMEM((seq_len + 8, LANE), F32)],
        input_output_aliases=aliases,
        compiler_params=_params(2),
        name="ssd_seq" if nchunks > 1 else "ssd_step",
    )(*ins)
    return out[0], out[1], out[2]


def _lru_kernel(x_ref, y_ref, cw_ref, cb_ref, wa_ref, ba_ref, wx_ref, bx_ref, lam_ref, cbuf0_ref,
                h0_ref, _, o_ref, hout_ref, cbout_ref, h_scr, tail, xcs, ext,
                *, nseq, seq_len, nchunks):
    r = nseq * seq_len
    sub = min(seq_len, CHUNK)
    c = pl.program_id(2)

    @pl.when(c == 0)
    def _():
        h_scr[...] = h0_ref[0]
        tail[...] = cbuf0_ref[...]

    _conv_block(x_ref, 0, 0, cw_ref, cb_ref, tail, ext, xcs, nseq, seq_len, lambda v: v)
    xc = xcs[0]
    xc_b = xc.astype(BF16)
    rg = jax.nn.sigmoid(jnp.dot(xc_b, wa_ref[0], preferred_element_type=F32) + ba_ref[0])
    ig = jax.nn.sigmoid(jnp.dot(xc_b, wx_ref[0], preferred_element_type=F32) + bx_ref[0])
    log_a = -LRU_C * rg * _softplus(-lam_ref[0])
    a_cum = jnp.exp(log_a)
    th = jnp.tanh(log_a)
    u_cum = jnp.sqrt(-2.0 * th / (1.0 - th)) * (ig * xc)

    pos = lax.broadcasted_iota(jnp.int32, (r, LANE), 0) & (sub - 1)
    d = 1
    while d < sub:
        take = pos >= d
        a_prev = pltpu.roll(a_cum, d, axis=0)
        u_prev = pltpu.roll(u_cum, d, axis=0)
        u_cum = jnp.where(take, a_cum * u_prev + u_cum, u_cum)
        a_cum = jnp.where(take, a_cum * a_prev, a_cum)
        d *= 2

    gel = jax.nn.gelu(y_ref[0])
    for s in range(nseq):
        h_prev = h_scr[s]
        for sb in range(seq_len // sub):
            r0 = s * seq_len + sb * sub
            hs = a_cum[r0:r0 + sub] * h_prev + u_cum[r0:r0 + sub]
            o_ref[0, r0:r0 + sub, :] = hs * gel[r0:r0 + sub]
            h_prev = hs[sub - 1:sub]
        h_scr[s] = h_prev

    @pl.when(c == nchunks - 1)
    def _():
        hout_ref[0] = h_scr[...]
        cbout_ref[...] = tail[...]


def lru_mixer(p3, o3, h0, cbuf0, params, *, row0, nbatch, t_len, nseq, rows):
    seq_len = min(rows, t_len)
    nchunks = t_len // seq_len
    r = nseq * seq_len
    nrb = nbatch // nseq
    rb0 = row0 // r
    cw, cbias, wa, ba, wx, bx, lam = params
    n_rows = p3.shape[1]

    def slab(off):
        return pl.BlockSpec((1, r, LANE), lambda i, n, c: (off + n, rb0 + i * nchunks + c, 0))

    def per_blk(arr):
        shp = (1,) + arr.shape[1:]
        nd = arr.ndim
        return pl.BlockSpec(shp, lambda i, n, c: (n,) + (0,) * (nd - 1))

    ins = [p3, p3, cw, cbias, wa, ba, wx, bx, lam, cbuf0, h0, o3]
    specs = [slab(48), slab(64), per_blk(cw), per_blk(cbias), per_blk(wa), per_blk(ba),
             per_blk(wx), per_blk(bx), per_blk(lam),
             pl.BlockSpec((nseq, 1, 8, LANE), lambda i, n, c: (i, n, 0, 0)),
             pl.BlockSpec((1, nseq, 1, LANE), lambda i, n, c: (n, i, 0, 0)),
             pl.BlockSpec(memory_space=pl.ANY)]
    out = pl.pallas_call(
        functools.partial(_lru_kernel, nseq=nseq, seq_len=seq_len, nchunks=nchunks),
        grid=(nrb, 16, nchunks),
        in_specs=specs,
        out_specs=[pl.BlockSpec((1, r, LANE), lambda i, n, c: (16 + n, rb0 + i * nchunks + c, 0)),
                   pl.BlockSpec((1, nseq, 1, LANE), lambda i, n, c: (n, i, 0, 0)),
                   pl.BlockSpec((nseq, 1, 8, LANE), lambda i, n, c: (i, n, 0, 0))],
        out_shape=[jax.ShapeDtypeStruct((32, n_rows, LANE), F32),
                   jax.ShapeDtypeStruct((16, nbatch, 1, LANE), F32),
                   jax.ShapeDtypeStruct((nbatch, 16, 8, LANE), F32)],
        scratch_shapes=[pltpu.VMEM((nseq, 1, LANE), F32),
                        pltpu.VMEM((nseq, 1, 8, LANE), F32),
                        pltpu.VMEM((1, r, LANE), F32),
                        pltpu.VMEM((seq_len + 8, LANE), F32)],
        input_output_aliases={len(ins) - 1: 0},
        compiler_params=_params(3),
        name="lru_seq" if nchunks > 1 else "lru_step",
    )(*ins)
    return out[0], out[1], out[2]


def _slabs(v, nblk):
    return v.astype(F32).reshape(nblk, 1, LANE)


def _pad_lanes(a, width=LANE):
    return jnp.pad(a, [(0, 0)] * (a.ndim - 1) + [(0, width - a.shape[-1])])


def _conv_state_in(buf, nblk):
    b = buf.shape[0]
    t = jnp.transpose(buf.astype(F32).reshape(b, 3, nblk, LANE), (0, 2, 1, 3))
    return jnp.pad(t, ((0, 0), (0, 0), (5, 0), (0, 0)))


def _conv_state_out(t):
    b, nblk = t.shape[:2]
    return jnp.transpose(t[:, :, 5:8, :], (0, 2, 1, 3)).reshape(b, 3, nblk * LANE)


EVEN_MAIN = 13312
ODD_GLOW = (6144, 6160)
IN_TN = 1024


def _prep_w_in_even(w):
    nl, d, _ = w.shape
    dt = _pad_lanes(w[:, :, EVEN_MAIN:].reshape(nl, d, SSM_GROUPS, SSM_HPG))
    return w.astype(BF16), dt.reshape(nl, d, SSM_GROUPS * LANE).astype(BF16)


def _prep_w_in_odd(w):
    g0, g1 = ODD_GLOW
    main = jnp.concatenate([w[:, :, :g0], w[:, :, g1:]], axis=2).astype(BF16)
    return main, _pad_lanes(w[:, :, g0:g1]).astype(BF16)


def kernel(x_prompt, x_sample, state_hgrn, state_ssm, state_ssm_conv, state_gla, state_lru,
           state_lru_conv, norm_mix_pre, norm_mix_post, norm_ffn_pre, norm_ffn_post,
           w_in_even, w_out_even, hgrn_lb_logits, hgrn_norm_w,
           ssm_conv_w, ssm_conv_b, ssm_dt_bias, ssm_a_log, ssm_d, ssm_norm_w,
           w_in_odd, w_out_odd, gla_gate_w2, gla_gate_b, gla_norm_w,
           lru_conv_w, lru_conv_b, lru_wa, lru_ba, lru_wx, lru_bx, lru_lambda,
           ffn_w_gate, ffn_w_up, ffn_w_down):
    bp, tp, d = x_prompt.shape
    bs, ts, _ = x_sample.shape
    n_p, n_s = bp * tp, bs * ts
    depth = norm_mix_pre.shape[0]
    x = jnp.concatenate([x_prompt.reshape(n_p, d), x_sample.reshape(n_s, d)], axis=0)

    lb_soft = jax.nn.softmax(hgrn_lb_logits.astype(F32), axis=0)
    hgrn_lb = jnp.maximum(jnp.cumsum(lb_soft, axis=0) - lb_soft[0], 0.0)

    paths = (dict(row0=0, nbatch=bp, t_len=tp), dict(row0=n_p, nbatch=bs, t_len=ts))
    outs = {k: ([], []) for k in ("ssm_cb", "lru", "lru_cb")}
    big = {k: [None, None] for k in ("hg", "ssm", "gla")}
    n_even, n_odd = w_in_even.shape[0], w_in_odd.shape[0]

    we_main, we_dt = _prep_w_in_even(w_in_even)
    wo_main, wo_glow = _prep_w_in_odd(w_in_odd)
    w_out_e, w_out_o = w_out_even.astype(BF16), w_out_odd.astype(BF16)
    wg, wu, wd = ffn_w_gate.astype(BF16), ffn_w_up.astype(BF16), ffn_w_down.astype(BF16)
    st_hgrn = state_hgrn.astype(F32)
    st_ssm = state_ssm.astype(F32).reshape(n_even, bs, 16, LANE, LANE)
    st_gla = state_gla.astype(F32)
    zero_hg = jnp.zeros((1, bp) + state_hgrn.shape[2:], F32)
    zero_ssm = jnp.zeros((1, bp, 16, LANE, LANE), F32)
    zero_gla = jnp.zeros((1, bp) + state_gla.shape[2:], F32)

    for l in range(depth):
        j = l // 2
        if l % 2 == 0:
            p3 = inproj(x, norm_mix_pre[l],
                        [(we_main, j, EVEN_MAIN // IN_TN, IN_TN), (we_dt, j, 1, SSM_GROUPS * LANE)],
                        tn=IN_TN)
            hg_params = (_slabs(hgrn_lb[j], 16), _slabs(hgrn_norm_w[j], 1))
            dsk = jnp.repeat(ssm_d[j].astype(F32), SSM_HEADDIM)
            ssd_params = (
                jnp.transpose(ssm_conv_w[j].astype(F32).reshape(4, 24, LANE), (1, 0, 2)),
                _slabs(ssm_conv_b[j], 24),
                _pad_lanes(ssm_dt_bias[j].astype(F32).reshape(SSM_GROUPS, 1, SSM_HPG)),
                _pad_lanes(ssm_a_log[j].astype(F32).reshape(SSM_GROUPS, 1, SSM_HPG)),
                _slabs(dsk, 16), _slabs(ssm_norm_w[j], 16))
            o3 = None
            for pi, path in enumerate(paths):
                nb = path["nbatch"]
                if pi == 0:
                    s_hg, s_ssm, s_layer = zero_hg, zero_ssm, 0
                    s_cb = jnp.zeros((nb, 24, 8, LANE), F32)
                    nseq_a, hb_a, unroll_a, nseq_b = 1, 16, 8, 1
                else:
                    s_hg, s_ssm, s_layer = st_hgrn, st_ssm, j
                    s_cb = _conv_state_in(state_ssm_conv[j], 24)
                    nseq_a, hb_a, unroll_a, nseq_b = 8, 8, 4, 4
                o3, big["hg"][pi] = gla_mixer(
                    "hgrn", p3, o3, s_hg, s_layer, big["hg"][pi], j, n_even, hg_params,
                    nseq=nseq_a, hb=hb_a, heads=16, dk=128, dv=128, offs=(0, 16, 32, 48), out_off=0,
                    unroll=unroll_a, **path)
                o3, big["ssm"][pi], cb_b = ssd_mixer(p3, o3, s_ssm, s_layer, big["ssm"][pi], j,
                                                     n_even, s_cb, ssd_params, nseq=nseq_b, **path)
                outs["ssm_cb"][pi].append(_conv_state_out(cb_b).astype(state_ssm_conv.dtype))
            x = outproj(o3, w_out_e, j, x, norm_mix_post[l])
        else:
            p3 = inproj(x, norm_mix_pre[l],
                        [(wo_main, j, wo_main.shape[2] // IN_TN, IN_TN), (wo_glow, j, 1, LANE)],
                        tn=IN_TN)
            w2 = jnp.pad(gla_gate_w2[j], ((0, LANE - gla_gate_w2.shape[1]), (0, 0)))
            w2 = jnp.transpose(w2.reshape(LANE, 4, 256), (1, 0, 2)).astype(BF16)
            gla_params = (w2, _slabs(gla_gate_b[j], 8), _slabs(gla_norm_w[j], 4))
            lru_params = (
                jnp.transpose(lru_conv_w[j].astype(F32).reshape(4, 16, LANE), (1, 0, 2)),
                _slabs(lru_conv_b[j], 16), lru_wa[j].astype(BF16), _slabs(lru_ba[j], 16),
                lru_wx[j].astype(BF16), _slabs(lru_bx[j], 16), _slabs(lru_lambda[j], 16))
            o3 = None
            for pi, path in enumerate(paths):
                nb = path["nbatch"]
                if pi == 0:
                    s_gla, s_layer = zero_gla, 0
                    s_lru = jnp.zeros((16, nb, 1, LANE), F32)
                    s_cb = jnp.zeros((nb, 16, 8, LANE), F32)
                    nseq_c, hb_c, nseq_d, rows_d = 1, 4, 1, 512
                else:
                    s_gla, s_layer = st_gla, j
                    s_lru = jnp.transpose(state_lru[j].astype(F32).reshape(nb, 16, 1, LANE),
                                          (1, 0, 2, 3))
                    s_cb = _conv_state_in(state_lru_conv[j], 16)
                    nseq_c, hb_c, nseq_d, rows_d = 8, 1, 16, 8
                o3, big["gla"][pi] = gla_mixer(
                    "gla", p3, o3, s_gla, s_layer, big["gla"][pi], j, n_odd, gla_params,
                    nseq=nseq_c, hb=hb_c, heads=4, dk=256, dv=512, offs=(0, 8, 16, 32, 80),
                    out_off=0, unroll=min(4, hb_c), **path)
                o3, st_d, cb_d = lru_mixer(p3, o3, s_lru, s_cb, lru_params, nseq=nseq_d,
                                           rows=rows_d, **path)
                outs["lru"][pi].append(jnp.transpose(st_d, (1, 0, 2, 3)).reshape(nb, 16 * LANE)
                                       .astype(state_lru.dtype))
                outs["lru_cb"][pi].append(_conv_state_out(cb_d).astype(state_lru_conv.dtype))
            x = outproj(o3, w_out_o, j, x, norm_mix_post[l])
        x = ffn(x, norm_ffn_pre[l], wg, wu, wd, l, norm_ffn_post[l])

    y_prompt = x[:n_p].reshape(bp, tp, d).astype(x_prompt.dtype)
    y_sample = x[n_p:].reshape(bs, ts, d).astype(x_sample.dtype)
    ssm_shape = state_ssm.shape[2:]
    res = [y_prompt, y_sample]
    stacked = lambda key, pi: jnp.stack(outs[key][pi])
    res += [big["hg"][0].astype(state_hgrn.dtype), big["hg"][1].astype(state_hgrn.dtype),
            big["ssm"][0].reshape((n_even, bp) + ssm_shape).astype(state_ssm.dtype),
            big["ssm"][1].reshape((n_even, bs) + ssm_shape).astype(state_ssm.dtype),
            stacked("ssm_cb", 0), stacked("ssm_cb", 1),
            big["gla"][0].astype(state_gla.dtype), big["gla"][1].astype(state_gla.dtype),
            stacked("lru", 0), stacked("lru", 1), stacked("lru_cb", 0), stacked("lru_cb", 1)]
    return tuple(res)
```

```python
import functools
import math

import numpy as np
import jax
import jax.numpy as jnp
from jax import lax
from jax.experimental import pallas as pl
from jax.experimental.pallas import tpu as pltpu

F32 = jnp.float32
BF16 = jnp.bfloat16
LANE = 128
VMEM_LIMIT = 56 * 1024 * 1024

DOWN_STRIP = 512

EPS = 1e-6
F_MIN = 1e-30
CHUNK = 64
GLA_GATE_NORMALIZER = 16.0
LRU_C = 8.0
SSM_GROUPS = 4
SSM_HPG = 8
SSM_HEADDIM = 64


def _params(n_axes):
    return pltpu.CompilerParams(dimension_semantics=("arbitrary",) * n_axes,
                                vmem_limit_bytes=VMEM_LIMIT)


def _softplus(x):
    return jnp.maximum(x, 0.0) + jnp.log1p(jnp.exp(-jnp.abs(x)))


def _log_sigmoid(x):
    return jnp.minimum(x, 0.0) - jnp.log1p(jnp.exp(-jnp.abs(x)))


def _split3(x):
    hi = x.astype(BF16)
    r1 = x - hi.astype(F32)
    mid = r1.astype(BF16)
    lo = (r1 - mid.astype(F32)).astype(BF16)
    return jnp.concatenate([hi, mid, lo], axis=1)


def _sum3(y, w):
    return y[:, :w] + y[:, w:2 * w] + y[:, 2 * w:3 * w]


def _cat(ref, base, n):
    if n == 1:
        return ref[base]
    return jnp.concatenate([ref[base + j] for j in range(n)], axis=1)


_NT = (((1,), (1,)), ((), ()))
_TN = (((0,), (0,)), ((), ()))


def _prenorm_rows(x_ref, w_ref, h_scr, tm, rows=64):
    w = w_ref[...]

    def body(i, c):
        r0 = pl.multiple_of(i * rows, rows)
        x = x_ref[pl.ds(r0, rows), :]
        ms = jnp.mean(x * x, axis=1, keepdims=True)
        h_scr[pl.ds(r0, rows), :] = (x * lax.rsqrt(ms + EPS) * w).astype(BF16)
        return c

    lax.fori_loop(0, tm // rows, body, 0)


def _postnorm_rows(o_ref, x_ref, w_ref, tm, rows=64):
    w = w_ref[...]

    def body(i, c):
        r0 = pl.multiple_of(i * rows, rows)
        y = o_ref[pl.ds(r0, rows), :]
        ms = jnp.mean(y * y, axis=1, keepdims=True)
        o_ref[pl.ds(r0, rows), :] = x_ref[pl.ds(r0, rows), :] + y * lax.rsqrt(ms + EPS) * w
        return c

    lax.fori_loop(0, tm // rows, body, 0)


def _resident(shape, index_map):
    return pl.BlockSpec(shape, index_map, pipeline_mode=pl.Buffered(1))


def _inproj_kernel(x_ref, nw_ref, *rest, tm, segs):
    w_refs, o_ref, h_scr = rest[:len(segs)], rest[len(segs)], rest[len(segs) + 1]
    j = pl.program_id(1)

    @pl.when(j == 0)
    def _():
        _prenorm_rows(x_ref, nw_ref, h_scr, tm)

    for w_ref, (start, ntiles, width) in zip(w_refs, segs):
        @pl.when((j >= start) & (j < start + ntiles))
        def _(w_ref=w_ref, width=width):
            r = jnp.dot(h_scr[...], w_ref[0], preferred_element_type=F32)
            for c in range(width // LANE):
                o_ref[c] = r[:, c * LANE:(c + 1) * LANE]
            for c in range(width // LANE, o_ref.shape[0]):
                o_ref[c] = jnp.zeros(o_ref.shape[1:], F32)


def inproj(x, norm_w, segments, *, tm=512, tn=1024):
    n, d = x.shape
    cb = tn // LANE
    segs, specs, start = [], [], 0
    for w, layer, ntiles, width in segments:
        segs.append((start, ntiles, width))
        specs.append(pl.BlockSpec(
            (1, d, width),
            lambda i, j, layer=layer, start=start, ntiles=ntiles: (layer, 0, jnp.clip(j - start, 0, ntiles - 1))))
        start += ntiles
    return pl.pallas_call(
        functools.partial(_inproj_kernel, tm=tm, segs=tuple(segs)),
        grid=(n // tm, start),
        in_specs=[pl.BlockSpec((tm, d), lambda i, j: (i, 0)),
                  pl.BlockSpec((1, d), lambda i, j: (0, 0))] + specs,
        out_specs=pl.BlockSpec((cb, tm, LANE), lambda i, j: (j, i, 0)),
        out_shape=jax.ShapeDtypeStruct((start * cb, n, LANE), F32),
        scratch_shapes=[pltpu.VMEM((tm, d), BF16)],
        compiler_params=_params(2),
        name="inproj",
    )(x, norm_w.reshape(1, d), *[s[0] for s in segments])


def _outproj_kernel(a_ref, w3_ref, x_ref, nw_ref, o_ref, *, tm, kb, nk):
    k = pl.program_id(1)
    w_ref = w3_ref.at[0]
    a = _cat(a_ref, 0, kb).astype(BF16)

    @pl.when(k == 0)
    def _():
        o_ref[...] = jnp.zeros_like(o_ref)

    for n0 in range(0, o_ref.shape[1], DOWN_STRIP):
        o_ref[:, n0:n0 + DOWN_STRIP] += jnp.dot(a, w_ref[:, n0:n0 + DOWN_STRIP],
                                                preferred_element_type=F32)

    @pl.when(k == nk - 1)
    def _():
        _postnorm_rows(o_ref, x_ref, nw_ref, tm)


def outproj(a3, w, layer, x, norm_w, *, tm=512, tk=512):
    n, d = x.shape
    kdim = w.shape[1]
    kb = tk // LANE
    nk = kdim // tk
    return pl.pallas_call(
        functools.partial(_outproj_kernel, tm=tm, kb=kb, nk=nk),
        grid=(n // tm, nk),
        in_specs=[pl.BlockSpec((kb, tm, LANE), lambda i, k: (k, i, 0)),
                  pl.BlockSpec((1, tk, d), lambda i, k: (layer, k, 0)),
                  pl.BlockSpec((tm, d), lambda i, k: (i, 0)),
                  pl.BlockSpec((1, d), lambda i, k: (0, 0))],
        out_specs=pl.BlockSpec((tm, d), lambda i, k: (i, 0)),
        out_shape=jax.ShapeDtypeStruct((n, d), F32),
        compiler_params=_params(2),
        name="outproj",
    )(a3, w, x, norm_w.reshape(1, d))


def _ffn_kernel(x_ref, pre_ref, wg_ref, wu_ref, wd3_ref, post_ref, o_ref, h_scr, a_scr, *, tm, nf):
    f = pl.program_id(1)
    wd_ref = wd3_ref.at[0]

    @pl.when(f == 0)
    def _():
        _prenorm_rows(x_ref, pre_ref, h_scr, tm)
        o_ref[...] = jnp.zeros_like(o_ref)
        a_scr[...] = jnp.zeros_like(a_scr)

    a_prev = a_scr[...]
    h = h_scr[...]
    g = jnp.dot(h, wg_ref[0], preferred_element_type=F32)
    u = jnp.dot(h, wu_ref[0], preferred_element_type=F32)
    for n0 in range(0, o_ref.shape[1], DOWN_STRIP):
        o_ref[:, n0:n0 + DOWN_STRIP] += jnp.dot(a_prev, wd_ref[:, n0:n0 + DOWN_STRIP],
                                                preferred_element_type=F32)
    a_scr[...] = (jax.nn.silu(g) * u).astype(BF16)

    @pl.when(f == nf)
    def _():
        _postnorm_rows(o_ref, x_ref, post_ref, tm)


def ffn(x, pre_w, wg, wu, wd, layer, post_w, *, tm=512, tf=256):
    n, d = x.shape
    dff = wg.shape[2]
    nf = dff // tf
    return pl.pallas_call(
        functools.partial(_ffn_kernel, tm=tm, nf=nf),
        grid=(n // tm, nf + 1),
        in_specs=[pl.BlockSpec((tm, d), lambda i, f: (i, 0)),
                  pl.BlockSpec((1, d), lambda i, f: (0, 0)),
                  pl.BlockSpec((1, d, tf), lambda i, f: (layer, 0, jnp.minimum(f, nf - 1))),
                  pl.BlockSpec((1, d, tf), lambda i, f: (layer, 0, jnp.minimum(f, nf - 1))),
                  pl.BlockSpec((1, tf, d), lambda i, f: (layer, jnp.maximum(f - 1, 0), 0)),
                  pl.BlockSpec((1, d), lambda i, f: (0, 0))],
        out_specs=pl.BlockSpec((tm, d), lambda i, f: (i, 0)),
        out_shape=jax.ShapeDtypeStruct((n, d), F32),
        scratch_shapes=[pltpu.VMEM((tm, d), BF16), pltpu.VMEM((tm, tf), BF16)],
        compiler_params=_params(2),
        name="ffn",
    )(x, pre_w.reshape(1, d), wg, wu, wd, post_w.reshape(1, d))


def _gla_consts(nseq, seq_len):
    r = nseq * seq_len
    t = np.arange(r)[:, None]
    s = np.arange(r)[None, :]
    prefix = ((t // seq_len) == (s // seq_len)) & (s <= t)
    masks = []
    c = seq_len // 2
    while c >= 1:
        right = (t % (2 * c)) >= c
        masks.append(((t // (2 * c)) == (s // (2 * c))) & right & ((s % (2 * c)) < c))
        c //= 2
    m = np.stack(masks).astype(np.float32)
    return jnp.asarray(prefix.astype(np.float32), BF16), jnp.asarray(m, F32)


def _block_row(b, blk, pick, pos):
    r, w = b.shape
    if blk >= 8:
        return jnp.concatenate(
            [jnp.broadcast_to(b[s + pick:s + pick + 1, :], (blk, w)) for s in range(0, r, blk)], axis=0)
    out = b
    for p in range(blk):
        if p != pick:
            out = jnp.where(pos == p, pltpu.roll(b, (p - pick) % r, axis=0), out)
    return out


def _gla_kernel(*refs, mode, nseq, seq_len, hb, dk, dv, nchunks, unroll):
    s_scr, q_scr, k_scr, b_scr, dc_scr, sc_scr = refs[-6:]
    if mode == "hgrn":
        (q_ref, f_ref, v_ref, gt_ref, lb_ref, nw_ref, a_ref, m_ref, s0_ref, _, _,
         o_ref, so_ref) = refs[:-6]
    else:
        (q_ref, k_ref, v_ref, gt_ref, gl_ref, w2_ref, gb_ref, nw_ref, a_ref, m_ref, s0_ref, _, _,
         o_ref, so_ref) = refs[:-6]
    dkb, dvb = dk // LANE, dv // LANE
    r = nseq * seq_len
    nlev = int(math.log2(seq_len))
    c = pl.program_id(2)

    @pl.when(c == 0)
    def _():
        s_scr[...] = s0_ref[0]

    a_mat = a_ref[...]
    seq_shift = int(math.log2(seq_len))
    rowk = lax.broadcasted_iota(jnp.int32, (r, dk), 0)
    colk = lax.broadcasted_iota(jnp.int32, (dk, r), 1)
    seq_of_row = lax.broadcasted_iota(jnp.int32, (r, LANE), 0) >> seq_shift
    lane_id = lax.broadcasted_iota(jnp.int32, (r, LANE), 1)
    seqsel = (seq_of_row == lane_id).astype(BF16)
    nw = _cat(nw_ref, 0, dvb)

    def prepare(h, carry):
        qr = _cat(q_ref, h * dkb, dkb)
        if mode == "hgrn":
            lb = _cat(lb_ref, h * dkb, dkb)
            fg = lb + (1.0 - lb) * jax.nn.sigmoid(_cat(f_ref, h * dkb, dkb))
            g = jnp.log(jnp.maximum(fg, F_MIN))
            k = 1.0 - fg
            q = jax.nn.silu(qr) * (dk ** -0.5)
        else:
            k = _cat(k_ref, h * dkb, dkb)
            q = qr * (dk ** -0.5)
            lin = jnp.dot(gl_ref[0].astype(BF16), w2_ref[h], preferred_element_type=F32)
            g = _log_sigmoid(lin + _cat(gb_ref, h * dkb, dkb)) / GLA_GATE_NORMALIZER

        g3 = _split3(g)
        q_scr[h] = q
        k_scr[h] = k
        b_scr[h] = _sum3(jnp.dot(a_mat, g3, preferred_element_type=F32), dk)
        d3 = lax.dot_general(g3, seqsel, _TN, preferred_element_type=F32)
        dc_scr[h] = d3[0:dk] + d3[dk:2 * dk] + d3[2 * dk:3 * dk]
        return carry

    def intra(h, carry):
        q, k, b = q_scr[h], k_scr[h], b_scr[h]
        scores = jnp.zeros((r, r), F32)
        for l in range(nlev):
            half = seq_len >> (l + 1)
            pos = rowk & (2 * half - 1)
            right = pos >= half
            b_m = _block_row(b, 2 * half, half - 1, pos)
            x = (jnp.where(right, q, k) * jnp.exp(-jnp.abs(b - b_m))).astype(BF16)
            scores = scores + m_ref[l] * lax.dot_general(x, x, _NT, preferred_element_type=F32)
        sc_scr[h] = scores
        return carry

    def combine(h, carry):
        q, k, b = q_scr[h], k_scr[h], b_scr[h]
        v = _cat(v_ref, h * dvb, dvb)
        gate = _cat(gt_ref, h * dvb, dvb)
        vb = v.astype(BF16)
        o = jnp.dot(sc_scr[h].astype(BF16), vb, preferred_element_type=F32)
        o = o + jnp.sum(q * k, axis=1, keepdims=True) * v

        qb = q * jnp.exp(b)
        kb = k * jnp.exp(_block_row(b, seq_len, seq_len - 1, None) - b)
        dcol = dc_scr[h]
        if nseq > 1:
            kb_t = kb.T
        for s in range(nseq):
            st = s_scr[s, h]
            if nseq == 1:
                qs = qb
                upd = lax.dot_general(kb.astype(BF16), vb, _TN, preferred_element_type=F32)
            else:
                qs = jnp.where((rowk >> seq_shift) == s, qb, 0.0)
                ks_t = jnp.where((colk >> seq_shift) == s, kb_t, 0.0)
                upd = jnp.dot(ks_t.astype(BF16), vb, preferred_element_type=F32)
            o = o + jnp.dot(qs.astype(BF16), st.astype(BF16), preferred_element_type=F32)
            dec = jnp.exp(jnp.broadcast_to(dcol[:, s:s + 1], (dk, dv)))
            s_scr[s, h] = dec * st + upd

        ms = jnp.mean(o * o, axis=1, keepdims=True)
        y = o * lax.rsqrt(ms + EPS) * nw * jax.nn.silu(gate)
        for j in range(dvb):
            o_ref[h * dvb + j] = y[:, j * LANE:(j + 1) * LANE]
        return carry

    for stage in (prepare, intra, combine):
        lax.fori_loop(0, hb, stage, 0, unroll=unroll)

    @pl.when(c == nchunks - 1)
    def _():
        so_ref[0] = s_scr[...]


def _alias_or_dummy(arr, ins, specs, aliases, out_idx):
    if arr is None:
        ins.append(jnp.zeros((8, LANE), F32))
    else:
        ins.append(arr)
        aliases[len(ins) - 1] = out_idx
    specs.append(pl.BlockSpec(memory_space=pl.ANY))


def gla_mixer(mode, p3, o3, s0, s0_layer, st_all, layer, n_layers, params, *, row0, nbatch, t_len,
              nseq, hb, heads, dk, dv, offs, out_off, unroll):
    seq_len = min(CHUNK, t_len)
    nchunks = t_len // seq_len
    r = nseq * seq_len
    dkb, dvb = dk // LANE, dv // LANE
    nrb = nbatch // nseq
    rb0 = row0 // r
    a_mat, masks = _gla_consts(nseq, seq_len)
    n_rows = p3.shape[1]

    def slab(nblk, off):
        return pl.BlockSpec((nblk, r, LANE),
                            lambda i, hg, c: (off // nblk + hg, rb0 + i * nchunks + c, 0))

    def const(arr):
        nd = arr.ndim
        return pl.BlockSpec(arr.shape, lambda i, hg, c: (0,) * nd)

    def per_head(arr, nblk):
        return pl.BlockSpec((nblk, 1, LANE), lambda i, hg, c: (hg, 0, 0))

    if mode == "hgrn":
        lb, nw = params
        ins = [p3, p3, p3, p3, lb, nw, a_mat, masks, s0]
        specs = [slab(hb * dkb, offs[0]), slab(hb * dkb, offs[1]), slab(hb * dvb, offs[2]),
                 slab(hb * dvb, offs[3]), per_head(lb, hb * dkb), const(nw), const(a_mat),
                 const(masks)]
    else:
        w2, gb, nw = params
        ins = [p3, p3, p3, p3, p3, w2, gb, nw, a_mat, masks, s0]
        specs = [slab(hb * dkb, offs[0]), slab(hb * dkb, offs[1]), slab(hb * dvb, offs[2]),
                 slab(hb * dvb, offs[3]),
                 pl.BlockSpec((1, r, LANE), lambda i, hg, c: (offs[4], rb0 + i * nchunks + c, 0)),
                 pl.BlockSpec((hb, LANE, dk), lambda i, hg, c: (hg, 0, 0)),
                 per_head(gb, hb * dkb), const(nw), const(a_mat), const(masks)]
    specs.append(pl.BlockSpec((1, nseq, hb, dk, dv), lambda i, hg, c: (s0_layer, i, hg, 0, 0)))
    aliases = {}
    _alias_or_dummy(o3, ins, specs, aliases, 0)
    _alias_or_dummy(st_all, ins, specs, aliases, 1)

    out = pl.pallas_call(
        functools.partial(_gla_kernel, mode=mode, nseq=nseq, seq_len=seq_len, hb=hb, dk=dk, dv=dv,
                          nchunks=nchunks, unroll=unroll),
        grid=(nrb, heads // hb, nchunks),
        in_specs=specs,
        out_specs=[pl.BlockSpec((hb * dvb, r, LANE),
                                lambda i, hg, c: (out_off // (hb * dvb) + hg, rb0 + i * nchunks + c, 0)),
                   pl.BlockSpec((1, nseq, hb, dk, dv), lambda i, hg, c: (layer, i, hg, 0, 0))],
        out_shape=[jax.ShapeDtypeStruct((32, n_rows, LANE), F32),
                   jax.ShapeDtypeStruct((n_layers, nbatch, heads, dk, dv), F32)],
        scratch_shapes=[pltpu.VMEM((nseq, hb, dk, dv), F32),
                        pltpu.VMEM((hb, r, dk), F32),
                        pltpu.VMEM((hb, r, dk), F32),
                        pltpu.VMEM((hb, r, dk), F32),
                        pltpu.VMEM((hb, dk, LANE), F32),
                        pltpu.VMEM((hb, r, r), F32)],
        input_output_aliases=aliases,
        compiler_params=_params(3),
        name=mode + ("_seq" if nchunks > 1 else "_step"),
    )(*ins)
    return out[0], out[1]


def _conv_block(ref, i, cbi, cw_ref, cb_ref, tail, ext, xcs, nseq, seq_len, act):
    w = cw_ref[cbi]
    b = cb_ref[cbi]
    for s in range(nseq):
        ext[0:8, :] = tail[s, cbi]
        ext[8:8 + seq_len, :] = ref[i, s * seq_len:(s + 1) * seq_len, :]
        y = (b + w[3:4] * ext[8:8 + seq_len, :] + w[2:3] * ext[7:7 + seq_len, :]
             + w[1:2] * ext[6:6 + seq_len, :] + w[0:1] * ext[5:5 + seq_len, :])
        xcs[cbi, s * seq_len:(s + 1) * seq_len, :] = act(y)
        tail[s, cbi] = ext[seq_len:seq_len + 8, :]


def _ssd_kernel(z_ref, xa_ref, xb_ref, xc_ref, dt_ref, cw_ref, cb_ref, dtb_ref, alog_ref, dsk_ref,
                nw_ref, t_ref, tt_ref, cbuf0_ref, h0_ref, _o3_any, _h_any, o_ref, hout_ref, cbout_ref,
                h_scr, tail, xcs, ext, *, nseq, seq_len, nchunks):
    r = nseq * seq_len
    c = pl.program_id(1)

    @pl.when(c == 0)
    def _():
        h_scr[...] = h0_ref[0]
        tail[...] = cbuf0_ref[...]

    for part, ref in enumerate((xa_ref, xb_ref, xc_ref)):
        def body(i, carry, part=part, ref=ref):
            _conv_block(ref, i, part * 8 + i, cw_ref, cb_ref, tail, ext, xcs, nseq, seq_len,
                        jax.nn.silu)
            return carry
        lax.fori_loop(0, 8, body, 0)

    lane = lax.broadcasted_iota(jnp.int32, (r, LANE), 1)
    lo = lane < SSM_HEADDIM
    row_lo = lax.broadcasted_iota(jnp.int32, (LANE, LANE), 0) < SSM_HEADDIM
    seq_of_row = lax.broadcasted_iota(jnp.int32, (r, LANE), 0) >> int(math.log2(seq_len))
    tmat = t_ref[...]
    ttmat = tt_ref[...]
    causal = tmat.astype(F32) > 0.0

    def bcast_col(arr, j):
        return jnp.broadcast_to(arr[:, j:j + 1], (r, LANE))

    def group(g, carry):
        dt = _softplus(dt_ref[g] + dtb_ref[g])
        dta = dt * (-jnp.exp(alog_ref[g]))
        d3 = _split3(dta)
        cum = _sum3(jnp.dot(tmat, d3, preferred_element_type=F32), LANE)
        ct3 = lax.dot_general(d3, ttmat, _TN, preferred_element_type=F32)
        cum_t = ct3[0:LANE] + ct3[LANE:2 * LANE] + ct3[2 * LANE:3 * LANE]
        b_g = xcs[16 + g]
        c_g = xcs[20 + g]
        b_b = b_g.astype(BF16)
        cb_m = lax.dot_general(c_g.astype(BF16), b_b, _NT, preferred_element_type=F32)

        def decay_mat(j):
            rel = (jnp.broadcast_to(cum[:, j:j + 1], (r, r))
                   - jnp.broadcast_to(cum_t[j:j + 1, :], (r, r)))
            dec = jnp.where(causal, jnp.exp(jnp.where(causal, rel, 0.0)), 0.0)
            return (cb_m * dec).astype(BF16)

        ys = []
        for jj in range(4):
            j0, j1 = 2 * jj, 2 * jj + 1
            cbi = g * 4 + jj
            x_cb = xcs[cbi]
            dt_e = jnp.where(lo, bcast_col(dt, j0), bcast_col(dt, j1))
            cum_e = jnp.where(lo, bcast_col(cum, j0), bcast_col(cum, j1))
            u = x_cb * dt_e
            u_b = u.astype(BF16)
            y = jnp.where(lo,
                          jnp.dot(decay_mat(j0), u_b, preferred_element_type=F32),
                          jnp.dot(decay_mat(j1), u_b, preferred_element_type=F32))
            y_in = jnp.zeros((r, LANE), F32)
            for s in range(nseq):
                rl = s * seq_len + seq_len - 1
                h_cb = h_scr[s, cbi]
                if nseq == 1:
                    c_s = c_g
                    rel = cum_e[rl:rl + 1, :] - cum_e
                    uw = u * jnp.exp(rel)
                else:
                    in_seq = seq_of_row == s
                    c_s = jnp.where(in_seq, c_g, 0.0)
                    rel = jnp.where(in_seq, cum_e[rl:rl + 1, :] - cum_e, 0.0)
                    uw = jnp.where(in_seq, u * jnp.exp(rel), 0.0)
                y_in = y_in + lax.dot_general(c_s.astype(BF16), h_cb.astype(BF16), _NT,
                                              preferred_element_type=F32)
                last = jnp.where(row_lo,
                                 jnp.broadcast_to(cum[rl:rl + 1, j0:j0 + 1], (LANE, LANE)),
                                 jnp.broadcast_to(cum[rl:rl + 1, j1:j1 + 1], (LANE, LANE)))
                h_scr[s, cbi] = jnp.exp(last) * h_cb + lax.dot_general(
                    uw.astype(BF16), b_b, _TN, preferred_element_type=F32)
            y = y + y_in * jnp.exp(cum_e) + dsk_ref[cbi] * x_cb
            ys.append(y * jax.nn.silu(z_ref[cbi]))
        ms = sum(jnp.sum(y * y, axis=1, keepdims=True) for y in ys) * (1.0 / (4 * LANE))
        rinv = lax.rsqrt(ms + EPS)
        for jj in range(4):
            o_ref[g * 4 + jj] = ys[jj] * rinv * nw_ref[g * 4 + jj]
        return carry

    lax.fori_loop(0, SSM_GROUPS, group, 0, unroll=True)

    @pl.when(c == nchunks - 1)
    def _():
        hout_ref[0] = h_scr[...]
        cbout_ref[...] = tail[...]


def _tri_consts(nseq, seq_len):
    r = nseq * seq_len
    t = np.arange(r)[:, None]
    s = np.arange(r)[None, :]
    m = (((t // seq_len) == (s // seq_len)) & (s <= t)).astype(np.float32)
    return jnp.asarray(m, BF16), jnp.asarray(m.T, BF16)


def ssd_mixer(p3, o3, h0, h0_layer, h_all, layer, n_layers, cbuf0, params, *, row0, nbatch, t_len,
              nseq):
    seq_len = min(CHUNK, t_len)
    nchunks = t_len // seq_len
    r = nseq * seq_len
    nrb = nbatch // nseq
    rb0 = row0 // r
    cw, cbias, dtb, alog, dsk, nw = params
    tmat, ttmat = _tri_consts(nseq, seq_len)
    n_rows = p3.shape[1]

    def slab(nblk, blk_idx):
        return pl.BlockSpec((nblk, r, LANE), lambda i, c: (blk_idx, rb0 + i * nchunks + c, 0))

    def const(arr):
        nd = arr.ndim
        return pl.BlockSpec(arr.shape, lambda i, c: (0,) * nd)

    ins = [p3, p3, p3, p3, p3, cw, cbias, dtb, alog, dsk, nw, tmat, ttmat, cbuf0, h0]
    specs = [slab(16, 4), slab(8, 10), slab(8, 11), slab(8, 12), slab(4, 26),
             const(cw), const(cbias), const(dtb), const(alog), const(dsk), const(nw),
             const(tmat), const(ttmat),
             pl.BlockSpec((nseq, 24, 8, LANE), lambda i, c: (i, 0, 0, 0)),
             pl.BlockSpec((1, nseq, 16, LANE, LANE), lambda i, c: (h0_layer, i, 0, 0, 0))]
    aliases = {}
    _alias_or_dummy(o3, ins, specs, aliases, 0)
    _alias_or_dummy(h_all, ins, specs, aliases, 1)
    out = pl.pallas_call(
        functools.partial(_ssd_kernel, nseq=nseq, seq_len=seq_len, nchunks=nchunks),
        grid=(nrb, nchunks),
        in_specs=specs,
        out_specs=[pl.BlockSpec((16, r, LANE), lambda i, c: (1, rb0 + i * nchunks + c, 0)),
                   pl.BlockSpec((1, nseq, 16, LANE, LANE), lambda i, c: (layer, i, 0, 0, 0)),
                   pl.BlockSpec((nseq, 24, 8, LANE), lambda i, c: (i, 0, 0, 0))],
        out_shape=[jax.ShapeDtypeStruct((32, n_rows, LANE), F32),
                   jax.ShapeDtypeStruct((n_layers, nbatch, 16, LANE, LANE), F32),
                   jax.ShapeDtypeStruct((nbatch, 24, 8, LANE), F32)],
        scratch_shapes=[pltpu.VMEM((nseq, 16, LANE, LANE), F32),
                        pltpu.VMEM((nseq, 24, 8, LANE), F32),
                        pltpu.VMEM((24, r, LANE), F32),
                        pltpu.VMEM((seq_len + 8, LANE), F32)],
        input_output_aliases=aliases,
        compiler_params=_params(2),
        name="ssd_seq" if nchunks > 1 else "ssd_step",
    )(*ins)
    return out[0], out[1], out[2]


def _lru_kernel(x_ref, y_ref, cw_ref, cb_ref, wa_ref, ba_ref, wx_ref, bx_ref, lam_ref, cbuf0_ref,
                h0_ref, _, o_ref, hout_ref, cbout_ref, h_scr, tail, xcs, ext,
                *, nseq, seq_len, nchunks):
    r = nseq * seq_len
    sub = min(seq_len, CHUNK)
    c = pl.program_id(2)

    @pl.when(c == 0)
    def _():
        h_scr[...] = h0_ref[0]
        tail[...] = cbuf0_ref[...]

    _conv_block(x_ref, 0, 0, cw_ref, cb_ref, tail, ext, xcs, nseq, seq_len, lambda v: v)
    xc = xcs[0]
    xc_b = xc.astype(BF16)
    rg = jax.nn.sigmoid(jnp.dot(xc_b, wa_ref[0], preferred_element_type=F32) + ba_ref[0])
    ig = jax.nn.sigmoid(jnp.dot(xc_b, wx_ref[0], preferred_element_type=F32) + bx_ref[0])
    log_a = -LRU_C * rg * _softplus(-lam_ref[0])
    a_cum = jnp.exp(log_a)
    th = jnp.tanh(log_a)
    u_cum = jnp.sqrt(-2.0 * th / (1.0 - th)) * (ig * xc)

    pos = lax.broadcasted_iota(jnp.int32, (r, LANE), 0) & (sub - 1)
    d = 1
    while d < sub:
        take = pos >= d
        a_prev = pltpu.roll(a_cum, d, axis=0)
        u_prev = pltpu.roll(u_cum, d, axis=0)
        u_cum = jnp.where(take, a_cum * u_prev + u_cum, u_cum)
        a_cum = jnp.where(take, a_cum * a_prev, a_cum)
        d *= 2

    gel = jax.nn.gelu(y_ref[0])
    for s in range(nseq):
        h_prev = h_scr[s]
        for sb in range(seq_len // sub):
            r0 = s * seq_len + sb * sub
            hs = a_cum[r0:r0 + sub] * h_prev + u_cum[r0:r0 + sub]
            o_ref[0, r0:r0 + sub, :] = hs * gel[r0:r0 + sub]
            h_prev = hs[sub - 1:sub]
        h_scr[s] = h_prev

    @pl.when(c == nchunks - 1)
    def _():
        hout_ref[0] = h_scr[...]
        cbout_ref[...] = tail[...]


def lru_mixer(p3, o3, h0, cbuf0, params, *, row0, nbatch, t_len, nseq, rows):
    seq_len = min(rows, t_len)
    nchunks = t_len // seq_len
    r = nseq * seq_len
    nrb = nbatch // nseq
    rb0 = row0 // r
    cw, cbias, wa, ba, wx, bx, lam = params
    n_rows = p3.shape[1]

    def slab(off):
        return pl.BlockSpec((1, r, LANE), lambda i, n, c: (off + n, rb0 + i * nchunks + c, 0))

    def per_blk(arr):
        shp = (1,) + arr.shape[1:]
        nd = arr.ndim
        return pl.BlockSpec(shp, lambda i, n, c: (n,) + (0,) * (nd - 1))

    ins = [p3, p3, cw, cbias, wa, ba, wx, bx, lam, cbuf0, h0, o3]
    specs = [slab(48), slab(64), per_blk(cw), per_blk(cbias), per_blk(wa), per_blk(ba),
             per_blk(wx), per_blk(bx), per_blk(lam),
             pl.BlockSpec((nseq, 1, 8, LANE), lambda i, n, c: (i, n, 0, 0)),
             pl.BlockSpec((1, nseq, 1, LANE), lambda i, n, c: (n, i, 0, 0)),
             pl.BlockSpec(memory_space=pl.ANY)]
    out = pl.pallas_call(
        functools.partial(_lru_kernel, nseq=nseq, seq_len=seq_len, nchunks=nchunks),
        grid=(nrb, 16, nchunks),
        in_specs=specs,
        out_specs=[pl.BlockSpec((1, r, LANE), lambda i, n, c: (16 + n, rb0 + i * nchunks + c, 0)),
                   pl.BlockSpec((1, nseq, 1, LANE), lambda i, n, c: (n, i, 0, 0)),
                   pl.BlockSpec((nseq, 1, 8, LANE), lambda i, n, c: (i, n, 0, 0))],
        out_shape=[jax.ShapeDtypeStruct((32, n_rows, LANE), F32),
                   jax.ShapeDtypeStruct((16, nbatch, 1, LANE), F32),
                   jax.ShapeDtypeStruct((nbatch, 16, 8, LANE), F32)],
        scratch_shapes=[pltpu.VMEM((nseq, 1, LANE), F32),
                        pltpu.VMEM((nseq, 1, 8, LANE), F32),
                        pltpu.VMEM((1, r, LANE), F32),
                        pltpu.VMEM((seq_len + 8, LANE), F32)],
        input_output_aliases={len(ins) - 1: 0},
        compiler_params=_params(3),
        name="lru_seq" if nchunks > 1 else "lru_step",
    )(*ins)
    return out[0], out[1], out[2]


def _slabs(v, nblk):
    return v.astype(F32).reshape(nblk, 1, LANE)


def _pad_lanes(a, width=LANE):
    return jnp.pad(a, [(0, 0)] * (a.ndim - 1) + [(0, width - a.shape[-1])])


def _conv_state_in(buf, nblk):
    b = buf.shape[0]
    t = jnp.transpose(buf.astype(F32).reshape(b, 3, nblk, LANE), (0, 2, 1, 3))
    return jnp.pad(t, ((0, 0), (0, 0), (5, 0), (0, 0)))


def _conv_state_out(t):
    b, nblk = t.shape[:2]
    return jnp.transpose(t[:, :, 5:8, :], (0, 2, 1, 3)).reshape(b, 3, nblk * LANE)


EVEN_MAIN = 13312
ODD_GLOW = (6144, 6160)
IN_TN = 1024


def _prep_w_in_even(w):
    nl, d, _ = w.shape
    dt = _pad_lanes(w[:, :, EVEN_MAIN:].reshape(nl, d, SSM_GROUPS, SSM_HPG))
    return w.astype(BF16), dt.reshape(nl, d, SSM_GROUPS * LANE).astype(BF16)


def _prep_w_in_odd(w):
    g0, g1 = ODD_GLOW
    main = jnp.concatenate([w[:, :, :g0], w[:, :, g1:]], axis=2).astype(BF16)
    return main, _pad_lanes(w[:, :, g0:g1]).astype(BF16)


def kernel(x_prompt, x_sample, state_hgrn, state_ssm, state_ssm_conv, state_gla, state_lru,
           state_lru_conv, norm_mix_pre, norm_mix_post, norm_ffn_pre, norm_ffn_post,
           w_in_even, w_out_even, hgrn_lb_logits, hgrn_norm_w,
           ssm_conv_w, ssm_conv_b, ssm_dt_bias, ssm_a_log, ssm_d, ssm_norm_w,
           w_in_odd, w_out_odd, gla_gate_w2, gla_gate_b, gla_norm_w,
           lru_conv_w, lru_conv_b, lru_wa, lru_ba, lru_wx, lru_bx, lru_lambda,
           ffn_w_gate, ffn_w_up, ffn_w_down):
    bp, tp, d = x_prompt.shape
    bs, ts, _ = x_sample.shape
    n_p, n_s = bp * tp, bs * ts
    depth = norm_mix_pre.shape[0]
    x = jnp.concatenate([x_prompt.reshape(n_p, d), x_sample.reshape(n_s, d)], axis=0)

    lb_soft = jax.nn.softmax(hgrn_lb_logits.astype(F32), axis=0)
    hgrn_lb = jnp.maximum(jnp.cumsum(lb_soft, axis=0) - lb_soft[0], 0.0)

    paths = (dict(row0=0, nbatch=bp, t_len=tp), dict(row0=n_p, nbatch=bs, t_len=ts))
    outs = {k: ([], []) for k in ("ssm_cb", "lru", "lru_cb")}
    big = {k: [None, None] for k in ("hg", "ssm", "gla")}
    n_even, n_odd = w_in_even.shape[0], w_in_odd.shape[0]

    we_main, we_dt = _prep_w_in_even(w_in_even)
    wo_main, wo_glow = _prep_w_in_odd(w_in_odd)
    w_out_e, w_out_o = w_out_even.astype(BF16), w_out_odd.astype(BF16)
    wg, wu, wd = ffn_w_gate.astype(BF16), ffn_w_up.astype(BF16), ffn_w_down.astype(BF16)
    st_hgrn = state_hgrn.astype(F32)
    st_ssm = state_ssm.astype(F32).reshape(n_even, bs, 16, LANE, LANE)
    st_gla = state_gla.astype(F32)
    zero_hg = jnp.zeros((1, bp) + state_hgrn.shape[2:], F32)
    zero_ssm = jnp.zeros((1, bp, 16, LANE, LANE), F32)
    zero_gla = jnp.zeros((1, bp) + state_gla.shape[2:], F32)

    for l in range(depth):
        j = l // 2
        if l % 2 == 0:
            p3 = inproj(x, norm_mix_pre[l],
                        [(we_main, j, EVEN_MAIN // IN_TN, IN_TN), (we_dt, j, 1, SSM_GROUPS * LANE)],
                        tn=IN_TN)
            hg_params = (_slabs(hgrn_lb[j], 16), _slabs(hgrn_norm_w[j], 1))
            dsk = jnp.repeat(ssm_d[j].astype(F32), SSM_HEADDIM)
            ssd_params = (
                jnp.transpose(ssm_conv_w[j].astype(F32).reshape(4, 24, LANE), (1, 0, 2)),
                _slabs(ssm_conv_b[j], 24),
                _pad_lanes(ssm_dt_bias[j].astype(F32).reshape(SSM_GROUPS, 1, SSM_HPG)),
                _pad_lanes(ssm_a_log[j].astype(F32).reshape(SSM_GROUPS, 1, SSM_HPG)),
                _slabs(dsk, 16), _slabs(ssm_norm_w[j], 16))
            o3 = None
            for pi, path in enumerate(paths):
                nb = path["nbatch"]
                if pi == 0:
                    s_hg, s_ssm, s_layer = zero_hg, zero_ssm, 0
                    s_cb = jnp.zeros((nb, 24, 8, LANE), F32)
                    nseq_a, hb_a, unroll_a, nseq_b = 1, 16, 8, 1
                else:
                    s_hg, s_ssm, s_layer = st_hgrn, st_ssm, j
                    s_cb = _conv_state_in(state_ssm_conv[j], 24)
                    nseq_a, hb_a, unroll_a, nseq_b = 8, 8, 4, 4
                o3, big["hg"][pi] = gla_mixer(
                    "hgrn", p3, o3, s_hg, s_layer, big["hg"][pi], j, n_even, hg_params,
                    nseq=nseq_a, hb=hb_a, heads=16, dk=128, dv=128, offs=(0, 16, 32, 48), out_off=0,
                    unroll=unroll_a, **path)
                o3, big["ssm"][pi], cb_b = ssd_mixer(p3, o3, s_ssm, s_layer, big["ssm"][pi], j,
                                                     n_even, s_cb, ssd_params, nseq=nseq_b, **path)
                outs["ssm_cb"][pi].append(_conv_state_out(cb_b).astype(state_ssm_conv.dtype))
            x = outproj(o3, w_out_e, j, x, norm_mix_post[l])
        else:
            p3 = inproj(x, norm_mix_pre[l],
                        [(wo_main, j, wo_main.shape[2] // IN_TN, IN_TN), (wo_glow, j, 1, LANE)],
                        tn=IN_TN)
            w2 = jnp.pad(gla_gate_w2[j], ((0, LANE - gla_gate_w2.shape[1]), (0, 0)))
            w2 = jnp.transpose(w2.reshape(LANE, 4, 256), (1, 0, 2)).astype(BF16)
            gla_params = (w2, _slabs(gla_gate_b[j], 8), _slabs(gla_norm_w[j], 4))
            lru_params = (
                jnp.transpose(lru_conv_w[j].astype(F32).reshape(4, 16, LANE), (1, 0, 2)),
                _slabs(lru_conv_b[j], 16), lru_wa[j].astype(BF16), _slabs(lru_ba[j], 16),
                lru_wx[j].astype(BF16), _slabs(lru_bx[j], 16), _slabs(lru_lambda[j], 16))
            o3 = None
            for pi, path in enumerate(paths):
                nb = path["nbatch"]
                if pi == 0:
                    s_gla, s_layer = zero_gla, 0
                    s_lru = jnp.zeros((16, nb, 1, LANE), F32)
                    s_cb = jnp.zeros((nb, 16, 8, LANE), F32)
                    nseq_c, hb_c, nseq_d, rows_d = 1, 4, 1, 512
                else:
                    s_gla, s_layer = st_gla, j
                    s_lru = jnp.transpose(state_lru[j].astype(F32).reshape(nb, 16, 1, LANE),
                                          (1, 0, 2, 3))
                    s_cb = _conv_state_in(state_lru_conv[j], 16)
                    nseq_c, hb_c, nseq_d, rows_d = 8, 1, 16, 8
                o3, big["gla"][pi] = gla_mixer(
                    "gla", p3, o3, s_gla, s_layer, big["gla"][pi], j, n_odd, gla_params,
                    nseq=nseq_c, hb=hb_c, heads=4, dk=256, dv=512, offs=(0, 8, 16, 32, 80),
                    out_off=0, unroll=min(4, hb_c), **path)
                o3, st_d, cb_d = lru_mixer(p3, o3, s_lru, s_cb, lru_params, nseq=nseq_d,
                                           rows=rows_d, **path)
                outs["lru"][pi].append(jnp.transpose(st_d, (1, 0, 2, 3)).reshape(nb, 16 * LANE)
                                       .astype(state_lru.dtype))
                outs["lru_cb"][pi].append(_conv_state_out(cb_d).astype(state_lru_conv.dtype))
            x = outproj(o3, w_out_o, j, x, norm_mix_post[l])
        x = ffn(x, norm_ffn_pre[l], wg, wu, wd, l, norm_ffn_post[l])

    y_prompt = x[:n_p].reshape(bp, tp, d).astype(x_prompt.dtype)
    y_sample = x[n_p:].reshape(bs, ts, d).astype(x_sample.dtype)
    ssm_shape = state_ssm.shape[2:]
    res = [y_prompt, y_sample]
    stacked = lambda key, pi: jnp.stack(outs[key][pi])
    res += [big["hg"][0].astype(state_hgrn.dtype), big["hg"][1].astype(state_hgrn.dtype),
            big["ssm"][0].reshape((n_even, bp) + ssm_shape).astype(state_ssm.dtype),
            big["ssm"][1].reshape((n_even, bs) + ssm_shape).astype(state_ssm.dtype),
            stacked("ssm_cb", 0), stacked("ssm_cb", 1),
            big["gla"][0].astype(state_gla.dtype), big["gla"][1].astype(state_gla.dtype),
            stacked("lru", 0), stacked("lru", 1), stacked("lru_cb", 0), stacked("lru_cb", 1)]
    return tuple(res)
```

```python
import functools
import math

import numpy as np
import jax
import jax.numpy as jnp
from jax import lax
from jax.experimental import pallas as pl
from jax.experimental.pallas import tpu as pltpu

F32 = jnp.float32
BF16 = jnp.bfloat16
LANE = 128
VMEM_LIMIT = 56 * 1024 * 1024

DOWN_STRIP = 512

EPS = 1e-6
F_MIN = 1e-30
CHUNK = 64
GLA_GATE_NORMALIZER = 16.0
LRU_C = 8.0
SSM_GROUPS = 4
SSM_HPG = 8
SSM_HEADDIM = 64


def _params(n_axes):
    return pltpu.CompilerParams(dimension_semantics=("arbitrary",) * n_axes,
                                vmem_limit_bytes=VMEM_LIMIT)


def _softplus(x):
    return jnp.maximum(x, 0.0) + jnp.log1p(jnp.exp(-jnp.abs(x)))


def _log_sigmoid(x):
    return jnp.minimum(x, 0.0) - jnp.log1p(jnp.exp(-jnp.abs(x)))


def _split3(x):
    hi = x.astype(BF16)
    r1 = x - hi.astype(F32)
    mid = r1.astype(BF16)
    lo = (r1 - mid.astype(F32)).astype(BF16)
    return jnp.concatenate([hi, mid, lo], axis=1)


def _sum3(y, w):
    return y[:, :w] + y[:, w:2 * w] + y[:, 2 * w:3 * w]


def _cat(ref, base, n):
    if n == 1:
        return ref[base]
    return jnp.concatenate([ref[base + j] for j in range(n)], axis=1)


_NT = (((1,), (1,)), ((), ()))
_TN = (((0,), (0,)), ((), ()))


def _prenorm_rows(x_ref, w_ref, h_scr, tm, rows=64):
    w = w_ref[...]

    def body(i, c):
        r0 = pl.multiple_of(i * rows, rows)
        x = x_ref[pl.ds(r0, rows), :]
        ms = jnp.mean(x * x, axis=1, keepdims=True)
        h_scr[pl.ds(r0, rows), :] = (x * lax.rsqrt(ms + EPS) * w).astype(BF16)
        return c

    lax.fori_loop(0, tm // rows, body, 0)


def _postnorm_rows(o_ref, x_ref, w_ref, tm, rows=64):
    w = w_ref[...]

    def body(i, c):
        r0 = pl.multiple_of(i * rows, rows)
        y = o_ref[pl.ds(r0, rows), :]
        ms = jnp.mean(y * y, axis=1, keepdims=True)
        o_ref[pl.ds(r0, rows), :] = x_ref[pl.ds(r0, rows), :] + y * lax.rsqrt(ms + EPS) * w
        return c

    lax.fori_loop(0, tm // rows, body, 0)


def _resident(shape, index_map):
    return pl.BlockSpec(shape, index_map, pipeline_mode=pl.Buffered(1))


def _inproj_kernel(x_ref, nw_ref, *rest, tm, segs):
    w_refs, o_ref, h_scr = rest[:len(segs)], rest[len(segs)], rest[len(segs) + 1]
    j = pl.program_id(1)

    @pl.when(j == 0)
    def _():
        _prenorm_rows(x_ref, nw_ref, h_scr, tm)

    for w_ref, (start, ntiles, width) in zip(w_refs, segs):
        @pl.when((j >= start) & (j < start + ntiles))
        def _(w_ref=w_ref, width=width):
            r = jnp.dot(h_scr[...], w_ref[0], preferred_element_type=F32)
            for c in range(width // LANE):
                o_ref[c] = r[:, c * LANE:(c + 1) * LANE]
            for c in range(width // LANE, o_ref.shape[0]):
                o_ref[c] = jnp.zeros(o_ref.shape[1:], F32)


def inproj(x, norm_w, segments, *, tm=512, tn=1024):
    n, d = x.shape
    cb = tn // LANE
    segs, specs, start = [], [], 0
    for w, layer, ntiles, width in segments:
        segs.append((start, ntiles, width))
        specs.append(pl.BlockSpec(
            (1, d, width),
            lambda i, j, layer=layer, start=start, ntiles=ntiles: (layer, 0, jnp.clip(j - start, 0, ntiles - 1))))
        start += ntiles
    return pl.pallas_call(
        functools.partial(_inproj_kernel, tm=tm, segs=tuple(segs)),
        grid=(n // tm, start),
        in_specs=[pl.BlockSpec((tm, d), lambda i, j: (i, 0)),
                  pl.BlockSpec((1, d), lambda i, j: (0, 0))] + specs,
        out_specs=pl.BlockSpec((cb, tm, LANE), lambda i, j: (j, i, 0)),
        out_shape=jax.ShapeDtypeStruct((start * cb, n, LANE), F32),
        scratch_shapes=[pltpu.VMEM((tm, d), BF16)],
        compiler_params=_params(2),
        name="inproj",
    )(x, norm_w.reshape(1, d), *[s[0] for s in segments])


def _outproj_kernel(a_ref, w3_ref, x_ref, nw_ref, o_ref, *, tm, kb, nk):
    k = pl.program_id(1)
    w_ref = w3_ref.at[0]
    a = _cat(a_ref, 0, kb).astype(BF16)

    @pl.when(k == 0)
    def _():
        o_ref[...] = jnp.zeros_like(o_ref)

    for n0 in range(0, o_ref.shape[1], DOWN_STRIP):
        o_ref[:, n0:n0 + DOWN_STRIP] += jnp.dot(a, w_ref[:, n0:n0 + DOWN_STRIP],
                                                preferred_element_type=F32)

    @pl.when(k == nk - 1)
    def _():
        _postnorm_rows(o_ref, x_ref, nw_ref, tm)


def outproj(a3, w, layer, x, norm_w, *, tm=512, tk=1024):
    n, d = x.shape
    kdim = w.shape[1]
    kb = tk // LANE
    nk = kdim // tk
    return pl.pallas_call(
        functools.partial(_outproj_kernel, tm=tm, kb=kb, nk=nk),
        grid=(n // tm, nk),
        in_specs=[pl.BlockSpec((kb, tm, LANE), lambda i, k: (k, i, 0)),
                  pl.BlockSpec((1, tk, d), lambda i, k: (layer, k, 0)),
                  pl.BlockSpec((tm, d), lambda i, k: (i, 0)),
                  pl.BlockSpec((1, d), lambda i, k: (0, 0))],
        out_specs=pl.BlockSpec((tm, d), lambda i, k: (i, 0)),
        out_shape=jax.ShapeDtypeStruct((n, d), F32),
        compiler_params=_params(2),
        name="outproj",
    )(a3, w, x, norm_w.reshape(1, d))


def _ffn_kernel(x_ref, pre_ref, wg_ref, wu_ref, wd3_ref, post_ref, o_ref, h_scr, a_scr, *, tm, nf):
    f = pl.program_id(1)
    wd_ref = wd3_ref.at[0]

    @pl.when(f == 0)
    def _():
        _prenorm_rows(x_ref, pre_ref, h_scr, tm)
        o_ref[...] = jnp.zeros_like(o_ref)
        a_scr[...] = jnp.zeros_like(a_scr)

    a_prev = a_scr[...]
    h = h_scr[...]
    g = jnp.dot(h, wg_ref[0], preferred_element_type=F32)
    u = jnp.dot(h, wu_ref[0], preferred_element_type=F32)
    for n0 in range(0, o_ref.shape[1], DOWN_STRIP):
        o_ref[:, n0:n0 + DOWN_STRIP] += jnp.dot(a_prev, wd_ref[:, n0:n0 + DOWN_STRIP],
                                                preferred_element_type=F32)
    a_scr[...] = (jax.nn.silu(g) * u).astype(BF16)

    @pl.when(f == nf)
    def _():
        _postnorm_rows(o_ref, x_ref, post_ref, tm)


def ffn(x, pre_w, wg, wu, wd, layer, post_w, *, tm=512, tf=256):
    n, d = x.shape
    dff = wg.shape[2]
    nf = dff // tf
    return pl.pallas_call(
        functools.partial(_ffn_kernel, tm=tm, nf=nf),
        grid=(n // tm, nf + 1),
        in_specs=[pl.BlockSpec((tm, d), lambda i, f: (i, 0)),
                  pl.BlockSpec((1, d), lambda i, f: (0, 0)),
                  pl.BlockSpec((1, d, tf), lambda i, f: (layer, 0, jnp.minimum(f, nf - 1))),
                  pl.BlockSpec((1, d, tf), lambda i, f: (layer, 0, jnp.minimum(f, nf - 1))),
                  pl.BlockSpec((1, tf, d), lambda i, f: (layer, jnp.maximum(f - 1, 0), 0)),
                  pl.BlockSpec((1, d), lambda i, f: (0, 0))],
        out_specs=pl.BlockSpec((tm, d), lambda i, f: (i, 0)),
        out_shape=jax.ShapeDtypeStruct((n, d), F32),
        scratch_shapes=[pltpu.VMEM((tm, d), BF16), pltpu.VMEM((tm, tf), BF16)],
        compiler_params=_params(2),
        name="ffn",
    )(x, pre_w.reshape(1, d), wg, wu, wd, post_w.reshape(1, d))


def _gla_consts(nseq, seq_len):
    r = nseq * seq_len
    t = np.arange(r)[:, None]
    s = np.arange(r)[None, :]
    prefix = ((t // seq_len) == (s // seq_len)) & (s <= t)
    masks = []
    c = seq_len // 2
    while c >= 1:
        right = (t % (2 * c)) >= c
        masks.append(((t // (2 * c)) == (s // (2 * c))) & right & ((s % (2 * c)) < c))
        c //= 2
    m = np.stack(masks).astype(np.float32)
    return jnp.asarray(prefix.astype(np.float32), BF16), jnp.asarray(m, F32)


def _block_row(b, blk, pick, pos):
    r, w = b.shape
    if blk >= 8:
        return jnp.concatenate(
            [jnp.broadcast_to(b[s + pick:s + pick + 1, :], (blk, w)) for s in range(0, r, blk)], axis=0)
    out = b
    for p in range(blk):
        if p != pick:
            out = jnp.where(pos == p, pltpu.roll(b, (p - pick) % r, axis=0), out)
    return out


def _gla_kernel(*refs, mode, nseq, seq_len, hb, dk, dv, nchunks, unroll):
    s_scr, q_scr, k_scr, b_scr, dc_scr, sc_scr = refs[-6:]
    if mode == "hgrn":
        (q_ref, f_ref, v_ref, gt_ref, lb_ref, nw_ref, a_ref, m_ref, s0_ref, _, _,
         o_ref, so_ref) = refs[:-6]
    else:
        (q_ref, k_ref, v_ref, gt_ref, gl_ref, w2_ref, gb_ref, nw_ref, a_ref, m_ref, s0_ref, _, _,
         o_ref, so_ref) = refs[:-6]
    dkb, dvb = dk // LANE, dv // LANE
    r = nseq * seq_len
    nlev = int(math.log2(seq_len))
    c = pl.program_id(2)

    @pl.when(c == 0)
    def _():
        s_scr[...] = s0_ref[0]

    a_mat = a_ref[...]
    seq_shift = int(math.log2(seq_len))
    rowk = lax.broadcasted_iota(jnp.int32, (r, dk), 0)
    colk = lax.broadcasted_iota(jnp.int32, (dk, r), 1)
    seq_of_row = lax.broadcasted_iota(jnp.int32, (r, LANE), 0) >> seq_shift
    lane_id = lax.broadcasted_iota(jnp.int32, (r, LANE), 1)
    seqsel = (seq_of_row == lane_id).astype(BF16)
    nw = _cat(nw_ref, 0, dvb)

    def prepare(h, carry):
        qr = _cat(q_ref, h * dkb, dkb)
        if mode == "hgrn":
            lb = _cat(lb_ref, h * dkb, dkb)
            fg = lb + (1.0 - lb) * jax.nn.sigmoid(_cat(f_ref, h * dkb, dkb))
            g = jnp.log(jnp.maximum(fg, F_MIN))
            k = 1.0 - fg
            q = jax.nn.silu(qr) * (dk ** -0.5)
        else:
            k = _cat(k_ref, h * dkb, dkb)
            q = qr * (dk ** -0.5)
            lin = jnp.dot(gl_ref[0].astype(BF16), w2_ref[h], preferred_element_type=F32)
            g = _log_sigmoid(lin + _cat(gb_ref, h * dkb, dkb)) / GLA_GATE_NORMALIZER

        g3 = _split3(g)
        q_scr[h] = q
        k_scr[h] = k
        b_scr[h] = _sum3(jnp.dot(a_mat, g3, preferred_element_type=F32), dk)
        d3 = lax.dot_general(g3, seqsel, _TN, preferred_element_type=F32)
        dc_scr[h] = d3[0:dk] + d3[dk:2 * dk] + d3[2 * dk:3 * dk]
        return carry

    def intra(h, carry):
        q, k, b = q_scr[h], k_scr[h], b_scr[h]
        scores = jnp.zeros((r, r), F32)
        for l in range(nlev):
            half = seq_len >> (l + 1)
            pos = rowk & (2 * half - 1)
            right = pos >= half
            b_m = _block_row(b, 2 * half, half - 1, pos)
            x = (jnp.where(right, q, k) * jnp.exp(-jnp.abs(b - b_m))).astype(BF16)
            scores = scores + m_ref[l] * lax.dot_general(x, x, _NT, preferred_element_type=F32)
        sc_scr[h] = scores
        return carry

    def combine(h, carry):
        q, k, b = q_scr[h], k_scr[h], b_scr[h]
        v = _cat(v_ref, h * dvb, dvb)
        gate = _cat(gt_ref, h * dvb, dvb)
        vb = v.astype(BF16)
        o = jnp.dot(sc_scr[h].astype(BF16), vb, preferred_element_type=F32)
        o = o + jnp.sum(q * k, axis=1, keepdims=True) * v

        qb = q * jnp.exp(b)
        kb = k * jnp.exp(_block_row(b, seq_len, seq_len - 1, None) - b)
        dcol = dc_scr[h]
        if nseq > 1:
            kb_t = kb.T
        for s in range(nseq):
            st = s_scr[s, h]
            if nseq == 1:
                qs = qb
                upd = lax.dot_general(kb.astype(BF16), vb, _TN, preferred_element_type=F32)
            else:
                qs = jnp.where((rowk >> seq_shift) == s, qb, 0.0)
                ks_t = jnp.where((colk >> seq_shift) == s, kb_t, 0.0)
                upd = jnp.dot(ks_t.astype(BF16), vb, preferred_element_type=F32)
            o = o + jnp.dot(qs.astype(BF16), st.astype(BF16), preferred_element_type=F32)
            dec = jnp.exp(jnp.broadcast_to(dcol[:, s:s + 1], (dk, dv)))
            s_scr[s, h] = dec * st + upd

        ms = jnp.mean(o * o, axis=1, keepdims=True)
        y = o * lax.rsqrt(ms + EPS) * nw * jax.nn.silu(gate)
        for j in range(dvb):
            o_ref[h * dvb + j] = y[:, j * LANE:(j + 1) * LANE]
        return carry

    for stage in (prepare, intra, combine):
        lax.fori_loop(0, hb, stage, 0, unroll=unroll)

    @pl.when(c == nchunks - 1)
    def _():
        so_ref[0] = s_scr[...]
        for other in range(1, so_ref.shape[0]):
            so_ref[other] = jnp.zeros(so_ref.shape[1:], F32)


def _alias_or_dummy(arr, ins, specs, aliases, out_idx):
    if arr is None:
        ins.append(jnp.zeros((8, LANE), F32))
    else:
        ins.append(arr)
        aliases[len(ins) - 1] = out_idx
    specs.append(pl.BlockSpec(memory_space=pl.ANY))


def gla_mixer(mode, p3, o3, s0, s0_layer, st_all, layer, n_layers, params, *, row0, nbatch, t_len,
              nseq, hb, heads, dk, dv, offs, out_off, unroll):
    seq_len = min(CHUNK, t_len)
    nchunks = t_len // seq_len
    r = nseq * seq_len
    dkb, dvb = dk // LANE, dv // LANE
    nrb = nbatch // nseq
    rb0 = row0 // r
    a_mat, masks = _gla_consts(nseq, seq_len)
    n_rows = p3.shape[1]

    def slab(nblk, off):
        return pl.BlockSpec((nblk, r, LANE),
                            lambda i, hg, c: (off // nblk + hg, rb0 + i * nchunks + c, 0))

    def const(arr):
        nd = arr.ndim
        return pl.BlockSpec(arr.shape, lambda i, hg, c: (0,) * nd)

    def per_head(arr, nblk):
        return pl.BlockSpec((nblk, 1, LANE), lambda i, hg, c: (hg, 0, 0))

    if mode == "hgrn":
        lb, nw = params
        ins = [p3, p3, p3, p3, lb, nw, a_mat, masks, s0]
        specs = [slab(hb * dkb, offs[0]), slab(hb * dkb, offs[1]), slab(hb * dvb, offs[2]),
                 slab(hb * dvb, offs[3]), per_head(lb, hb * dkb), const(nw), const(a_mat),
                 const(masks)]
    else:
        w2, gb, nw = params
        ins = [p3, p3, p3, p3, p3, w2, gb, nw, a_mat, masks, s0]
        specs = [slab(hb * dkb, offs[0]), slab(hb * dkb, offs[1]), slab(hb * dvb, offs[2]),
                 slab(hb * dvb, offs[3]),
                 pl.BlockSpec((1, r, LANE), lambda i, hg, c: (offs[4], rb0 + i * nchunks + c, 0)),
                 pl.BlockSpec((hb, LANE, dk), lambda i, hg, c: (hg, 0, 0)),
                 per_head(gb, hb * dkb), const(nw), const(a_mat), const(masks)]
    specs.append(pl.BlockSpec((1, nseq, hb, dk, dv), lambda i, hg, c: (s0_layer, i, hg, 0, 0)))
    aliases = {}
    _alias_or_dummy(o3, ins, specs, aliases, 0)
    _alias_or_dummy(st_all, ins, specs, aliases, 1)
    assert st_all is not None or layer == 0
    slots = 1 if st_all is not None else n_layers

    out = pl.pallas_call(
        functools.partial(_gla_kernel, mode=mode, nseq=nseq, seq_len=seq_len, hb=hb, dk=dk, dv=dv,
                          nchunks=nchunks, unroll=unroll),
        grid=(nrb, heads // hb, nchunks),
        in_specs=specs,
        out_specs=[pl.BlockSpec((hb * dvb, r, LANE),
                                lambda i, hg, c: (out_off // (hb * dvb) + hg, rb0 + i * nchunks + c, 0)),
                   pl.BlockSpec((slots, nseq, hb, dk, dv), lambda i, hg, c: (layer, i, hg, 0, 0))],
        out_shape=[jax.ShapeDtypeStruct((32, n_rows, LANE), F32),
                   jax.ShapeDtypeStruct((n_layers, nbatch, heads, dk, dv), F32)],
        scratch_shapes=[pltpu.VMEM((nseq, hb, dk, dv), F32),
                        pltpu.VMEM((hb, r, dk), F32),
                        pltpu.VMEM((hb, r, dk), F32),
                        pltpu.VMEM((hb, r, dk), F32),
                        pltpu.VMEM((hb, dk, LANE), F32),
                        pltpu.VMEM((hb, r, r), F32)],
        input_output_aliases=aliases,
        compiler_params=_params(3),
        name=mode + ("_seq" if nchunks > 1 else "_step"),
    )(*ins)
    return out[0], out[1]


def _conv_block(ref, i, cbi, cw_ref, cb_ref, tail, ext, xcs, nseq, seq_len, act):
    w = cw_ref[cbi]
    b = cb_ref[cbi]
    for s in range(nseq):
        ext[0:8, :] = tail[s, cbi]
        ext[8:8 + seq_len, :] = ref[i, s * seq_len:(s + 1) * seq_len, :]
        y = (b + w[3:4] * ext[8:8 + seq_len, :] + w[2:3] * ext[7:7 + seq_len, :]
             + w[1:2] * ext[6:6 + seq_len, :] + w[0:1] * ext[5:5 + seq_len, :])
        xcs[cbi, s * seq_len:(s + 1) * seq_len, :] = act(y)
        tail[s, cbi] = ext[seq_len:seq_len + 8, :]


def _ssd_kernel(z_ref, xa_ref, xb_ref, xc_ref, dt_ref, cw_ref, cb_ref, dtb_ref, alog_ref, dsk_ref,
                nw_ref, t_ref, tt_ref, cbuf0_ref, h0_ref, _o3_any, _h_any, o_ref, hout_ref, cbout_ref,
                h_scr, tail, xcs, ext, *, nseq, seq_len, nchunks):
    r = nseq * seq_len
    c = pl.program_id(1)

    @pl.when(c == 0)
    def _():
        h_scr[...] = h0_ref[0]
        tail[...] = cbuf0_ref[...]

    for part, ref in enumerate((xa_ref, xb_ref, xc_ref)):
        def body(i, carry, part=part, ref=ref):
            _conv_block(ref, i, part * 8 + i, cw_ref, cb_ref, tail, ext, xcs, nseq, seq_len,
                        jax.nn.silu)
            return carry
        lax.fori_loop(0, 8, body, 0)

    lane = lax.broadcasted_iota(jnp.int32, (r, LANE), 1)
    lo = lane < SSM_HEADDIM
    row_lo = lax.broadcasted_iota(jnp.int32, (LANE, LANE), 0) < SSM_HEADDIM
    seq_of_row = lax.broadcasted_iota(jnp.int32, (r, LANE), 0) >> int(math.log2(seq_len))
    tmat = t_ref[...]
    ttmat = tt_ref[...]
    causal = tmat.astype(F32) > 0.0

    def bcast_col(arr, j):
        return jnp.broadcast_to(arr[:, j:j + 1], (r, LANE))

    def group(g, carry):
        dt = _softplus(dt_ref[g] + dtb_ref[g])
        dta = dt * (-jnp.exp(alog_ref[g]))
        d3 = _split3(dta)
        cum = _sum3(jnp.dot(tmat, d3, preferred_element_type=F32), LANE)
        ct3 = lax.dot_general(d3, ttmat, _TN, preferred_element_type=F32)
        cum_t = ct3[0:LANE] + ct3[LANE:2 * LANE] + ct3[2 * LANE:3 * LANE]
        b_g = xcs[16 + g]
        c_g = xcs[20 + g]
        b_b = b_g.astype(BF16)
        cb_m = lax.dot_general(c_g.astype(BF16), b_b, _NT, preferred_element_type=F32)

        def decay_mat(j):
            rel = (jnp.broadcast_to(cum[:, j:j + 1], (r, r))
                   - jnp.broadcast_to(cum_t[j:j + 1, :], (r, r)))
            dec = jnp.where(causal, jnp.exp(jnp.where(causal, rel, 0.0)), 0.0)
            return (cb_m * dec).astype(BF16)

        ys = []
        for jj in range(4):
            j0, j1 = 2 * jj, 2 * jj + 1
            cbi = g * 4 + jj
            x_cb = xcs[cbi]
            dt_e = jnp.where(lo, bcast_col(dt, j0), bcast_col(dt, j1))
            cum_e = jnp.where(lo, bcast_col(cum, j0), bcast_col(cum, j1))
            u = x_cb * dt_e
            u_b = u.astype(BF16)
            y = jnp.where(lo,
                          jnp.dot(decay_mat(j0), u_b, preferred_element_type=F32),
                          jnp.dot(decay_mat(j1), u_b, preferred_element_type=F32))
            y_in = jnp.zeros((r, LANE), F32)
            for s in range(nseq):
                rl = s * seq_len + seq_len - 1
                h_cb = h_scr[s, cbi]
                if nseq == 1:
                    c_s = c_g
                    rel = cum_e[rl:rl + 1, :] - cum_e
                    uw = u * jnp.exp(rel)
                else:
                    in_seq = seq_of_row == s
                    c_s = jnp.where(in_seq, c_g, 0.0)
                    rel = jnp.where(in_seq, cum_e[rl:rl + 1, :] - cum_e, 0.0)
                    uw = jnp.where(in_seq, u * jnp.exp(rel), 0.0)
                y_in = y_in + lax.dot_general(c_s.astype(BF16), h_cb.astype(BF16), _NT,
                                              preferred_element_type=F32)
                last = jnp.where(row_lo,
                                 jnp.broadcast_to(cum[rl:rl + 1, j0:j0 + 1], (LANE, LANE)),
                                 jnp.broadcast_to(cum[rl:rl + 1, j1:j1 + 1], (LANE, LANE)))
                h_scr[s, cbi] = jnp.exp(last) * h_cb + lax.dot_general(
                    uw.astype(BF16), b_b, _TN, preferred_element_type=F32)
            y = y + y_in * jnp.exp(cum_e) + dsk_ref[cbi] * x_cb
            ys.append(y * jax.nn.silu(z_ref[cbi]))
        ms = sum(jnp.sum(y * y, axis=1, keepdims=True) for y in ys) * (1.0 / (4 * LANE))
        rinv = lax.rsqrt(ms + EPS)
        for jj in range(4):
            o_ref[g * 4 + jj] = ys[jj] * rinv * nw_ref[g * 4 + jj]
        return carry

    lax.fori_loop(0, SSM_GROUPS, group, 0, unroll=True)

    @pl.when(c == nchunks - 1)
    def _():
        hout_ref[0] = h_scr[...]
        for other in range(1, hout_ref.shape[0]):
            hout_ref[other] = jnp.zeros(hout_ref.shape[1:], F32)
        cbout_ref[...] = tail[...]


def _tri_consts(nseq, seq_len):
    r = nseq * seq_len
    t = np.arange(r)[:, None]
    s = np.arange(r)[None, :]
    m = (((t // seq_len) == (s // seq_len)) & (s <= t)).astype(np.float32)
    return jnp.asarray(m, BF16), jnp.asarray(m.T, BF16)


def ssd_mixer(p3, o3, h0, h0_layer, h_all, layer, n_layers, cbuf0, params, *, row0, nbatch, t_len,
              nseq):
    seq_len = min(CHUNK, t_len)
    nchunks = t_len // seq_len
    r = nseq * seq_len
    nrb = nbatch // nseq
    rb0 = row0 // r
    cw, cbias, dtb, alog, dsk, nw = params
    tmat, ttmat = _tri_consts(nseq, seq_len)
    n_rows = p3.shape[1]

    def slab(nblk, blk_idx):
        return pl.BlockSpec((nblk, r, LANE), lambda i, c: (blk_idx, rb0 + i * nchunks + c, 0))

    def const(arr):
        nd = arr.ndim
        return pl.BlockSpec(arr.shape, lambda i, c: (0,) * nd)

    ins = [p3, p3, p3, p3, p3, cw, cbias, dtb, alog, dsk, nw, tmat, ttmat, cbuf0, h0]
    specs = [slab(16, 4), slab(8, 10), slab(8, 11), slab(8, 12), slab(4, 26),
             const(cw), const(cbias), const(dtb), const(alog), const(dsk), const(nw),
             const(tmat), const(ttmat),
             pl.BlockSpec((nseq, 24, 8, LANE), lambda i, c: (i, 0, 0, 0)),
             pl.BlockSpec((1, nseq, 16, LANE, LANE), lambda i, c: (h0_layer, i, 0, 0, 0))]
    aliases = {}
    _alias_or_dummy(o3, ins, specs, aliases, 0)
    _alias_or_dummy(h_all, ins, specs, aliases, 1)
    assert h_all is not None or layer == 0
    slots = 1 if h_all is not None else n_layers
    out = pl.pallas_call(
        functools.partial(_ssd_kernel, nseq=nseq, seq_len=seq_len, nchunks=nchunks),
        grid=(nrb, nchunks),
        in_specs=specs,
        out_specs=[pl.BlockSpec((16, r, LANE), lambda i, c: (1, rb0 + i * nchunks + c, 0)),
                   pl.BlockSpec((slots, nseq, 16, LANE, LANE), lambda i, c: (layer, i, 0, 0, 0)),
                   pl.BlockSpec((nseq, 24, 8, LANE), lambda i, c: (i, 0, 0, 0))],
        out_shape=[jax.ShapeDtypeStruct((32, n_rows, LANE), F32),
                   jax.ShapeDtypeStruct((n_layers, nbatch, 16, LANE, LANE), F32),
                   jax.ShapeDtypeStruct((nbatch, 24, 8, LANE), F32)],
        scratch_shapes=[pltpu.VMEM((nseq, 16, LANE, LANE), F32),
                        pltpu.VMEM((nseq, 24, 8, LANE), F32),
                        pltpu.VMEM((24, r, LANE), F32),
                        pltpu.VMEM((seq_len + 8, LANE), F32)],
        input_output_aliases=aliases,
        compiler_params=_params(2),
        name="ssd_seq" if nchunks > 1 else "ssd_step",
    )(*ins)
    return out[0], out[1], out[2]


def _lru_kernel(x_ref, y_ref, cw_ref, cb_ref, wa_ref, ba_ref, wx_ref, bx_ref, lam_ref, cbuf0_ref,
                h0_ref, _, o_ref, hout_ref, cbout_ref, h_scr, tail, xcs, ext,
                *, nseq, seq_len, nchunks):
    r = nseq * seq_len
    sub = min(seq_len, CHUNK)
    c = pl.program_id(2)

    @pl.when(c == 0)
    def _():
        h_scr[...] = h0_ref[0]
        tail[...] = cbuf0_ref[...]

    _conv_block(x_ref, 0, 0, cw_ref, cb_ref, tail, ext, xcs, nseq, seq_len, lambda v: v)
    xc = xcs[0]
    xc_b = xc.astype(BF16)
    rg = jax.nn.sigmoid(jnp.dot(xc_b, wa_ref[0], preferred_element_type=F32) + ba_ref[0])
    ig = jax.nn.sigmoid(jnp.dot(xc_b, wx_ref[0], preferred_element_type=F32) + bx_ref[0])
    log_a = -LRU_C * rg * _softplus(-lam_ref[0])
    a_cum = jnp.exp(log_a)
    th = jnp.tanh(log_a)
    u_cum = jnp.sqrt(-2.0 * th / (1.0 - th)) * (ig * xc)

    pos = lax.broadcasted_iota(jnp.int32, (r, LANE), 0) & (sub - 1)
    d = 1
    while d < sub:
        take = pos >= d
        a_prev = pltpu.roll(a_cum, d, axis=0)
        u_prev = pltpu.roll(u_cum, d, axis=0)
        u_cum = jnp.where(take, a_cum * u_prev + u_cum, u_cum)
        a_cum = jnp.where(take, a_cum * a_prev, a_cum)
        d *= 2

    gel = jax.nn.gelu(y_ref[0])
    for s in range(nseq):
        h_prev = h_scr[s]
        for sb in range(seq_len // sub):
            r0 = s * seq_len + sb * sub
            hs = a_cum[r0:r0 + sub] * h_prev + u_cum[r0:r0 + sub]
            o_ref[0, r0:r0 + sub, :] = hs * gel[r0:r0 + sub]
            h_prev = hs[sub - 1:sub]
        h_scr[s] = h_prev

    @pl.when(c == nchunks - 1)
    def _():
        hout_ref[0] = h_scr[...]
        cbout_ref[...] = tail[...]


def lru_mixer(p3, o3, h0, cbuf0, params, *, row0, nbatch, t_len, nseq, rows):
    seq_len = min(rows, t_len)
    nchunks = t_len // seq_len
    r = nseq * seq_len
    nrb = nbatch // nseq
    rb0 = row0 // r
    cw, cbias, wa, ba, wx, bx, lam = params
    n_rows = p3.shape[1]

    def slab(off):
        return pl.BlockSpec((1, r, LANE), lambda i, n, c: (off + n, rb0 + i * nchunks + c, 0))

    def per_blk(arr):
        shp = (1,) + arr.shape[1:]
        nd = arr.ndim
        return pl.BlockSpec(shp, lambda i, n, c: (n,) + (0,) * (nd - 1))

    ins = [p3, p3, cw, cbias, wa, ba, wx, bx, lam, cbuf0, h0, o3]
    specs = [slab(48), slab(64), per_blk(cw), per_blk(cbias), per_blk(wa), per_blk(ba),
             per_blk(wx), per_blk(bx), per_blk(lam),
             pl.BlockSpec((nseq, 1, 8, LANE), lambda i, n, c: (i, n, 0, 0)),
             pl.BlockSpec((1, nseq, 1, LANE), lambda i, n, c: (n, i, 0, 0)),
             pl.BlockSpec(memory_space=pl.ANY)]
    out = pl.pallas_call(
        functools.partial(_lru_kernel, nseq=nseq, seq_len=seq_len, nchunks=nchunks),
        grid=(nrb, 16, nchunks),
        in_specs=specs,
        out_specs=[pl.BlockSpec((1, r, LANE), lambda i, n, c: (16 + n, rb0 + i * nchunks + c, 0)),
                   pl.BlockSpec((1, nseq, 1, LANE), lambda i, n, c: (n, i, 0, 0)),
                   pl.BlockSpec((nseq, 1, 8, LANE), lambda i, n, c: (i, n, 0, 0))],
        out_shape=[jax.ShapeDtypeStruct((32, n_rows, LANE), F32),
                   jax.ShapeDtypeStruct((16, nbatch, 1, LANE), F32),
                   jax.ShapeDtypeStruct((nbatch, 16, 8, LANE), F32)],
        scratch_shapes=[pltpu.VMEM((nseq, 1, LANE), F32),
                        pltpu.VMEM((nseq, 1, 8, LANE), F32),
                        pltpu.VMEM((1, r, LANE), F32),
                        pltpu.VMEM((seq_len + 8, LANE), F32)],
        input_output_aliases={len(ins) - 1: 0},
        compiler_params=_params(3),
        name="lru_seq" if nchunks > 1 else "lru_step",
    )(*ins)
    return out[0], out[1], out[2]


def _slabs(v, nblk):
    return v.astype(F32).reshape(nblk, 1, LANE)


def _pad_lanes(a, width=LANE):
    return jnp.pad(a, [(0, 0)] * (a.ndim - 1) + [(0, width - a.shape[-1])])


def _conv_state_in(buf, nblk):
    b = buf.shape[0]
    t = jnp.transpose(buf.astype(F32).reshape(b, 3, nblk, LANE), (0, 2, 1, 3))
    return jnp.pad(t, ((0, 0), (0, 0), (5, 0), (0, 0)))


def _conv_state_out(t):
    b, nblk = t.shape[:2]
    return jnp.transpose(t[:, :, 5:8, :], (0, 2, 1, 3)).reshape(b, 3, nblk * LANE)


EVEN_MAIN = 13312
ODD_GLOW = (6144, 6160)
IN_TN = 1024


def _prep_w_in_even(w):
    nl, d, _ = w.shape
    dt = _pad_lanes(w[:, :, EVEN_MAIN:].reshape(nl, d, SSM_GROUPS, SSM_HPG))
    return w.astype(BF16), dt.reshape(nl, d, SSM_GROUPS * LANE).astype(BF16)


def _prep_w_in_odd(w):
    g0, g1 = ODD_GLOW
    main = jnp.concatenate([w[:, :, :g0], w[:, :, g1:]], axis=2).astype(BF16)
    return main, _pad_lanes(w[:, :, g0:g1]).astype(BF16)


def kernel(x_prompt, x_sample, state_hgrn, state_ssm, state_ssm_conv, state_gla, state_lru,
           state_lru_conv, norm_mix_pre, norm_mix_post, norm_ffn_pre, norm_ffn_post,
           w_in_even, w_out_even, hgrn_lb_logits, hgrn_norm_w,
           ssm_conv_w, ssm_conv_b, ssm_dt_bias, ssm_a_log, ssm_d, ssm_norm_w,
           w_in_odd, w_out_odd, gla_gate_w2, gla_gate_b, gla_norm_w,
           lru_conv_w, lru_conv_b, lru_wa, lru_ba, lru_wx, lru_bx, lru_lambda,
           ffn_w_gate, ffn_w_up, ffn_w_down):
    bp, tp, d = x_prompt.shape
    bs, ts, _ = x_sample.shape
    n_p, n_s = bp * tp, bs * ts
    depth = norm_mix_pre.shape[0]
    x = jnp.concatenate([x_prompt.reshape(n_p, d), x_sample.reshape(n_s, d)], axis=0)

    lb_soft = jax.nn.softmax(hgrn_lb_logits.astype(F32), axis=0)
    hgrn_lb = jnp.maximum(jnp.cumsum(lb_soft, axis=0) - lb_soft[0], 0.0)

    paths = (dict(row0=0, nbatch=bp, t_len=tp), dict(row0=n_p, nbatch=bs, t_len=ts))
    outs = {k: ([], []) for k in ("ssm_cb", "lru", "lru_cb")}
    big = {k: [None, None] for k in ("hg", "ssm", "gla")}
    n_even, n_odd = w_in_even.shape[0], w_in_odd.shape[0]

    we_main, we_dt = _prep_w_in_even(w_in_even)
    wo_main, wo_glow = _prep_w_in_odd(w_in_odd)
    w_out_e, w_out_o = w_out_even.astype(BF16), w_out_odd.astype(BF16)
    wg, wu, wd = ffn_w_gate.astype(BF16), ffn_w_up.astype(BF16), ffn_w_down.astype(BF16)
    st_hgrn = state_hgrn.astype(F32)
    st_ssm = state_ssm.astype(F32).reshape(n_even, bs, 16, LANE, LANE)
    st_gla = state_gla.astype(F32)
    zero_hg = jnp.zeros((1, bp) + state_hgrn.shape[2:], F32)
    zero_ssm = jnp.zeros((1, bp, 16, LANE, LANE), F32)
    zero_gla = jnp.zeros((1, bp) + state_gla.shape[2:], F32)
    o3_buf = jnp.zeros((32, n_p + n_s, LANE), F32)

    for l in range(depth):
        j = l // 2
        if l % 2 == 0:
            p3 = inproj(x, norm_mix_pre[l],
                        [(we_main, j, EVEN_MAIN // IN_TN, IN_TN), (we_dt, j, 1, SSM_GROUPS * LANE)],
                        tn=IN_TN)
            hg_params = (_slabs(hgrn_lb[j], 16), _slabs(hgrn_norm_w[j], 1))
            dsk = jnp.repeat(ssm_d[j].astype(F32), SSM_HEADDIM)
            ssd_params = (
                jnp.transpose(ssm_conv_w[j].astype(F32).reshape(4, 24, LANE), (1, 0, 2)),
                _slabs(ssm_conv_b[j], 24),
                _pad_lanes(ssm_dt_bias[j].astype(F32).reshape(SSM_GROUPS, 1, SSM_HPG)),
                _pad_lanes(ssm_a_log[j].astype(F32).reshape(SSM_GROUPS, 1, SSM_HPG)),
                _slabs(dsk, 16), _slabs(ssm_norm_w[j], 16))
            o3 = o3_buf
            for pi, path in enumerate(paths):
                nb = path["nbatch"]
                if pi == 0:
                    s_hg, s_ssm, s_layer = zero_hg, zero_ssm, 0
                    s_cb = jnp.zeros((nb, 24, 8, LANE), F32)
                    nseq_a, hb_a, unroll_a, nseq_b = 1, 16, 8, 1
                else:
                    s_hg, s_ssm, s_layer = st_hgrn, st_ssm, j
                    s_cb = _conv_state_in(state_ssm_conv[j], 24)
                    nseq_a, hb_a, unroll_a, nseq_b = 8, 8, 4, 4
                o3, big["hg"][pi] = gla_mixer(
                    "hgrn", p3, o3, s_hg, s_layer, big["hg"][pi], j, n_even, hg_params,
                    nseq=nseq_a, hb=hb_a, heads=16, dk=128, dv=128, offs=(0, 16, 32, 48), out_off=0,
                    unroll=unroll_a, **path)
                o3, big["ssm"][pi], cb_b = ssd_mixer(p3, o3, s_ssm, s_layer, big["ssm"][pi], j,
                                                     n_even, s_cb, ssd_params, nseq=nseq_b, **path)
                outs["ssm_cb"][pi].append(_conv_state_out(cb_b).astype(state_ssm_conv.dtype))
            x = outproj(o3, w_out_e, j, x, norm_mix_post[l])
            o3_buf = o3
        else:
            p3 = inproj(x, norm_mix_pre[l],
                        [(wo_main, j, wo_main.shape[2] // IN_TN, IN_TN), (wo_glow, j, 1, LANE)],
                        tn=IN_TN)
            w2 = jnp.pad(gla_gate_w2[j], ((0, LANE - gla_gate_w2.shape[1]), (0, 0)))
            w2 = jnp.transpose(w2.reshape(LANE, 4, 256), (1, 0, 2)).astype(BF16)
            gla_params = (w2, _slabs(gla_gate_b[j], 8), _slabs(gla_norm_w[j], 4))
            lru_params = (
                jnp.transpose(lru_conv_w[j].astype(F32).reshape(4, 16, LANE), (1, 0, 2)),
                _slabs(lru_conv_b[j], 16), lru_wa[j].astype(BF16), _slabs(lru_ba[j], 16),
                lru_wx[j].astype(BF16), _slabs(lru_bx[j], 16), _slabs(lru_lambda[j], 16))
            o3 = o3_buf
            for pi, path in enumerate(paths):
                nb = path["nbatch"]
                if pi == 0:
                    s_gla, s_layer = zero_gla, 0
                    s_lru = jnp.zeros((16, nb, 1, LANE), F32)
                    s_cb = jnp.zeros((nb, 16, 8, LANE), F32)
                    nseq_c, hb_c, nseq_d, rows_d = 1, 4, 1, 512
                else:
                    s_gla, s_layer = st_gla, j
                    s_lru = jnp.transpose(state_lru[j].astype(F32).reshape(nb, 16, 1, LANE),
                                          (1, 0, 2, 3))
                    s_cb = _conv_state_in(state_lru_conv[j], 16)
                    nseq_c, hb_c, nseq_d, rows_d = 8, 1, 16, 8
                o3, big["gla"][pi] = gla_mixer(
                    "gla", p3, o3, s_gla, s_layer, big["gla"][pi], j, n_odd, gla_params,
                    nseq=nseq_c, hb=hb_c, heads=4, dk=256, dv=512, offs=(0, 8, 16, 32, 80),
                    out_off=0, unroll=min(4, hb_c), **path)
                o3, st_d, cb_d = lru_mixer(p3, o3, s_lru, s_cb, lru_params, nseq=nseq_d,
                                           rows=rows_d, **path)
                outs["lru"][pi].append(jnp.transpose(st_d, (1, 0, 2, 3)).reshape(nb, 16 * LANE)
                                       .astype(state_lru.dtype))
                outs["lru_cb"][pi].append(_conv_state_out(cb_d).astype(state_lru_conv.dtype))
            x = outproj(o3, w_out_o, j, x, norm_mix_post[l])
            o3_buf = o3
        x = ffn(x, norm_ffn_pre[l], wg, wu, wd, l, norm_ffn_post[l])

    y_prompt = x[:n_p].reshape(bp, tp, d).astype(x_prompt.dtype)
    y_sample = x[n_p:].reshape(bs, ts, d).astype(x_sample.dtype)
    ssm_shape = state_ssm.shape[2:]
    res = [y_prompt, y_sample]
    stacked = lambda key, pi: jnp.stack(outs[key][pi])
    res += [big["hg"][0].astype(state_hgrn.dtype), big["hg"][1].astype(state_hgrn.dtype),
            big["ssm"][0].reshape((n_even, bp) + ssm_shape).astype(state_ssm.dtype),
            big["ssm"][1].reshape((n_even, bs) + ssm_shape).astype(state_ssm.dtype),
            stacked("ssm_cb", 0), stacked("ssm_cb", 1),
            big["gla"][0].astype(state_gla.dtype), big["gla"][1].astype(state_gla.dtype),
            stacked("lru", 0), stacked("lru", 1), stacked("lru_cb", 0), stacked("lru_cb", 1)]
    return tuple(res)
```

```python
import functools
import math

import numpy as np
import jax
import jax.numpy as jnp
from jax import lax
from jax.experimental import pallas as pl
from jax.experimental.pallas import tpu as pltpu

F32 = jnp.float32
BF16 = jnp.bfloat16
LANE = 128
VMEM_LIMIT = 56 * 1024 * 1024

DOWN_STRIP = 512

EPS = 1e-6
F_MIN = 1e-30
CHUNK = 64
GLA_GATE_NORMALIZER = 16.0
LRU_C = 8.0
SSM_GROUPS = 4
SSM_HPG = 8
SSM_HEADDIM = 64


def _params(n_axes):
    return pltpu.CompilerParams(dimension_semantics=("arbitrary",) * n_axes,
                                vmem_limit_bytes=VMEM_LIMIT)


def _softplus(x):
    return jnp.maximum(x, 0.0) + jnp.log1p(jnp.exp(-jnp.abs(x)))


def _log_sigmoid(x):
    return jnp.minimum(x, 0.0) - jnp.log1p(jnp.exp(-jnp.abs(x)))


def _split3(x):
    hi = x.astype(BF16)
    r1 = x - hi.astype(F32)
    mid = r1.astype(BF16)
    lo = (r1 - mid.astype(F32)).astype(BF16)
    return jnp.concatenate([hi, mid, lo], axis=1)


def _sum3(y, w):
    return y[:, :w] + y[:, w:2 * w] + y[:, 2 * w:3 * w]


def _cat(ref, base, n):
    if n == 1:
        return ref[base]
    return jnp.concatenate([ref[base + j] for j in range(n)], axis=1)


_NT = (((1,), (1,)), ((), ()))
_TN = (((0,), (0,)), ((), ()))


def _prenorm_rows(x_ref, w_ref, h_scr, tm, rows=64):
    w = w_ref[...]

    def body(i, c):
        r0 = pl.multiple_of(i * rows, rows)
        x = x_ref[pl.ds(r0, rows), :]
        ms = jnp.mean(x * x, axis=1, keepdims=True)
        h_scr[pl.ds(r0, rows), :] = (x * lax.rsqrt(ms + EPS) * w).astype(BF16)
        return c

    lax.fori_loop(0, tm // rows, body, 0)


def _postnorm_rows(o_ref, x_ref, w_ref, tm, rows=64):
    w = w_ref[...]

    def body(i, c):
        r0 = pl.multiple_of(i * rows, rows)
        y = o_ref[pl.ds(r0, rows), :]
        ms = jnp.mean(y * y, axis=1, keepdims=True)
        o_ref[pl.ds(r0, rows), :] = x_ref[pl.ds(r0, rows), :] + y * lax.rsqrt(ms + EPS) * w
        return c

    lax.fori_loop(0, tm // rows, body, 0)


def _resident(shape, index_map):
    return pl.BlockSpec(shape, index_map, pipeline_mode=pl.Buffered(1))


def _inproj_kernel(x_ref, nw_ref, *rest, tm, segs):
    w_refs, o_ref, h_scr = rest[:len(segs)], rest[len(segs)], rest[len(segs) + 1]
    j = pl.program_id(1)

    @pl.when(j == 0)
    def _():
        _prenorm_rows(x_ref, nw_ref, h_scr, tm)

    for w_ref, (start, ntiles, width) in zip(w_refs, segs):
        @pl.when((j >= start) & (j < start + ntiles))
        def _(w_ref=w_ref, width=width):
            r = jnp.dot(h_scr[...], w_ref[0], preferred_element_type=F32)
            for c in range(width // LANE):
                o_ref[c] = r[:, c * LANE:(c + 1) * LANE]
            for c in range(width // LANE, o_ref.shape[0]):
                o_ref[c] = jnp.zeros(o_ref.shape[1:], F32)


def inproj(x, norm_w, segments, *, tm=512, tn=1024):
    n, d = x.shape
    cb = tn // LANE
    segs, specs, start = [], [], 0
    for w, layer, ntiles, width in segments:
        segs.append((start, ntiles, width))
        specs.append(pl.BlockSpec(
            (1, d, width),
            lambda i, j, layer=layer, start=start, ntiles=ntiles: (layer, 0, jnp.clip(j - start, 0, ntiles - 1))))
        start += ntiles
    return pl.pallas_call(
        functools.partial(_inproj_kernel, tm=tm, segs=tuple(segs)),
        grid=(n // tm, start),
        in_specs=[pl.BlockSpec((tm, d), lambda i, j: (i, 0)),
                  pl.BlockSpec((1, d), lambda i, j: (0, 0))] + specs,
        out_specs=pl.BlockSpec((cb, tm, LANE), lambda i, j: (j, i, 0)),
        out_shape=jax.ShapeDtypeStruct((start * cb, n, LANE), F32),
        scratch_shapes=[pltpu.VMEM((tm, d), BF16)],
        compiler_params=_params(2),
        name="inproj",
    )(x, norm_w.reshape(1, d), *[s[0] for s in segments])


def _outproj_kernel(a_ref, w3_ref, x_ref, nw_ref, o_ref, *, tm, kb, nk):
    k = pl.program_id(1)
    w_ref = w3_ref.at[0]
    a = _cat(a_ref, 0, kb).astype(BF16)

    @pl.when(k == 0)
    def _():
        o_ref[...] = jnp.zeros_like(o_ref)

    for n0 in range(0, o_ref.shape[1], DOWN_STRIP):
        o_ref[:, n0:n0 + DOWN_STRIP] += jnp.dot(a, w_ref[:, n0:n0 + DOWN_STRIP],
                                                preferred_element_type=F32)

    @pl.when(k == nk - 1)
    def _():
        _postnorm_rows(o_ref, x_ref, nw_ref, tm)


def outproj(a3, w, layer, x, norm_w, *, tm=512, tk=1024):
    n, d = x.shape
    kdim = w.shape[1]
    kb = tk // LANE
    nk = kdim // tk
    return pl.pallas_call(
        functools.partial(_outproj_kernel, tm=tm, kb=kb, nk=nk),
        grid=(n // tm, nk),
        in_specs=[pl.BlockSpec((kb, tm, LANE), lambda i, k: (k, i, 0)),
                  pl.BlockSpec((1, tk, d), lambda i, k: (layer, k, 0)),
                  pl.BlockSpec((tm, d), lambda i, k: (i, 0)),
                  pl.BlockSpec((1, d), lambda i, k: (0, 0))],
        out_specs=pl.BlockSpec((tm, d), lambda i, k: (i, 0)),
        out_shape=jax.ShapeDtypeStruct((n, d), F32),
        compiler_params=_params(2),
        name="outproj",
    )(a3, w, x, norm_w.reshape(1, d))


def _ffn_kernel(x_ref, pre_ref, wg_ref, wu_ref, wd3_ref, post_ref, o_ref, h_scr, a_scr, *, tm, nf):
    f = pl.program_id(1)
    wd_ref = wd3_ref.at[0]

    @pl.when(f == 0)
    def _():
        _prenorm_rows(x_ref, pre_ref, h_scr, tm)
        o_ref[...] = jnp.zeros_like(o_ref)
        a_scr[...] = jnp.zeros_like(a_scr)

    a_prev = a_scr[...]
    h = h_scr[...]
    g = jnp.dot(h, wg_ref[0], preferred_element_type=F32)
    u = jnp.dot(h, wu_ref[0], preferred_element_type=F32)
    for n0 in range(0, o_ref.shape[1], DOWN_STRIP):
        o_ref[:, n0:n0 + DOWN_STRIP] += jnp.dot(a_prev, wd_ref[:, n0:n0 + DOWN_STRIP],
                                                preferred_element_type=F32)
    a_scr[...] = (jax.nn.silu(g) * u).astype(BF16)

    @pl.when(f == nf)
    def _():
        _postnorm_rows(o_ref, x_ref, post_ref, tm)


def ffn(x, pre_w, wg, wu, wd, layer, post_w, *, tm=512, tf=256):
    n, d = x.shape
    dff = wg.shape[2]
    nf = dff // tf
    return pl.pallas_call(
        functools.partial(_ffn_kernel, tm=tm, nf=nf),
        grid=(n // tm, nf + 1),
        in_specs=[pl.BlockSpec((tm, d), lambda i, f: (i, 0)),
                  pl.BlockSpec((1, d), lambda i, f: (0, 0)),
                  pl.BlockSpec((1, d, tf), lambda i, f: (layer, 0, jnp.minimum(f, nf - 1))),
                  pl.BlockSpec((1, d, tf), lambda i, f: (layer, 0, jnp.minimum(f, nf - 1))),
                  pl.BlockSpec((1, tf, d), lambda i, f: (layer, jnp.maximum(f - 1, 0), 0)),
                  pl.BlockSpec((1, d), lambda i, f: (0, 0))],
        out_specs=pl.BlockSpec((tm, d), lambda i, f: (i, 0)),
        out_shape=jax.ShapeDtypeStruct((n, d), F32),
        scratch_shapes=[pltpu.VMEM((tm, d), BF16), pltpu.VMEM((tm, tf), BF16)],
        compiler_params=_params(2),
        name="ffn",
    )(x, pre_w.reshape(1, d), wg, wu, wd, post_w.reshape(1, d))


def _gla_consts(nseq, seq_len):
    r = nseq * seq_len
    t = np.arange(r)[:, None]
    s = np.arange(r)[None, :]
    prefix = ((t // seq_len) == (s // seq_len)) & (s <= t)
    masks = []
    c = seq_len // 2
    while c >= 1:
        right = (t % (2 * c)) >= c
        masks.append(((t // (2 * c)) == (s // (2 * c))) & right & ((s % (2 * c)) < c))
        c //= 2
    m = np.stack(masks).astype(np.float32)
    return jnp.asarray(prefix.astype(np.float32), BF16), jnp.asarray(m, F32)


def _block_row(b, blk, pick, pos):
    r, w = b.shape
    if blk >= 8:
        return jnp.concatenate(
            [jnp.broadcast_to(b[s + pick:s + pick + 1, :], (blk, w)) for s in range(0, r, blk)], axis=0)
    out = b
    for p in range(blk):
        if p != pick:
            out = jnp.where(pos == p, pltpu.roll(b, (p - pick) % r, axis=0), out)
    return out


def _gla_kernel(*refs, mode, nseq, seq_len, hb, dk, dv, nchunks, unroll):
    s_scr, q_scr, k_scr, b_scr, dc_scr, sc_scr = refs[-6:]
    if mode == "hgrn":
        (q_ref, f_ref, v_ref, gt_ref, lb_ref, nw_ref, a_ref, m_ref, s0_ref, _, _,
         o_ref, so_ref) = refs[:-6]
    else:
        (q_ref, k_ref, v_ref, gt_ref, gl_ref, w2_ref, gb_ref, nw_ref, a_ref, m_ref, s0_ref, _, _,
         o_ref, so_ref) = refs[:-6]
    dkb, dvb = dk // LANE, dv // LANE
    r = nseq * seq_len
    nlev = int(math.log2(seq_len))
    c = pl.program_id(2)

    @pl.when(c == 0)
    def _():
        s_scr[...] = s0_ref[0]

    a_mat = a_ref[...]
    seq_shift = int(math.log2(seq_len))
    rowk = lax.broadcasted_iota(jnp.int32, (r, dk), 0)
    colk = lax.broadcasted_iota(jnp.int32, (dk, r), 1)
    seq_of_row = lax.broadcasted_iota(jnp.int32, (r, LANE), 0) >> seq_shift
    lane_id = lax.broadcasted_iota(jnp.int32, (r, LANE), 1)
    seqsel = (seq_of_row == lane_id).astype(BF16)
    nw = _cat(nw_ref, 0, dvb)

    def prepare(h, carry):
        qr = _cat(q_ref, h * dkb, dkb)
        if mode == "hgrn":
            lb = _cat(lb_ref, h * dkb, dkb)
            fg = lb + (1.0 - lb) * jax.nn.sigmoid(_cat(f_ref, h * dkb, dkb))
            g = jnp.log(jnp.maximum(fg, F_MIN))
            k = 1.0 - fg
            q = jax.nn.silu(qr) * (dk ** -0.5)
        else:
            k = _cat(k_ref, h * dkb, dkb)
            q = qr * (dk ** -0.5)
            lin = jnp.dot(gl_ref[0].astype(BF16), w2_ref[h], preferred_element_type=F32)
            g = _log_sigmoid(lin + _cat(gb_ref, h * dkb, dkb)) / GLA_GATE_NORMALIZER

        g3 = _split3(g)
        q_scr[h] = q
        k_scr[h] = k
        b_scr[h] = _sum3(jnp.dot(a_mat, g3, preferred_element_type=F32), dk)
        d3 = lax.dot_general(g3, seqsel, _TN, preferred_element_type=F32)
        dc_scr[h] = d3[0:dk] + d3[dk:2 * dk] + d3[2 * dk:3 * dk]
        return carry

    def intra(h, carry):
        q, k, b = q_scr[h], k_scr[h], b_scr[h]
        scores = jnp.zeros((r, r), F32)
        for l in range(nlev):
            half = seq_len >> (l + 1)
            pos = rowk & (2 * half - 1)
            right = pos >= half
            b_m = _block_row(b, 2 * half, half - 1, pos)
            x = (jnp.where(right, q, k) * jnp.exp(-jnp.abs(b - b_m))).astype(BF16)
            scores = scores + m_ref[l] * lax.dot_general(x, x, _NT, preferred_element_type=F32)
        sc_scr[h] = scores
        return carry

    def combine(h, carry):
        q, k, b = q_scr[h], k_scr[h], b_scr[h]
        v = _cat(v_ref, h * dvb, dvb)
        gate = _cat(gt_ref, h * dvb, dvb)
        vb = v.astype(BF16)
        o = jnp.dot(sc_scr[h].astype(BF16), vb, preferred_element_type=F32)
        o = o + jnp.sum(q * k, axis=1, keepdims=True) * v

        qb = q * jnp.exp(b)
        kb = k * jnp.exp(_block_row(b, seq_len, seq_len - 1, None) - b)
        dcol = dc_scr[h]
        if nseq > 1:
            kb_t = kb.T
        for s in range(nseq):
            st = s_scr[s, h]
            if nseq == 1:
                qs = qb
                upd = lax.dot_general(kb.astype(BF16), vb, _TN, preferred_element_type=F32)
            else:
                qs = jnp.where((rowk >> seq_shift) == s, qb, 0.0)
                ks_t = jnp.where((colk >> seq_shift) == s, kb_t, 0.0)
                upd = jnp.dot(ks_t.astype(BF16), vb, preferred_element_type=F32)
            o = o + jnp.dot(qs.astype(BF16), st.astype(BF16), preferred_element_type=F32)
            dec = jnp.exp(jnp.broadcast_to(dcol[:, s:s + 1], (dk, dv)))
            s_scr[s, h] = dec * st + upd

        ms = jnp.mean(o * o, axis=1, keepdims=True)
        y = o * lax.rsqrt(ms + EPS) * nw * jax.nn.silu(gate)
        for j in range(dvb):
            o_ref[h * dvb + j] = y[:, j * LANE:(j + 1) * LANE]
        return carry

    for stage in (prepare, intra, combine):
        lax.fori_loop(0, hb, stage, 0, unroll=unroll)

    @pl.when(c == nchunks - 1)
    def _():
        so_ref[0] = s_scr[...]
        for other in range(1, so_ref.shape[0]):
            so_ref[other] = jnp.zeros(so_ref.shape[1:], F32)


def _alias_or_dummy(arr, ins, specs, aliases, out_idx):
    if arr is None:
        ins.append(jnp.zeros((8, LANE), F32))
    else:
        ins.append(arr)
        aliases[len(ins) - 1] = out_idx
    specs.append(pl.BlockSpec(memory_space=pl.ANY))


def gla_mixer(mode, p3, o3, s0, s0_layer, st_all, layer, n_layers, params, *, row0, nbatch, t_len,
              nseq, hb, heads, dk, dv, offs, out_off, unroll):
    seq_len = min(CHUNK, t_len)
    nchunks = t_len // seq_len
    r = nseq * seq_len
    dkb, dvb = dk // LANE, dv // LANE
    nrb = nbatch // nseq
    rb0 = row0 // r
    a_mat, masks = _gla_consts(nseq, seq_len)
    n_rows = p3.shape[1]

    def slab(nblk, off):
        return pl.BlockSpec((nblk, r, LANE),
                            lambda i, hg, c: (off // nblk + hg, rb0 + i * nchunks + c, 0))

    def const(arr):
        nd = arr.ndim
        return pl.BlockSpec(arr.shape, lambda i, hg, c: (0,) * nd)

    def per_head(arr, nblk):
        return pl.BlockSpec((nblk, 1, LANE), lambda i, hg, c: (hg, 0, 0))

    if mode == "hgrn":
        lb, nw = params
        ins = [p3, p3, p3, p3, lb, nw, a_mat, masks, s0]
        specs = [slab(hb * dkb, offs[0]), slab(hb * dkb, offs[1]), slab(hb * dvb, offs[2]),
                 slab(hb * dvb, offs[3]), per_head(lb, hb * dkb), const(nw), const(a_mat),
                 const(masks)]
    else:
        w2, gb, nw = params
        ins = [p3, p3, p3, p3, p3, w2, gb, nw, a_mat, masks, s0]
        specs = [slab(hb * dkb, offs[0]), slab(hb * dkb, offs[1]), slab(hb * dvb, offs[2]),
                 slab(hb * dvb, offs[3]),
                 pl.BlockSpec((1, r, LANE), lambda i, hg, c: (offs[4], rb0 + i * nchunks + c, 0)),
                 pl.BlockSpec((hb, LANE, dk), lambda i, hg, c: (hg, 0, 0)),
                 per_head(gb, hb * dkb), const(nw), const(a_mat), const(masks)]
    specs.append(pl.BlockSpec((1, nseq, hb, dk, dv), lambda i, hg, c: (s0_layer, i, hg, 0, 0)))
    aliases = {}
    _alias_or_dummy(o3, ins, specs, aliases, 0)
    _alias_or_dummy(st_all, ins, specs, aliases, 1)
    assert st_all is not None or layer == 0
    slots = 1 if st_all is not None else n_layers

    out = pl.pallas_call(
        functools.partial(_gla_kernel, mode=mode, nseq=nseq, seq_len=seq_len, hb=hb, dk=dk, dv=dv,
                          nchunks=nchunks, unroll=unroll),
        grid=(nrb, heads // hb, nchunks),
        in_specs=specs,
        out_specs=[pl.BlockSpec((hb * dvb, r, LANE),
                                lambda i, hg, c: (out_off // (hb * dvb) + hg, rb0 + i * nchunks + c, 0)),
                   pl.BlockSpec((slots, nseq, hb, dk, dv), lambda i, hg, c: (layer, i, hg, 0, 0))],
        out_shape=[jax.ShapeDtypeStruct((32, n_rows, LANE), F32),
                   jax.ShapeDtypeStruct((n_layers, nbatch, heads, dk, dv), F32)],
        scratch_shapes=[pltpu.VMEM((nseq, hb, dk, dv), F32),
                        pltpu.VMEM((hb, r, dk), F32),
                        pltpu.VMEM((hb, r, dk), F32),
                        pltpu.VMEM((hb, r, dk), F32),
                        pltpu.VMEM((hb, dk, LANE), F32),
                        pltpu.VMEM((hb, r, r), F32)],
        input_output_aliases=aliases,
        compiler_params=_params(3),
        name=mode + ("_seq" if nchunks > 1 else "_step"),
    )(*ins)
    return out[0], out[1]


def _conv_block(ref, i, cbi, cw_ref, cb_ref, tail, ext, xcs, nseq, seq_len, act):
    w = cw_ref[cbi]
    b = cb_ref[cbi]
    for s in range(nseq):
        ext[0:8, :] = tail[s, cbi]
        ext[8:8 + seq_len, :] = ref[i, s * seq_len:(s + 1) * seq_len, :]
        y = (b + w[3:4] * ext[8:8 + seq_len, :] + w[2:3] * ext[7:7 + seq_len, :]
             + w[1:2] * ext[6:6 + seq_len, :] + w[0:1] * ext[5:5 + seq_len, :])
        xcs[cbi, s * seq_len:(s + 1) * seq_len, :] = act(y)
        tail[s, cbi] = ext[seq_len:seq_len + 8, :]


def _ssd_kernel(z_ref, xa_ref, xb_ref, xc_ref, dt_ref, cw_ref, cb_ref, dtb_ref, alog_ref, dsk_ref,
                nw_ref, t_ref, tt_ref, cbuf0_ref, h0_ref, _o3_any, _h_any, o_ref, hout_ref, cbout_ref,
                h_scr, tail, xcs, ext, *, nseq, seq_len, nchunks):
    r = nseq * seq_len
    c = pl.program_id(1)

    @pl.when(c == 0)
    def _():
        h_scr[...] = h0_ref[0]
        tail[...] = cbuf0_ref[...]

    for part, ref in enumerate((xa_ref, xb_ref, xc_ref)):
        def body(i, carry, part=part, ref=ref):
            _conv_block(ref, i, part * 8 + i, cw_ref, cb_ref, tail, ext, xcs, nseq, seq_len,
                        jax.nn.silu)
            return carry
        lax.fori_loop(0, 8, body, 0)

    lane = lax.broadcasted_iota(jnp.int32, (r, LANE), 1)
    lo = lane < SSM_HEADDIM
    row_lo = lax.broadcasted_iota(jnp.int32, (LANE, LANE), 0) < SSM_HEADDIM
    seq_of_row = lax.broadcasted_iota(jnp.int32, (r, LANE), 0) >> int(math.log2(seq_len))
    tmat = t_ref[...]
    ttmat = tt_ref[...]
    causal = tmat.astype(F32) > 0.0

    def bcast_col(arr, j):
        return jnp.broadcast_to(arr[:, j:j + 1], (r, LANE))

    def group(g, carry):
        dt = _softplus(dt_ref[g] + dtb_ref[g])
        dta = dt * (-jnp.exp(alog_ref[g]))
        d3 = _split3(dta)
        cum = _sum3(jnp.dot(tmat, d3, preferred_element_type=F32), LANE)
        ct3 = lax.dot_general(d3, ttmat, _TN, preferred_element_type=F32)
        cum_t = ct3[0:LANE] + ct3[LANE:2 * LANE] + ct3[2 * LANE:3 * LANE]
        b_g = xcs[16 + g]
        c_g = xcs[20 + g]
        b_b = b_g.astype(BF16)
        cb_m = lax.dot_general(c_g.astype(BF16), b_b, _NT, preferred_element_type=F32)

        def decay_mat(j):
            rel = (jnp.broadcast_to(cum[:, j:j + 1], (r, r))
                   - jnp.broadcast_to(cum_t[j:j + 1, :], (r, r)))
            dec = jnp.where(causal, jnp.exp(jnp.where(causal, rel, 0.0)), 0.0)
            return (cb_m * dec).astype(BF16)

        ys = []
        for jj in range(4):
            j0, j1 = 2 * jj, 2 * jj + 1
            cbi = g * 4 + jj
            x_cb = xcs[cbi]
            dt_e = jnp.where(lo, bcast_col(dt, j0), bcast_col(dt, j1))
            cum_e = jnp.where(lo, bcast_col(cum, j0), bcast_col(cum, j1))
            u = x_cb * dt_e
            u_b = u.astype(BF16)
            y = jnp.where(lo,
                          jnp.dot(decay_mat(j0), u_b, preferred_element_type=F32),
                          jnp.dot(decay_mat(j1), u_b, preferred_element_type=F32))
            y_in = jnp.zeros((r, LANE), F32)
            for s in range(nseq):
                rl = s * seq_len + seq_len - 1
                h_cb = h_scr[s, cbi]
                if nseq == 1:
                    c_s = c_g
                    rel = cum_e[rl:rl + 1, :] - cum_e
                    uw = u * jnp.exp(rel)
                else:
                    in_seq = seq_of_row == s
                    c_s = jnp.where(in_seq, c_g, 0.0)
                    rel = jnp.where(in_seq, cum_e[rl:rl + 1, :] - cum_e, 0.0)
                    uw = jnp.where(in_seq, u * jnp.exp(rel), 0.0)
                y_in = y_in + lax.dot_general(c_s.astype(BF16), h_cb.astype(BF16), _NT,
                                              preferred_element_type=F32)
                last = jnp.where(row_lo,
                                 jnp.broadcast_to(cum[rl:rl + 1, j0:j0 + 1], (LANE, LANE)),
                                 jnp.broadcast_to(cum[rl:rl + 1, j1:j1 + 1], (LANE, LANE)))
                h_scr[s, cbi] = jnp.exp(last) * h_cb + lax.dot_general(
                    uw.astype(BF16), b_b, _TN, preferred_element_type=F32)
            y = y + y_in * jnp.exp(cum_e) + dsk_ref[cbi] * x_cb
            ys.append(y * jax.nn.silu(z_ref[cbi]))
        ms = sum(jnp.sum(y * y, axis=1, keepdims=True) for y in ys) * (1.0 / (4 * LANE))
        rinv = lax.rsqrt(ms + EPS)
        for jj in range(4):
            o_ref[g * 4 + jj] = ys[jj] * rinv * nw_ref[g * 4 + jj]
        return carry

    lax.fori_loop(0, SSM_GROUPS, group, 0, unroll=True)

    @pl.when(c == nchunks - 1)
    def _():
        hout_ref[0] = h_scr[...]
        for other in range(1, hout_ref.shape[0]):
            hout_ref[other] = jnp.zeros(hout_ref.shape[1:], F32)
        cbout_ref[...] = tail[...]


def _tri_consts(nseq, seq_len):
    r = nseq * seq_len
    t = np.arange(r)[:, None]
    s = np.arange(r)[None, :]
    m = (((t // seq_len) == (s // seq_len)) & (s <= t)).astype(np.float32)
    return jnp.asarray(m, BF16), jnp.asarray(m.T, BF16)


def ssd_mixer(p3, o3, h0, h0_layer, h_all, layer, n_layers, cbuf0, params, *, row0, nbatch, t_len,
              nseq):
    seq_len = min(CHUNK, t_len)
    nchunks = t_len // seq_len
    r = nseq * seq_len
    nrb = nbatch // nseq
    rb0 = row0 // r
    cw, cbias, dtb, alog, dsk, nw = params
    tmat, ttmat = _tri_consts(nseq, seq_len)
    n_rows = p3.shape[1]

    def slab(nblk, blk_idx):
        return pl.BlockSpec((nblk, r, LANE), lambda i, c: (blk_idx, rb0 + i * nchunks + c, 0))

    def const(arr):
        nd = arr.ndim
        return pl.BlockSpec(arr.shape, lambda i, c: (0,) * nd)

    ins = [p3, p3, p3, p3, p3, cw, cbias, dtb, alog, dsk, nw, tmat, ttmat, cbuf0, h0]
    specs = [slab(16, 4), slab(8, 10), slab(8, 11), slab(8, 12), slab(4, 26),
             const(cw), const(cbias), const(dtb), const(alog), const(dsk), const(nw),
             const(tmat), const(ttmat),
             pl.BlockSpec((nseq, 24, 8, LANE), lambda i, c: (i, 0, 0, 0)),
             pl.BlockSpec((1, nseq, 16, LANE, LANE), lambda i, c: (h0_layer, i, 0, 0, 0))]
    aliases = {}
    _alias_or_dummy(o3, ins, specs, aliases, 0)
    _alias_or_dummy(h_all, ins, specs, aliases, 1)
    assert h_all is not None or layer == 0
    slots = 1 if h_all is not None else n_layers
    out = pl.pallas_call(
        functools.partial(_ssd_kernel, nseq=nseq, seq_len=seq_len, nchunks=nchunks),
        grid=(nrb, nchunks),
        in_specs=specs,
        out_specs=[pl.BlockSpec((16, r, LANE), lambda i, c: (1, rb0 + i * nchunks + c, 0)),
                   pl.BlockSpec((slots, nseq, 16, LANE, LANE), lambda i, c: (layer, i, 0, 0, 0)),
                   pl.BlockSpec((nseq, 24, 8, LANE), lambda i, c: (i, 0, 0, 0))],
        out_shape=[jax.ShapeDtypeStruct((32, n_rows, LANE), F32),
                   jax.ShapeDtypeStruct((n_layers, nbatch, 16, LANE, LANE), F32),
                   jax.ShapeDtypeStruct((nbatch, 24, 8, LANE), F32)],
        scratch_shapes=[pltpu.VMEM((nseq, 16, LANE, LANE), F32),
                        pltpu.VMEM((nseq, 24, 8, LANE), F32),
                        pltpu.VMEM((24, r, LANE), F32),
                        pltpu.VMEM((seq_len + 8, LANE), F32)],
        input_output_aliases=aliases,
        compiler_params=_params(2),
        name="ssd_seq" if nchunks > 1 else "ssd_step",
    )(*ins)
    return out[0], out[1], out[2]


def _lru_kernel(x_ref, y_ref, cw_ref, cb_ref, wa_ref, ba_ref, wx_ref, bx_ref, lam_ref, cbuf0_ref,
                h0_ref, _, o_ref, hout_ref, cbout_ref, h_scr, tail, xcs, ext,
                *, nseq, seq_len, nchunks):
    r = nseq * seq_len
    sub = min(seq_len, CHUNK)
    c = pl.program_id(2)

    @pl.when(c == 0)
    def _():
        h_scr[...] = h0_ref[0]
        tail[...] = cbuf0_ref[...]

    _conv_block(x_ref, 0, 0, cw_ref, cb_ref, tail, ext, xcs, nseq, seq_len, lambda v: v)
    xc = xcs[0]
    xc_b = xc.astype(BF16)
    rg = jax.nn.sigmoid(jnp.dot(xc_b, wa_ref[0], preferred_element_type=F32) + ba_ref[0])
    ig = jax.nn.sigmoid(jnp.dot(xc_b, wx_ref[0], preferred_element_type=F32) + bx_ref[0])
    log_a = -LRU_C * rg * _softplus(-lam_ref[0])
    a_cum = jnp.exp(log_a)
    th = jnp.tanh(log_a)
    u_cum = jnp.sqrt(-2.0 * th / (1.0 - th)) * (ig * xc)

    pos = lax.broadcasted_iota(jnp.int32, (r, LANE), 0) & (sub - 1)
    d = 1
    while d < sub:
        take = pos >= d
        a_prev = pltpu.roll(a_cum, d, axis=0)
        u_prev = pltpu.roll(u_cum, d, axis=0)
        u_cum = jnp.where(take, a_cum * u_prev + u_cum, u_cum)
        a_cum = jnp.where(take, a_cum * a_prev, a_cum)
        d *= 2

    gel = jax.nn.gelu(y_ref[0])
    for s in range(nseq):
        h_prev = h_scr[s]
        for sb in range(seq_len // sub):
            r0 = s * seq_len + sb * sub
            hs = a_cum[r0:r0 + sub] * h_prev + u_cum[r0:r0 + sub]
            o_ref[0, r0:r0 + sub, :] = hs * gel[r0:r0 + sub]
            h_prev = hs[sub - 1:sub]
        h_scr[s] = h_prev

    @pl.when(c == nchunks - 1)
    def _():
        hout_ref[0] = h_scr[...]
        cbout_ref[...] = tail[...]


def lru_mixer(p3, o3, h0, cbuf0, params, *, row0, nbatch, t_len, nseq, rows):
    seq_len = min(rows, t_len)
    nchunks = t_len // seq_len
    r = nseq * seq_len
    nrb = nbatch // nseq
    rb0 = row0 // r
    cw, cbias, wa, ba, wx, bx, lam = params
    n_rows = p3.shape[1]

    def slab(off):
        return pl.BlockSpec((1, r, LANE), lambda i, n, c: (off + n, rb0 + i * nchunks + c, 0))

    def per_blk(arr):
        shp = (1,) + arr.shape[1:]
        nd = arr.ndim
        return pl.BlockSpec(shp, lambda i, n, c: (n,) + (0,) * (nd - 1))

    ins = [p3, p3, cw, cbias, wa, ba, wx, bx, lam, cbuf0, h0, o3]
    specs = [slab(48), slab(64), per_blk(cw), per_blk(cbias), per_blk(wa), per_blk(ba),
             per_blk(wx), per_blk(bx), per_blk(lam),
             pl.BlockSpec((nseq, 1, 8, LANE), lambda i, n, c: (i, n, 0, 0)),
             pl.BlockSpec((1, nseq, 1, LANE), lambda i, n, c: (n, i, 0, 0)),
             pl.BlockSpec(memory_space=pl.ANY)]
    out = pl.pallas_call(
        functools.partial(_lru_kernel, nseq=nseq, seq_len=seq_len, nchunks=nchunks),
        grid=(nrb, 16, nchunks),
        in_specs=specs,
        out_specs=[pl.BlockSpec((1, r, LANE), lambda i, n, c: (16 + n, rb0 + i * nchunks + c, 0)),
                   pl.BlockSpec((1, nseq, 1, LANE), lambda i, n, c: (n, i, 0, 0)),
                   pl.BlockSpec((nseq, 1, 8, LANE), lambda i, n, c: (i, n, 0, 0))],
        out_shape=[jax.ShapeDtypeStruct((32, n_rows, LANE), F32),
                   jax.ShapeDtypeStruct((16, nbatch, 1, LANE), F32),
                   jax.ShapeDtypeStruct((nbatch, 16, 8, LANE), F32)],
        scratch_shapes=[pltpu.VMEM((nseq, 1, LANE), F32),
                        pltpu.VMEM((nseq, 1, 8, LANE), F32),
                        pltpu.VMEM((1, r, LANE), F32),
                        pltpu.VMEM((seq_len + 8, LANE), F32)],
        input_output_aliases={len(ins) - 1: 0},
        compiler_params=_params(3),
        name="lru_seq" if nchunks > 1 else "lru_step",
    )(*ins)
    return out[0], out[1], out[2]


def _slabs(v, nblk):
    return v.astype(F32).reshape(nblk, 1, LANE)


def _pad_lanes(a, width=LANE):
    return jnp.pad(a, [(0, 0)] * (a.ndim - 1) + [(0, width - a.shape[-1])])


def _conv_state_in(buf, nblk):
    b = buf.shape[0]
    t = jnp.transpose(buf.astype(F32).reshape(b, 3, nblk, LANE), (0, 2, 1, 3))
    return jnp.pad(t, ((0, 0), (0, 0), (5, 0), (0, 0)))


def _conv_state_out(t):
    b, nblk = t.shape[:2]
    return jnp.transpose(t[:, :, 5:8, :], (0, 2, 1, 3)).reshape(b, 3, nblk * LANE)


EVEN_MAIN = 13312
ODD_GLOW = (6144, 6160)
IN_TN = 1024


def _prep_w_in_even(w):
    nl, d, _ = w.shape
    dt = _pad_lanes(w[:, :, EVEN_MAIN:].reshape(nl, d, SSM_GROUPS, SSM_HPG))
    return w.astype(BF16), dt.reshape(nl, d, SSM_GROUPS * LANE).astype(BF16)


def _prep_w_in_odd(w):
    g0, g1 = ODD_GLOW
    main = jnp.concatenate([w[:, :, :g0], w[:, :, g1:]], axis=2).astype(BF16)
    return main, _pad_lanes(w[:, :, g0:g1]).astype(BF16)


def kernel(x_prompt, x_sample, state_hgrn, state_ssm, state_ssm_conv, state_gla, state_lru,
           state_lru_conv, norm_mix_pre, norm_mix_post, norm_ffn_pre, norm_ffn_post,
           w_in_even, w_out_even, hgrn_lb_logits, hgrn_norm_w,
           ssm_conv_w, ssm_conv_b, ssm_dt_bias, ssm_a_log, ssm_d, ssm_norm_w,
           w_in_odd, w_out_odd, gla_gate_w2, gla_gate_b, gla_norm_w,
           lru_conv_w, lru_conv_b, lru_wa, lru_ba, lru_wx, lru_bx, lru_lambda,
           ffn_w_gate, ffn_w_up, ffn_w_down):
    bp, tp, d = x_prompt.shape
    bs, ts, _ = x_sample.shape
    n_p, n_s = bp * tp, bs * ts
    depth = norm_mix_pre.shape[0]
    x = jnp.concatenate([x_prompt.reshape(n_p, d), x_sample.reshape(n_s, d)], axis=0)

    lb_soft = jax.nn.softmax(hgrn_lb_logits.astype(F32), axis=0)
    hgrn_lb = jnp.maximum(jnp.cumsum(lb_soft, axis=0) - lb_soft[0], 0.0)

    paths = (dict(row0=0, nbatch=bp, t_len=tp), dict(row0=n_p, nbatch=bs, t_len=ts))
    outs = {k: ([], []) for k in ("ssm_cb", "lru", "lru_cb")}
    big = {k: [None, None] for k in ("hg", "ssm", "gla")}
    n_even, n_odd = w_in_even.shape[0], w_in_odd.shape[0]

    we_main, we_dt = _prep_w_in_even(w_in_even)
    wo_main, wo_glow = _prep_w_in_odd(w_in_odd)
    w_out_e, w_out_o = w_out_even.astype(BF16), w_out_odd.astype(BF16)
    wg, wu, wd = ffn_w_gate.astype(BF16), ffn_w_up.astype(BF16), ffn_w_down.astype(BF16)
    st_hgrn = state_hgrn.astype(F32)
    st_ssm = state_ssm.astype(F32).reshape(n_even, bs, 16, LANE, LANE)
    st_gla = state_gla.astype(F32)
    zero_hg = jnp.zeros((1, bp) + state_hgrn.shape[2:], F32)
    zero_ssm = jnp.zeros((1, bp, 16, LANE, LANE), F32)
    zero_gla = jnp.zeros((1, bp) + state_gla.shape[2:], F32)
    o3_buf = jnp.zeros((32, n_p + n_s, LANE), F32)

    for l in range(depth):
        j = l // 2
        if l % 2 == 0:
            p3 = inproj(x, norm_mix_pre[l],
                        [(we_main, j, EVEN_MAIN // IN_TN, IN_TN), (we_dt, j, 1, SSM_GROUPS * LANE)],
                        tn=IN_TN)
            hg_params = (_slabs(hgrn_lb[j], 16), _slabs(hgrn_norm_w[j], 1))
            dsk = jnp.repeat(ssm_d[j].astype(F32), SSM_HEADDIM)
            ssd_params = (
                jnp.transpose(ssm_conv_w[j].astype(F32).reshape(4, 24, LANE), (1, 0, 2)),
                _slabs(ssm_conv_b[j], 24),
                _pad_lanes(ssm_dt_bias[j].astype(F32).reshape(SSM_GROUPS, 1, SSM_HPG)),
                _pad_lanes(ssm_a_log[j].astype(F32).reshape(SSM_GROUPS, 1, SSM_HPG)),
                _slabs(dsk, 16), _slabs(ssm_norm_w[j], 16))
            o3 = o3_buf
            for pi, path in enumerate(paths):
                nb = path["nbatch"]
                if pi == 0:
                    s_hg, s_ssm, s_layer = zero_hg, zero_ssm, 0
                    s_cb = jnp.zeros((nb, 24, 8, LANE), F32)
                    nseq_a, hb_a, unroll_a, nseq_b = 1, 16, 16, 1
                else:
                    s_hg, s_ssm, s_layer = st_hgrn, st_ssm, j
                    s_cb = _conv_state_in(state_ssm_conv[j], 24)
                    nseq_a, hb_a, unroll_a, nseq_b = 8, 8, 8, 4
                o3, big["hg"][pi] = gla_mixer(
                    "hgrn", p3, o3, s_hg, s_layer, big["hg"][pi], j, n_even, hg_params,
                    nseq=nseq_a, hb=hb_a, heads=16, dk=128, dv=128, offs=(0, 16, 32, 48), out_off=0,
                    unroll=unroll_a, **path)
                o3, big["ssm"][pi], cb_b = ssd_mixer(p3, o3, s_ssm, s_layer, big["ssm"][pi], j,
                                                     n_even, s_cb, ssd_params, nseq=nseq_b, **path)
                outs["ssm_cb"][pi].append(_conv_state_out(cb_b).astype(state_ssm_conv.dtype))
            x = outproj(o3, w_out_e, j, x, norm_mix_post[l])
            o3_buf = o3
        else:
            p3 = inproj(x, norm_mix_pre[l],
                        [(wo_main, j, wo_main.shape[2] // IN_TN, IN_TN), (wo_glow, j, 1, LANE)],
                        tn=IN_TN)
            w2 = jnp.pad(gla_gate_w2[j], ((0, LANE - gla_gate_w2.shape[1]), (0, 0)))
            w2 = jnp.transpose(w2.reshape(LANE, 4, 256), (1, 0, 2)).astype(BF16)
            gla_params = (w2, _slabs(gla_gate_b[j], 8), _slabs(gla_norm_w[j], 4))
            lru_params = (
                jnp.transpose(lru_conv_w[j].astype(F32).reshape(4, 16, LANE), (1, 0, 2)),
                _slabs(lru_conv_b[j], 16), lru_wa[j].astype(BF16), _slabs(lru_ba[j], 16),
                lru_wx[j].astype(BF16), _slabs(lru_bx[j], 16), _slabs(lru_lambda[j], 16))
            o3 = o3_buf
            for pi, path in enumerate(paths):
                nb = path["nbatch"]
                if pi == 0:
                    s_gla, s_layer = zero_gla, 0
                    s_lru = jnp.zeros((16, nb, 1, LANE), F32)
                    s_cb = jnp.zeros((nb, 16, 8, LANE), F32)
                    nseq_c, hb_c, nseq_d, rows_d = 1, 4, 1, 512
                else:
                    s_gla, s_layer = st_gla, j
                    s_lru = jnp.transpose(state_lru[j].astype(F32).reshape(nb, 16, 1, LANE),
                                          (1, 0, 2, 3))
                    s_cb = _conv_state_in(state_lru_conv[j], 16)
                    nseq_c, hb_c, nseq_d, rows_d = 8, 1, 16, 8
                o3, big["gla"][pi] = gla_mixer(
                    "gla", p3, o3, s_gla, s_layer, big["gla"][pi], j, n_odd, gla_params,
                    nseq=nseq_c, hb=hb_c, heads=4, dk=256, dv=512, offs=(0, 8, 16, 32, 80),
                    out_off=0, unroll=min(4, hb_c), **path)
                o3, st_d, cb_d = lru_mixer(p3, o3, s_lru, s_cb, lru_params, nseq=nseq_d,
                                           rows=rows_d, **path)
                outs["lru"][pi].append(jnp.transpose(st_d, (1, 0, 2, 3)).reshape(nb, 16 * LANE)
                                       .astype(state_lru.dtype))
                outs["lru_cb"][pi].append(_conv_state_out(cb_d).astype(state_lru_conv.dtype))
            x = outproj(o3, w_out_o, j, x, norm_mix_post[l])
            o3_buf = o3
        x = ffn(x, norm_ffn_pre[l], wg, wu, wd, l, norm_ffn_post[l])

    y_prompt = x[:n_p].reshape(bp, tp, d).astype(x_prompt.dtype)
    y_sample = x[n_p:].reshape(bs, ts, d).astype(x_sample.dtype)
    ssm_shape = state_ssm.shape[2:]
    res = [y_prompt, y_sample]
    stacked = lambda key, pi: jnp.stack(outs[key][pi])
    res += [big["hg"][0].astype(state_hgrn.dtype), big["hg"][1].astype(state_hgrn.dtype),
            big["ssm"][0].reshape((n_even, bp) + ssm_shape).astype(state_ssm.dtype),
            big["ssm"][1].reshape((n_even, bs) + ssm_shape).astype(state_ssm.dtype),
            stacked("ssm_cb", 0), stacked("ssm_cb", 1),
            big["gla"][0].astype(state_gla.dtype), big["gla"][1].astype(state_gla.dtype),
            stacked("lru", 0), stacked("lru", 1), stacked("lru_cb", 0), stacked("lru_cb", 1)]
    return tuple(res)
```

```python
import functools
import math

import numpy as np
import jax
import jax.numpy as jnp
from jax import lax
from jax.experimental import pallas as pl
from jax.experimental.pallas import tpu as pltpu

F32 = jnp.float32
BF16 = jnp.bfloat16
LANE = 128
VMEM_LIMIT = 56 * 1024 * 1024

DOWN_STRIP = 512

EPS = 1e-6
F_MIN = 1e-30
CHUNK = 64
GLA_GATE_NORMALIZER = 16.0
LRU_C = 8.0
SSM_GROUPS = 4
SSM_HPG = 8
SSM_HEADDIM = 64


def _params(n_axes):
    return pltpu.CompilerParams(dimension_semantics=("arbitrary",) * n_axes,
                                vmem_limit_bytes=VMEM_LIMIT)


def _softplus(x):
    return jnp.maximum(x, 0.0) + jnp.log1p(jnp.exp(-jnp.abs(x)))


def _log_sigmoid(x):
    return jnp.minimum(x, 0.0) - jnp.log1p(jnp.exp(-jnp.abs(x)))


def _split3(x):
    hi = x.astype(BF16)
    r1 = x - hi.astype(F32)
    mid = r1.astype(BF16)
    lo = (r1 - mid.astype(F32)).astype(BF16)
    return jnp.concatenate([hi, mid, lo], axis=1)


def _sum3(y, w):
    return y[:, :w] + y[:, w:2 * w] + y[:, 2 * w:3 * w]


def _cat(ref, base, n):
    if n == 1:
        return ref[base]
    return jnp.concatenate([ref[base + j] for j in range(n)], axis=1)


_NT = (((1,), (1,)), ((), ()))
_TN = (((0,), (0,)), ((), ()))


def _prenorm_rows(x_ref, w_ref, h_scr, tm, rows=64):
    w = w_ref[...]

    def body(i, c):
        r0 = pl.multiple_of(i * rows, rows)
        x = x_ref[pl.ds(r0, rows), :]
        ms = jnp.mean(x * x, axis=1, keepdims=True)
        h_scr[pl.ds(r0, rows), :] = (x * lax.rsqrt(ms + EPS) * w).astype(BF16)
        return c

    lax.fori_loop(0, tm // rows, body, 0)


def _postnorm_rows(o_ref, x_ref, w_ref, tm, rows=64):
    w = w_ref[...]

    def body(i, c):
        r0 = pl.multiple_of(i * rows, rows)
        y = o_ref[pl.ds(r0, rows), :]
        ms = jnp.mean(y * y, axis=1, keepdims=True)
        o_ref[pl.ds(r0, rows), :] = x_ref[pl.ds(r0, rows), :] + y * lax.rsqrt(ms + EPS) * w
        return c

    lax.fori_loop(0, tm // rows, body, 0)


def _resident(shape, index_map):
    return pl.BlockSpec(shape, index_map, pipeline_mode=pl.Buffered(1))


def _inproj_kernel(x_ref, nw_ref, *rest, tm, segs):
    w_refs, o_ref, h_scr = rest[:len(segs)], rest[len(segs)], rest[len(segs) + 1]
    j = pl.program_id(1)

    @pl.when(j == 0)
    def _():
        _prenorm_rows(x_ref, nw_ref, h_scr, tm)

    for w_ref, (start, ntiles, width) in zip(w_refs, segs):
        @pl.when((j >= start) & (j < start + ntiles))
        def _(w_ref=w_ref, width=width):
            r = jnp.dot(h_scr[...], w_ref[0], preferred_element_type=F32)
            for c in range(width // LANE):
                o_ref[c] = r[:, c * LANE:(c + 1) * LANE]
            for c in range(width // LANE, o_ref.shape[0]):
                o_ref[c] = jnp.zeros(o_ref.shape[1:], F32)


def inproj(x, norm_w, segments, *, tm=512, tn=1024):
    n, d = x.shape
    cb = tn // LANE
    segs, specs, start = [], [], 0
    for w, layer, ntiles, width in segments:
        segs.append((start, ntiles, width))
        specs.append(pl.BlockSpec(
            (1, d, width),
            lambda i, j, layer=layer, start=start, ntiles=ntiles: (layer, 0, jnp.clip(j - start, 0, ntiles - 1))))
        start += ntiles
    return pl.pallas_call(
        functools.partial(_inproj_kernel, tm=tm, segs=tuple(segs)),
        grid=(n // tm, start),
        in_specs=[pl.BlockSpec((tm, d), lambda i, j: (i, 0)),
                  pl.BlockSpec((1, d), lambda i, j: (0, 0))] + specs,
        out_specs=pl.BlockSpec((cb, tm, LANE), lambda i, j: (j, i, 0)),
        out_shape=jax.ShapeDtypeStruct((start * cb, n, LANE), F32),
        scratch_shapes=[pltpu.VMEM((tm, d), BF16)],
        compiler_params=_params(2),
        name="inproj",
    )(x, norm_w.reshape(1, d), *[s[0] for s in segments])


def _outproj_kernel(a_ref, w3_ref, x_ref, nw_ref, o_ref, *, tm, kb, nk):
    k = pl.program_id(1)
    w_ref = w3_ref.at[0]
    a = _cat(a_ref, 0, kb).astype(BF16)

    @pl.when(k == 0)
    def _():
        o_ref[...] = jnp.zeros_like(o_ref)

    for n0 in range(0, o_ref.shape[1], DOWN_STRIP):
        o_ref[:, n0:n0 + DOWN_STRIP] += jnp.dot(a, w_ref[:, n0:n0 + DOWN_STRIP],
                                                preferred_element_type=F32)

    @pl.when(k == nk - 1)
    def _():
        _postnorm_rows(o_ref, x_ref, nw_ref, tm)


def outproj(a3, w, layer, x, norm_w, *, tm=512, tk=1024):
    n, d = x.shape
    kdim = w.shape[1]
    kb = tk // LANE
    nk = kdim // tk
    return pl.pallas_call(
        functools.partial(_outproj_kernel, tm=tm, kb=kb, nk=nk),
        grid=(n // tm, nk),
        in_specs=[pl.BlockSpec((kb, tm, LANE), lambda i, k: (k, i, 0)),
                  pl.BlockSpec((1, tk, d), lambda i, k: (layer, k, 0)),
                  pl.BlockSpec((tm, d), lambda i, k: (i, 0)),
                  pl.BlockSpec((1, d), lambda i, k: (0, 0))],
        out_specs=pl.BlockSpec((tm, d), lambda i, k: (i, 0)),
        out_shape=jax.ShapeDtypeStruct((n, d), F32),
        compiler_params=_params(2),
        name="outproj",
    )(a3, w, x, norm_w.reshape(1, d))


def _ffn_kernel(x_ref, pre_ref, wg_ref, wu_ref, wd3_ref, post_ref, o_ref, h_scr, a_scr, *, tm, nf):
    f = pl.program_id(1)
    wd_ref = wd3_ref.at[0]

    @pl.when(f == 0)
    def _():
        _prenorm_rows(x_ref, pre_ref, h_scr, tm)
        o_ref[...] = jnp.zeros_like(o_ref)
        a_scr[...] = jnp.zeros_like(a_scr)

    a_prev = a_scr[...]
    h = h_scr[...]
    g = jnp.dot(h, wg_ref[0], preferred_element_type=F32)
    u = jnp.dot(h, wu_ref[0], preferred_element_type=F32)
    for n0 in range(0, o_ref.shape[1], DOWN_STRIP):
        o_ref[:, n0:n0 + DOWN_STRIP] += jnp.dot(a_prev, wd_ref[:, n0:n0 + DOWN_STRIP],
                                                preferred_element_type=F32)
    a_scr[...] = (jax.nn.silu(g) * u).astype(BF16)

    @pl.when(f == nf)
    def _():
        _postnorm_rows(o_ref, x_ref, post_ref, tm)


def ffn(x, pre_w, wg, wu, wd, layer, post_w, *, tm=512, tf=256):
    n, d = x.shape
    dff = wg.shape[2]
    nf = dff // tf
    return pl.pallas_call(
        functools.partial(_ffn_kernel, tm=tm, nf=nf),
        grid=(n // tm, nf + 1),
        in_specs=[pl.BlockSpec((tm, d), lambda i, f: (i, 0)),
                  pl.BlockSpec((1, d), lambda i, f: (0, 0)),
                  pl.BlockSpec((1, d, tf), lambda i, f: (layer, 0, jnp.minimum(f, nf - 1))),
                  pl.BlockSpec((1, d, tf), lambda i, f: (layer, 0, jnp.minimum(f, nf - 1))),
                  pl.BlockSpec((1, tf, d), lambda i, f: (layer, jnp.maximum(f - 1, 0), 0)),
                  pl.BlockSpec((1, d), lambda i, f: (0, 0))],
        out_specs=pl.BlockSpec((tm, d), lambda i, f: (i, 0)),
        out_shape=jax.ShapeDtypeStruct((n, d), F32),
        scratch_shapes=[pltpu.VMEM((tm, d), BF16), pltpu.VMEM((tm, tf), BF16)],
        compiler_params=_params(2),
        name="ffn",
    )(x, pre_w.reshape(1, d), wg, wu, wd, post_w.reshape(1, d))


def _gla_consts(nseq, seq_len):
    r = nseq * seq_len
    t = np.arange(r)[:, None]
    s = np.arange(r)[None, :]
    prefix = ((t // seq_len) == (s // seq_len)) & (s <= t)
    masks = []
    c = seq_len // 2
    while c >= 1:
        right = (t % (2 * c)) >= c
        masks.append(((t // (2 * c)) == (s // (2 * c))) & right & ((s % (2 * c)) < c))
        c //= 2
    m = np.stack(masks).astype(np.float32)
    return jnp.asarray(prefix.astype(np.float32), BF16), jnp.asarray(m, F32)


def _block_row(b, blk, pick, pos):
    r, w = b.shape
    if blk >= 8:
        return jnp.concatenate(
            [jnp.broadcast_to(b[s + pick:s + pick + 1, :], (blk, w)) for s in range(0, r, blk)], axis=0)
    out = b
    for p in range(blk):
        if p != pick:
            out = jnp.where(pos == p, pltpu.roll(b, (p - pick) % r, axis=0), out)
    return out


def _gla_kernel(*refs, mode, nseq, seq_len, hb, dk, dv, nchunks, unroll):
    s_scr, q_scr, k_scr, b_scr, dc_scr, sc_scr = refs[-6:]
    if mode == "hgrn":
        (q_ref, f_ref, v_ref, gt_ref, lb_ref, nw_ref, a_ref, m_ref, s0_ref, _, _,
         o_ref, so_ref) = refs[:-6]
    else:
        (q_ref, k_ref, v_ref, gt_ref, gl_ref, w2_ref, gb_ref, nw_ref, a_ref, m_ref, s0_ref, _, _,
         o_ref, so_ref) = refs[:-6]
    dkb, dvb = dk // LANE, dv // LANE
    r = nseq * seq_len
    nlev = int(math.log2(seq_len))
    c = pl.program_id(2)

    @pl.when(c == 0)
    def _():
        s_scr[...] = s0_ref[0]

    a_mat = a_ref[...]
    seq_shift = int(math.log2(seq_len))
    rowk = lax.broadcasted_iota(jnp.int32, (r, dk), 0)
    colk = lax.broadcasted_iota(jnp.int32, (dk, r), 1)
    seq_of_row = lax.broadcasted_iota(jnp.int32, (r, LANE), 0) >> seq_shift
    lane_id = lax.broadcasted_iota(jnp.int32, (r, LANE), 1)
    seqsel = (seq_of_row == lane_id).astype(BF16)
    nw = _cat(nw_ref, 0, dvb)

    def prepare(h, carry):
        qr = _cat(q_ref, h * dkb, dkb)
        if mode == "hgrn":
            lb = _cat(lb_ref, h * dkb, dkb)
            fg = lb + (1.0 - lb) * jax.nn.sigmoid(_cat(f_ref, h * dkb, dkb))
            g = jnp.log(jnp.maximum(fg, F_MIN))
            k = 1.0 - fg
            q = jax.nn.silu(qr) * (dk ** -0.5)
        else:
            k = _cat(k_ref, h * dkb, dkb)
            q = qr * (dk ** -0.5)
            lin = jnp.dot(gl_ref[0].astype(BF16), w2_ref[h], preferred_element_type=F32)
            g = _log_sigmoid(lin + _cat(gb_ref, h * dkb, dkb)) / GLA_GATE_NORMALIZER

        g3 = _split3(g)
        q_scr[h] = q
        k_scr[h] = k
        b_scr[h] = _sum3(jnp.dot(a_mat, g3, preferred_element_type=F32), dk)
        d3 = lax.dot_general(g3, seqsel, _TN, preferred_element_type=F32)
        dc_scr[h] = d3[0:dk] + d3[dk:2 * dk] + d3[2 * dk:3 * dk]
        return carry

    def intra(h, carry):
        q, k, b = q_scr[h], k_scr[h], b_scr[h]
        scores = jnp.zeros((r, r), F32)
        for l in range(nlev):
            half = seq_len >> (l + 1)
            pos = rowk & (2 * half - 1)
            right = pos >= half
            b_m = _block_row(b, 2 * half, half - 1, pos)
            x = (jnp.where(right, q, k) * jnp.exp(-jnp.abs(b - b_m))).astype(BF16)
            scores = scores + m_ref[l] * lax.dot_general(x, x, _NT, preferred_element_type=F32)
        sc_scr[h] = scores
        return carry

    def combine(h, carry):
        q, k, b = q_scr[h], k_scr[h], b_scr[h]
        v = _cat(v_ref, h * dvb, dvb)
        gate = _cat(gt_ref, h * dvb, dvb)
        vb = v.astype(BF16)
        o = jnp.dot(sc_scr[h].astype(BF16), vb, preferred_element_type=F32)
        o = o + jnp.sum(q * k, axis=1, keepdims=True) * v

        qb = q * jnp.exp(b)
        kb = k * jnp.exp(_block_row(b, seq_len, seq_len - 1, None) - b)
        dcol = dc_scr[h]
        if nseq > 1:
            kb_t = kb.T
        for s in range(nseq):
            st = s_scr[s, h]
            if nseq == 1:
                qs = qb
                upd = lax.dot_general(kb.astype(BF16), vb, _TN, preferred_element_type=F32)
            else:
                qs = jnp.where((rowk >> seq_shift) == s, qb, 0.0)
                ks_t = jnp.where((colk >> seq_shift) == s, kb_t, 0.0)
                upd = jnp.dot(ks_t.astype(BF16), vb, preferred_element_type=F32)
            o = o + jnp.dot(qs.astype(BF16), st.astype(BF16), preferred_element_type=F32)
            dec = jnp.exp(jnp.broadcast_to(dcol[:, s:s + 1], (dk, dv)))
            s_scr[s, h] = dec * st + upd

        ms = jnp.mean(o * o, axis=1, keepdims=True)
        y = o * lax.rsqrt(ms + EPS) * nw * jax.nn.silu(gate)
        for j in range(dvb):
            o_ref[h * dvb + j] = y[:, j * LANE:(j + 1) * LANE]
        return carry

    for stage in (prepare, intra, combine):
        lax.fori_loop(0, hb, stage, 0, unroll=unroll)

    @pl.when(c == nchunks - 1)
    def _():
        so_ref[0] = s_scr[...]
        for other in range(1, so_ref.shape[0]):
            so_ref[other] = jnp.zeros(so_ref.shape[1:], F32)


def _alias_or_dummy(arr, ins, specs, aliases, out_idx):
    if arr is None:
        ins.append(jnp.zeros((8, LANE), F32))
    else:
        ins.append(arr)
        aliases[len(ins) - 1] = out_idx
    specs.append(pl.BlockSpec(memory_space=pl.ANY))


def gla_mixer(mode, p3, o3, s0, s0_layer, st_all, layer, n_layers, params, *, row0, nbatch, t_len,
              nseq, hb, heads, dk, dv, offs, out_off, unroll):
    seq_len = min(CHUNK, t_len)
    nchunks = t_len // seq_len
    r = nseq * seq_len
    dkb, dvb = dk // LANE, dv // LANE
    nrb = nbatch // nseq
    rb0 = row0 // r
    a_mat, masks = _gla_consts(nseq, seq_len)
    n_rows = p3.shape[1]

    def slab(nblk, off):
        return pl.BlockSpec((nblk, r, LANE),
                            lambda i, hg, c: (off // nblk + hg, rb0 + i * nchunks + c, 0))

    def const(arr):
        nd = arr.ndim
        return pl.BlockSpec(arr.shape, lambda i, hg, c: (0,) * nd)

    def per_head(arr, nblk):
        return pl.BlockSpec((nblk, 1, LANE), lambda i, hg, c: (hg, 0, 0))

    if mode == "hgrn":
        lb, nw = params
        ins = [p3, p3, p3, p3, lb, nw, a_mat, masks, s0]
        specs = [slab(hb * dkb, offs[0]), slab(hb * dkb, offs[1]), slab(hb * dvb, offs[2]),
                 slab(hb * dvb, offs[3]), per_head(lb, hb * dkb), const(nw), const(a_mat),
                 const(masks)]
    else:
        w2, gb, nw = params
        ins = [p3, p3, p3, p3, p3, w2, gb, nw, a_mat, masks, s0]
        specs = [slab(hb * dkb, offs[0]), slab(hb * dkb, offs[1]), slab(hb * dvb, offs[2]),
                 slab(hb * dvb, offs[3]),
                 pl.BlockSpec((1, r, LANE), lambda i, hg, c: (offs[4], rb0 + i * nchunks + c, 0)),
                 pl.BlockSpec((hb, LANE, dk), lambda i, hg, c: (hg, 0, 0)),
                 per_head(gb, hb * dkb), const(nw), const(a_mat), const(masks)]
    specs.append(pl.BlockSpec((1, nseq, hb, dk, dv), lambda i, hg, c: (s0_layer, i, hg, 0, 0)))
    aliases = {}
    _alias_or_dummy(o3, ins, specs, aliases, 0)
    _alias_or_dummy(st_all, ins, specs, aliases, 1)
    assert st_all is not None or layer == 0
    slots = 1 if st_all is not None else n_layers

    out = pl.pallas_call(
        functools.partial(_gla_kernel, mode=mode, nseq=nseq, seq_len=seq_len, hb=hb, dk=dk, dv=dv,
                          nchunks=nchunks, unroll=unroll),
        grid=(nrb, heads // hb, nchunks),
        in_specs=specs,
        out_specs=[pl.BlockSpec((hb * dvb, r, LANE),
                                lambda i, hg, c: (out_off // (hb * dvb) + hg, rb0 + i * nchunks + c, 0)),
                   pl.BlockSpec((slots, nseq, hb, dk, dv), lambda i, hg, c: (layer, i, hg, 0, 0))],
        out_shape=[jax.ShapeDtypeStruct((32, n_rows, LANE), F32),
                   jax.ShapeDtypeStruct((n_layers, nbatch, heads, dk, dv), F32)],
        scratch_shapes=[pltpu.VMEM((nseq, hb, dk, dv), F32),
                        pltpu.VMEM((hb, r, dk), F32),
                        pltpu.VMEM((hb, r, dk), F32),
                        pltpu.VMEM((hb, r, dk), F32),
                        pltpu.VMEM((hb, dk, LANE), F32),
                        pltpu.VMEM((hb, r, r), F32)],
        input_output_aliases=aliases,
        compiler_params=_params(3),
        name=mode + ("_seq" if nchunks > 1 else "_step"),
    )(*ins)
    return out[0], out[1]


def _conv_block(ref, i, cbi, cw_ref, cb_ref, tail, ext, xcs, nseq, seq_len, act):
    w = cw_ref[cbi]
    b = cb_ref[cbi]
    for s in range(nseq):
        ext[0:8, :] = tail[s, cbi]
        ext[8:8 + seq_len, :] = ref[i, s * seq_len:(s + 1) * seq_len, :]
        y = (b + w[3:4] * ext[8:8 + seq_len, :] + w[2:3] * ext[7:7 + seq_len, :]
             + w[1:2] * ext[6:6 + seq_len, :] + w[0:1] * ext[5:5 + seq_len, :])
        xcs[cbi, s * seq_len:(s + 1) * seq_len, :] = act(y)
        tail[s, cbi] = ext[seq_len:seq_len + 8, :]


def _ssd_kernel(z_ref, xa_ref, xb_ref, xc_ref, dt_ref, cw_ref, cb_ref, dtb_ref, alog_ref, dsk_ref,
                nw_ref, t_ref, tt_ref, cbuf0_ref, h0_ref, _o3_any, _h_any, o_ref, hout_ref, cbout_ref,
                h_scr, tail, xcs, ext, *, nseq, seq_len, nchunks):
    r = nseq * seq_len
    c = pl.program_id(1)

    @pl.when(c == 0)
    def _():
        h_scr[...] = h0_ref[0]
        tail[...] = cbuf0_ref[...]

    for part, ref in enumerate((xa_ref, xb_ref, xc_ref)):
        def body(i, carry, part=part, ref=ref):
            _conv_block(ref, i, part * 8 + i, cw_ref, cb_ref, tail, ext, xcs, nseq, seq_len,
                        jax.nn.silu)
            return carry
        lax.fori_loop(0, 8, body, 0, unroll=True)

    lane = lax.broadcasted_iota(jnp.int32, (r, LANE), 1)
    lo = lane < SSM_HEADDIM
    row_lo = lax.broadcasted_iota(jnp.int32, (LANE, LANE), 0) < SSM_HEADDIM
    seq_of_row = lax.broadcasted_iota(jnp.int32, (r, LANE), 0) >> int(math.log2(seq_len))
    tmat = t_ref[...]
    ttmat = tt_ref[...]
    causal = tmat.astype(F32) > 0.0

    def bcast_col(arr, j):
        return jnp.broadcast_to(arr[:, j:j + 1], (r, LANE))

    def group(g, carry):
        dt = _softplus(dt_ref[g] + dtb_ref[g])
        dta = dt * (-jnp.exp(alog_ref[g]))
        d3 = _split3(dta)
        cum = _sum3(jnp.dot(tmat, d3, preferred_element_type=F32), LANE)
        ct3 = lax.dot_general(d3, ttmat, _TN, preferred_element_type=F32)
        cum_t = ct3[0:LANE] + ct3[LANE:2 * LANE] + ct3[2 * LANE:3 * LANE]
        b_g = xcs[16 + g]
        c_g = xcs[20 + g]
        b_b = b_g.astype(BF16)
        cb_m = lax.dot_general(c_g.astype(BF16), b_b, _NT, preferred_element_type=F32)

        def decay_mat(j):
            rel = (jnp.broadcast_to(cum[:, j:j + 1], (r, r))
                   - jnp.broadcast_to(cum_t[j:j + 1, :], (r, r)))
            dec = jnp.where(causal, jnp.exp(jnp.where(causal, rel, 0.0)), 0.0)
            return (cb_m * dec).astype(BF16)

        ys = []
        for jj in range(4):
            j0, j1 = 2 * jj, 2 * jj + 1
            cbi = g * 4 + jj
            x_cb = xcs[cbi]
            dt_e = jnp.where(lo, bcast_col(dt, j0), bcast_col(dt, j1))
            cum_e = jnp.where(lo, bcast_col(cum, j0), bcast_col(cum, j1))
            u = x_cb * dt_e
            u_b = u.astype(BF16)
            y = jnp.where(lo,
                          jnp.dot(decay_mat(j0), u_b, preferred_element_type=F32),
                          jnp.dot(decay_mat(j1), u_b, preferred_element_type=F32))
            y_in = jnp.zeros((r, LANE), F32)
            for s in range(nseq):
                rl = s * seq_len + seq_len - 1
                h_cb = h_scr[s, cbi]
                if nseq == 1:
                    c_s = c_g
                    rel = cum_e[rl:rl + 1, :] - cum_e
                    uw = u * jnp.exp(rel)
                else:
                    in_seq = seq_of_row == s
                    c_s = jnp.where(in_seq, c_g, 0.0)
                    rel = jnp.where(in_seq, cum_e[rl:rl + 1, :] - cum_e, 0.0)
                    uw = jnp.where(in_seq, u * jnp.exp(rel), 0.0)
                y_in = y_in + lax.dot_general(c_s.astype(BF16), h_cb.astype(BF16), _NT,
                                              preferred_element_type=F32)
                last = jnp.where(row_lo,
                                 jnp.broadcast_to(cum[rl:rl + 1, j0:j0 + 1], (LANE, LANE)),
                                 jnp.broadcast_to(cum[rl:rl + 1, j1:j1 + 1], (LANE, LANE)))
                h_scr[s, cbi] = jnp.exp(last) * h_cb + lax.dot_general(
                    uw.astype(BF16), b_b, _TN, preferred_element_type=F32)
            y = y + y_in * jnp.exp(cum_e) + dsk_ref[cbi] * x_cb
            ys.append(y * jax.nn.silu(z_ref[cbi]))
        ms = sum(jnp.sum(y * y, axis=1, keepdims=True) for y in ys) * (1.0 / (4 * LANE))
        rinv = lax.rsqrt(ms + EPS)
        for jj in range(4):
            o_ref[g * 4 + jj] = ys[jj] * rinv * nw_ref[g * 4 + jj]
        return carry

    lax.fori_loop(0, SSM_GROUPS, group, 0, unroll=True)

    @pl.when(c == nchunks - 1)
    def _():
        hout_ref[0] = h_scr[...]
        for other in range(1, hout_ref.shape[0]):
            hout_ref[other] = jnp.zeros(hout_ref.shape[1:], F32)
        cbout_ref[...] = tail[...]


def _tri_consts(nseq, seq_len):
    r = nseq * seq_len
    t = np.arange(r)[:, None]
    s = np.arange(r)[None, :]
    m = (((t // seq_len) == (s // seq_len)) & (s <= t)).astype(np.float32)
    return jnp.asarray(m, BF16), jnp.asarray(m.T, BF16)


def ssd_mixer(p3, o3, h0, h0_layer, h_all, layer, n_layers, cbuf0, params, *, row0, nbatch, t_len,
              nseq):
    seq_len = min(CHUNK, t_len)
    nchunks = t_len // seq_len
    r = nseq * seq_len
    nrb = nbatch // nseq
    rb0 = row0 // r
    cw, cbias, dtb, alog, dsk, nw = params
    tmat, ttmat = _tri_consts(nseq, seq_len)
    n_rows = p3.shape[1]

    def slab(nblk, blk_idx):
        return pl.BlockSpec((nblk, r, LANE), lambda i, c: (blk_idx, rb0 + i * nchunks + c, 0))

    def const(arr):
        nd = arr.ndim
        return pl.BlockSpec(arr.shape, lambda i, c: (0,) * nd)

    ins = [p3, p3, p3, p3, p3, cw, cbias, dtb, alog, dsk, nw, tmat, ttmat, cbuf0, h0]
    specs = [slab(16, 4), slab(8, 10), slab(8, 11), slab(8, 12), slab(4, 26),
             const(cw), const(cbias), const(dtb), const(alog), const(dsk), const(nw),
             const(tmat), const(ttmat),
             pl.BlockSpec((nseq, 24, 8, LANE), lambda i, c: (i, 0, 0, 0)),
             pl.BlockSpec((1, nseq, 16, LANE, LANE), lambda i, c: (h0_layer, i, 0, 0, 0))]
    aliases = {}
    _alias_or_dummy(o3, ins, specs, aliases, 0)
    _alias_or_dummy(h_all, ins, specs, aliases, 1)
    assert h_all is not None or layer == 0
    slots = 1 if h_all is not None else n_layers
    out = pl.pallas_call(
        functools.partial(_ssd_kernel, nseq=nseq, seq_len=seq_len, nchunks=nchunks),
        grid=(nrb, nchunks),
        in_specs=specs,
        out_specs=[pl.BlockSpec((16, r, LANE), lambda i, c: (1, rb0 + i * nchunks + c, 0)),
                   pl.BlockSpec((slots, nseq, 16, LANE, LANE), lambda i, c: (layer, i, 0, 0, 0)),
                   pl.BlockSpec((nseq, 24, 8, LANE), lambda i, c: (i, 0, 0, 0))],
        out_shape=[jax.ShapeDtypeStruct((32, n_rows, LANE), F32),
                   jax.ShapeDtypeStruct((n_layers, nbatch, 16, LANE, LANE), F32),
                   jax.ShapeDtypeStruct((nbatch, 24, 8, LANE), F32)],
        scratch_shapes=[pltpu.VMEM((nseq, 16, LANE, LANE), F32),
                        pltpu.VMEM((nseq, 24, 8, LANE), F32),
                        pltpu.VMEM((24, r, LANE), F32),
                        pltpu.VMEM((seq_len + 8, LANE), F32)],
        input_output_aliases=aliases,
        compiler_params=_params(2),
        name="ssd_seq" if nchunks > 1 else "ssd_step",
    )(*ins)
    return out[0], out[1], out[2]


def _lru_kernel(x_ref, y_ref, cw_ref, cb_ref, wa_ref, ba_ref, wx_ref, bx_ref, lam_ref, cbuf0_ref,
                h0_ref, _, o_ref, hout_ref, cbout_ref, h_scr, tail, xcs, ext,
                *, nseq, seq_len, nchunks):
    r = nseq * seq_len
    sub = min(seq_len, CHUNK)
    c = pl.program_id(2)

    @pl.when(c == 0)
    def _():
        h_scr[...] = h0_ref[0]
        tail[...] = cbuf0_ref[...]

    _conv_block(x_ref, 0, 0, cw_ref, cb_ref, tail, ext, xcs, nseq, seq_len, lambda v: v)
    xc = xcs[0]
    xc_b = xc.astype(BF16)
    rg = jax.nn.sigmoid(jnp.dot(xc_b, wa_ref[0], preferred_element_type=F32) + ba_ref[0])
    ig = jax.nn.sigmoid(jnp.dot(xc_b, wx_ref[0], preferred_element_type=F32) + bx_ref[0])
    log_a = -LRU_C * rg * _softplus(-lam_ref[0])
    a_cum = jnp.exp(log_a)
    th = jnp.tanh(log_a)
    u_cum = jnp.sqrt(-2.0 * th / (1.0 - th)) * (ig * xc)

    pos = lax.broadcasted_iota(jnp.int32, (r, LANE), 0) & (sub - 1)
    d = 1
    while d < sub:
        take = pos >= d
        a_prev = pltpu.roll(a_cum, d, axis=0)
        u_prev = pltpu.roll(u_cum, d, axis=0)
        u_cum = jnp.where(take, a_cum * u_prev + u_cum, u_cum)
        a_cum = jnp.where(take, a_cum * a_prev, a_cum)
        d *= 2

    gel = jax.nn.gelu(y_ref[0])
    for s in range(nseq):
        h_prev = h_scr[s]
        for sb in range(seq_len // sub):
            r0 = s * seq_len + sb * sub
            hs = a_cum[r0:r0 + sub] * h_prev + u_cum[r0:r0 + sub]
            o_ref[0, r0:r0 + sub, :] = hs * gel[r0:r0 + sub]
            h_prev = hs[sub - 1:sub]
        h_scr[s] = h_prev

    @pl.when(c == nchunks - 1)
    def _():
        hout_ref[0] = h_scr[...]
        cbout_ref[...] = tail[...]


def lru_mixer(p3, o3, h0, cbuf0, params, *, row0, nbatch, t_len, nseq, rows):
    seq_len = min(rows, t_len)
    nchunks = t_len // seq_len
    r = nseq * seq_len
    nrb = nbatch // nseq
    rb0 = row0 // r
    cw, cbias, wa, ba, wx, bx, lam = params
    n_rows = p3.shape[1]

    def slab(off):
        return pl.BlockSpec((1, r, LANE), lambda i, n, c: (off + n, rb0 + i * nchunks + c, 0))

    def per_blk(arr):
        shp = (1,) + arr.shape[1:]
        nd = arr.ndim
        return pl.BlockSpec(shp, lambda i, n, c: (n,) + (0,) * (nd - 1))

    ins = [p3, p3, cw, cbias, wa, ba, wx, bx, lam, cbuf0, h0, o3]
    specs = [slab(48), slab(64), per_blk(cw), per_blk(cbias), per_blk(wa), per_blk(ba),
             per_blk(wx), per_blk(bx), per_blk(lam),
             pl.BlockSpec((nseq, 1, 8, LANE), lambda i, n, c: (i, n, 0, 0)),
             pl.BlockSpec((1, nseq, 1, LANE), lambda i, n, c: (n, i, 0, 0)),
             pl.BlockSpec(memory_space=pl.ANY)]
    out = pl.pallas_call(
        functools.partial(_lru_kernel, nseq=nseq, seq_len=seq_len, nchunks=nchunks),
        grid=(nrb, 16, nchunks),
        in_specs=specs,
        out_specs=[pl.BlockSpec((1, r, LANE), lambda i, n, c: (16 + n, rb0 + i * nchunks + c, 0)),
                   pl.BlockSpec((1, nseq, 1, LANE), lambda i, n, c: (n, i, 0, 0)),
                   pl.BlockSpec((nseq, 1, 8, LANE), lambda i, n, c: (i, n, 0, 0))],
        out_shape=[jax.ShapeDtypeStruct((32, n_rows, LANE), F32),
                   jax.ShapeDtypeStruct((16, nbatch, 1, LANE), F32),
                   jax.ShapeDtypeStruct((nbatch, 16, 8, LANE), F32)],
        scratch_shapes=[pltpu.VMEM((nseq, 1, LANE), F32),
                        pltpu.VMEM((nseq, 1, 8, LANE), F32),
                        pltpu.VMEM((1, r, LANE), F32),
                        pltpu.VMEM((seq_len + 8, LANE), F32)],
        input_output_aliases={len(ins) - 1: 0},
        compiler_params=_params(3),
        name="lru_seq" if nchunks > 1 else "lru_step",
    )(*ins)
    return out[0], out[1], out[2]


def _slabs(v, nblk):
    return v.astype(F32).reshape(nblk, 1, LANE)


def _pad_lanes(a, width=LANE):
    return jnp.pad(a, [(0, 0)] * (a.ndim - 1) + [(0, width - a.shape[-1])])


def _conv_state_in(buf, nblk):
    b = buf.shape[0]
    t = jnp.transpose(buf.astype(F32).reshape(b, 3, nblk, LANE), (0, 2, 1, 3))
    return jnp.pad(t, ((0, 0), (0, 0), (5, 0), (0, 0)))


def _conv_state_out(t):
    b, nblk = t.shape[:2]
    return jnp.transpose(t[:, :, 5:8, :], (0, 2, 1, 3)).reshape(b, 3, nblk * LANE)


EVEN_MAIN = 13312
ODD_GLOW = (6144, 6160)
IN_TN = 1024


def _prep_w_in_even(w):
    nl, d, _ = w.shape
    dt = _pad_lanes(w[:, :, EVEN_MAIN:].reshape(nl, d, SSM_GROUPS, SSM_HPG))
    return w.astype(BF16), dt.reshape(nl, d, SSM_GROUPS * LANE).astype(BF16)


def _prep_w_in_odd(w):
    g0, g1 = ODD_GLOW
    main = jnp.concatenate([w[:, :, :g0], w[:, :, g1:]], axis=2).astype(BF16)
    return main, _pad_lanes(w[:, :, g0:g1]).astype(BF16)


def kernel(x_prompt, x_sample, state_hgrn, state_ssm, state_ssm_conv, state_gla, state_lru,
           state_lru_conv, norm_mix_pre, norm_mix_post, norm_ffn_pre, norm_ffn_post,
           w_in_even, w_out_even, hgrn_lb_logits, hgrn_norm_w,
           ssm_conv_w, ssm_conv_b, ssm_dt_bias, ssm_a_log, ssm_d, ssm_norm_w,
           w_in_odd, w_out_odd, gla_gate_w2, gla_gate_b, gla_norm_w,
           lru_conv_w, lru_conv_b, lru_wa, lru_ba, lru_wx, lru_bx, lru_lambda,
           ffn_w_gate, ffn_w_up, ffn_w_down):
    bp, tp, d = x_prompt.shape
    bs, ts, _ = x_sample.shape
    n_p, n_s = bp * tp, bs * ts
    depth = norm_mix_pre.shape[0]
    x = jnp.concatenate([x_prompt.reshape(n_p, d), x_sample.reshape(n_s, d)], axis=0)

    lb_soft = jax.nn.softmax(hgrn_lb_logits.astype(F32), axis=0)
    hgrn_lb = jnp.maximum(jnp.cumsum(lb_soft, axis=0) - lb_soft[0], 0.0)

    paths = (dict(row0=0, nbatch=bp, t_len=tp), dict(row0=n_p, nbatch=bs, t_len=ts))
    outs = {k: ([], []) for k in ("ssm_cb", "lru", "lru_cb")}
    big = {k: [None, None] for k in ("hg", "ssm", "gla")}
    n_even, n_odd = w_in_even.shape[0], w_in_odd.shape[0]

    we_main, we_dt = _prep_w_in_even(w_in_even)
    wo_main, wo_glow = _prep_w_in_odd(w_in_odd)
    w_out_e, w_out_o = w_out_even.astype(BF16), w_out_odd.astype(BF16)
    wg, wu, wd = ffn_w_gate.astype(BF16), ffn_w_up.astype(BF16), ffn_w_down.astype(BF16)
    st_hgrn = state_hgrn.astype(F32)
    st_ssm = state_ssm.astype(F32).reshape(n_even, bs, 16, LANE, LANE)
    st_gla = state_gla.astype(F32)
    zero_hg = jnp.zeros((1, bp) + state_hgrn.shape[2:], F32)
    zero_ssm = jnp.zeros((1, bp, 16, LANE, LANE), F32)
    zero_gla = jnp.zeros((1, bp) + state_gla.shape[2:], F32)
    o3_buf = jnp.zeros((32, n_p + n_s, LANE), F32)

    for l in range(depth):
        j = l // 2
        if l % 2 == 0:
            p3 = inproj(x, norm_mix_pre[l],
                        [(we_main, j, EVEN_MAIN // IN_TN, IN_TN), (we_dt, j, 1, SSM_GROUPS * LANE)],
                        tn=IN_TN)
            hg_params = (_slabs(hgrn_lb[j], 16), _slabs(hgrn_norm_w[j], 1))
            dsk = jnp.repeat(ssm_d[j].astype(F32), SSM_HEADDIM)
            ssd_params = (
                jnp.transpose(ssm_conv_w[j].astype(F32).reshape(4, 24, LANE), (1, 0, 2)),
                _slabs(ssm_conv_b[j], 24),
                _pad_lanes(ssm_dt_bias[j].astype(F32).reshape(SSM_GROUPS, 1, SSM_HPG)),
                _pad_lanes(ssm_a_log[j].astype(F32).reshape(SSM_GROUPS, 1, SSM_HPG)),
                _slabs(dsk, 16), _slabs(ssm_norm_w[j], 16))
            o3 = o3_buf
            for pi, path in enumerate(paths):
                nb = path["nbatch"]
                if pi == 0:
                    s_hg, s_ssm, s_layer = zero_hg, zero_ssm, 0
                    s_cb = jnp.zeros((nb, 24, 8, LANE), F32)
                    nseq_a, hb_a, unroll_a, nseq_b = 1, 16, 16, 1
                else:
                    s_hg, s_ssm, s_layer = st_hgrn, st_ssm, j
                    s_cb = _conv_state_in(state_ssm_conv[j], 24)
                    nseq_a, hb_a, unroll_a, nseq_b = 8, 8, 8, 4
                o3, big["hg"][pi] = gla_mixer(
                    "hgrn", p3, o3, s_hg, s_layer, big["hg"][pi], j, n_even, hg_params,
                    nseq=nseq_a, hb=hb_a, heads=16, dk=128, dv=128, offs=(0, 16, 32, 48), out_off=0,
                    unroll=unroll_a, **path)
                o3, big["ssm"][pi], cb_b = ssd_mixer(p3, o3, s_ssm, s_layer, big["ssm"][pi], j,
                                                     n_even, s_cb, ssd_params, nseq=nseq_b, **path)
                outs["ssm_cb"][pi].append(_conv_state_out(cb_b).astype(state_ssm_conv.dtype))
            x = outproj(o3, w_out_e, j, x, norm_mix_post[l])
            o3_buf = o3
        else:
            p3 = inproj(x, norm_mix_pre[l],
                        [(wo_main, j, wo_main.shape[2] // IN_TN, IN_TN), (wo_glow, j, 1, LANE)],
                        tn=IN_TN)
            w2 = jnp.pad(gla_gate_w2[j], ((0, LANE - gla_gate_w2.shape[1]), (0, 0)))
            w2 = jnp.transpose(w2.reshape(LANE, 4, 256), (1, 0, 2)).astype(BF16)
            gla_params = (w2, _slabs(gla_gate_b[j], 8), _slabs(gla_norm_w[j], 4))
            lru_params = (
                jnp.transpose(lru_conv_w[j].astype(F32).reshape(4, 16, LANE), (1, 0, 2)),
                _slabs(lru_conv_b[j], 16), lru_wa[j].astype(BF16), _slabs(lru_ba[j], 16),
                lru_wx[j].astype(BF16), _slabs(lru_bx[j], 16), _slabs(lru_lambda[j], 16))
            o3 = o3_buf
            for pi, path in enumerate(paths):
                nb = path["nbatch"]
                if pi == 0:
                    s_gla, s_layer = zero_gla, 0
                    s_lru = jnp.zeros((16, nb, 1, LANE), F32)
                    s_cb = jnp.zeros((nb, 16, 8, LANE), F32)
                    nseq_c, hb_c, nseq_d, rows_d = 1, 4, 1, 512
                else:
                    s_gla, s_layer = st_gla, j
                    s_lru = jnp.transpose(state_lru[j].astype(F32).reshape(nb, 16, 1, LANE),
                                          (1, 0, 2, 3))
                    s_cb = _conv_state_in(state_lru_conv[j], 16)
                    nseq_c, hb_c, nseq_d, rows_d = 8, 1, 16, 8
                o3, big["gla"][pi] = gla_mixer(
                    "gla", p3, o3, s_gla, s_layer, big["gla"][pi], j, n_odd, gla_params,
                    nseq=nseq_c, hb=hb_c, heads=4, dk=256, dv=512, offs=(0, 8, 16, 32, 80),
                    out_off=0, unroll=min(4, hb_c), **path)
                o3, st_d, cb_d = lru_mixer(p3, o3, s_lru, s_cb, lru_params, nseq=nseq_d,
                                           rows=rows_d, **path)
                outs["lru"][pi].append(jnp.transpose(st_d, (1, 0, 2, 3)).reshape(nb, 16 * LANE)
                                       .astype(state_lru.dtype))
                outs["lru_cb"][pi].append(_conv_state_out(cb_d).astype(state_lru_conv.dtype))
            x = outproj(o3, w_out_o, j, x, norm_mix_post[l])
            o3_buf = o3
        x = ffn(x, norm_ffn_pre[l], wg, wu, wd, l, norm_ffn_post[l])

    y_prompt = x[:n_p].reshape(bp, tp, d).astype(x_prompt.dtype)
    y_sample = x[n_p:].reshape(bs, ts, d).astype(x_sample.dtype)
    ssm_shape = state_ssm.shape[2:]
    res = [y_prompt, y_sample]
    stacked = lambda key, pi: jnp.stack(outs[key][pi])
    res += [big["hg"][0].astype(state_hgrn.dtype), big["hg"][1].astype(state_hgrn.dtype),
            big["ssm"][0].reshape((n_even, bp) + ssm_shape).astype(state_ssm.dtype),
            big["ssm"][1].reshape((n_even, bs) + ssm_shape).astype(state_ssm.dtype),
            stacked("ssm_cb", 0), stacked("ssm_cb", 1),
            big["gla"][0].astype(state_gla.dtype), big["gla"][1].astype(state_gla.dtype),
            stacked("lru", 0), stacked("lru", 1), stacked("lru_cb", 0), stacked("lru_cb", 1)]
    return tuple(res)
```

```python
import functools
import math

import numpy as np
import jax
import jax.numpy as jnp
from jax import lax
from jax.experimental import pallas as pl
from jax.experimental.pallas import tpu as pltpu

F32 = jnp.float32
BF16 = jnp.bfloat16
LANE = 128
VMEM_LIMIT = 56 * 1024 * 1024

DOWN_STRIP = 512

EPS = 1e-6
F_MIN = 1e-30
CHUNK = 64
GLA_GATE_NORMALIZER = 16.0
LRU_C = 8.0
SSM_GROUPS = 4
SSM_HPG = 8
SSM_HEADDIM = 64


def _params(n_axes):
    return pltpu.CompilerParams(dimension_semantics=("arbitrary",) * n_axes,
                                vmem_limit_bytes=VMEM_LIMIT)


def _softplus(x):
    return jnp.maximum(x, 0.0) + jnp.log1p(jnp.exp(-jnp.abs(x)))


def _log_sigmoid(x):
    return jnp.minimum(x, 0.0) - jnp.log1p(jnp.exp(-jnp.abs(x)))


def _split3(x):
    hi = x.astype(BF16)
    r1 = x - hi.astype(F32)
    mid = r1.astype(BF16)
    lo = (r1 - mid.astype(F32)).astype(BF16)
    return jnp.concatenate([hi, mid, lo], axis=1)


def _sum3(y, w):
    return y[:, :w] + y[:, w:2 * w] + y[:, 2 * w:3 * w]


def _cat(ref, base, n):
    if n == 1:
        return ref[base]
    return jnp.concatenate([ref[base + j] for j in range(n)], axis=1)


_NT = (((1,), (1,)), ((), ()))
_TN = (((0,), (0,)), ((), ()))


def _prenorm_rows(x_ref, w_ref, h_scr, tm, rows=64):
    w = w_ref[...]

    def body(i, c):
        r0 = pl.multiple_of(i * rows, rows)
        x = x_ref[pl.ds(r0, rows), :]
        ms = jnp.mean(x * x, axis=1, keepdims=True)
        h_scr[pl.ds(r0, rows), :] = (x * lax.rsqrt(ms + EPS) * w).astype(BF16)
        return c

    lax.fori_loop(0, tm // rows, body, 0)


def _postnorm_rows(o_ref, x_ref, w_ref, tm, rows=64):
    w = w_ref[...]

    def body(i, c):
        r0 = pl.multiple_of(i * rows, rows)
        y = o_ref[pl.ds(r0, rows), :]
        ms = jnp.mean(y * y, axis=1, keepdims=True)
        o_ref[pl.ds(r0, rows), :] = x_ref[pl.ds(r0, rows), :] + y * lax.rsqrt(ms + EPS) * w
        return c

    lax.fori_loop(0, tm // rows, body, 0)


def _resident(shape, index_map):
    return pl.BlockSpec(shape, index_map, pipeline_mode=pl.Buffered(1))


def _inproj_kernel(x_ref, nw_ref, *rest, tm, segs):
    w_refs, o_ref, h_scr = rest[:len(segs)], rest[len(segs)], rest[len(segs) + 1]
    j = pl.program_id(1)

    @pl.when(j == 0)
    def _():
        _prenorm_rows(x_ref, nw_ref, h_scr, tm)

    for w_ref, (start, ntiles, width) in zip(w_refs, segs):
        @pl.when((j >= start) & (j < start + ntiles))
        def _(w_ref=w_ref, width=width):
            r = jnp.dot(h_scr[...], w_ref[0], preferred_element_type=F32)
            for c in range(width // LANE):
                o_ref[c] = r[:, c * LANE:(c + 1) * LANE]
            for c in range(width // LANE, o_ref.shape[0]):
                o_ref[c] = jnp.zeros(o_ref.shape[1:], F32)


def inproj(x, norm_w, segments, *, tm=512, tn=1024):
    n, d = x.shape
    cb = tn // LANE
    segs, specs, start = [], [], 0
    for w, layer, ntiles, width in segments:
        segs.append((start, ntiles, width))
        specs.append(pl.BlockSpec(
            (1, d, width),
            lambda i, j, layer=layer, start=start, ntiles=ntiles: (layer, 0, jnp.clip(j - start, 0, ntiles - 1))))
        start += ntiles
    return pl.pallas_call(
        functools.partial(_inproj_kernel, tm=tm, segs=tuple(segs)),
        grid=(n // tm, start),
        in_specs=[pl.BlockSpec((tm, d), lambda i, j: (i, 0)),
                  pl.BlockSpec((1, d), lambda i, j: (0, 0))] + specs,
        out_specs=pl.BlockSpec((cb, tm, LANE), lambda i, j: (j, i, 0)),
        out_shape=jax.ShapeDtypeStruct((start * cb, n, LANE), F32),
        scratch_shapes=[pltpu.VMEM((tm, d), BF16)],
        compiler_params=_params(2),
        name="inproj",
    )(x, norm_w.reshape(1, d), *[s[0] for s in segments])


def _outproj_kernel(a_ref, w3_ref, x_ref, nw_ref, o_ref, *, tm, kb, nk):
    k = pl.program_id(1)
    w_ref = w3_ref.at[0]
    a = _cat(a_ref, 0, kb).astype(BF16)

    @pl.when(k == 0)
    def _():
        o_ref[...] = jnp.zeros_like(o_ref)

    for n0 in range(0, o_ref.shape[1], DOWN_STRIP):
        o_ref[:, n0:n0 + DOWN_STRIP] += jnp.dot(a, w_ref[:, n0:n0 + DOWN_STRIP],
                                                preferred_element_type=F32)

    @pl.when(k == nk - 1)
    def _():
        _postnorm_rows(o_ref, x_ref, nw_ref, tm)


def outproj(a3, w, layer, x, norm_w, *, tm=512, tk=1024):
    n, d = x.shape
    kdim = w.shape[1]
    kb = tk // LANE
    nk = kdim // tk
    return pl.pallas_call(
        functools.partial(_outproj_kernel, tm=tm, kb=kb, nk=nk),
        grid=(n // tm, nk),
        in_specs=[pl.BlockSpec((kb, tm, LANE), lambda i, k: (k, i, 0)),
                  pl.BlockSpec((1, tk, d), lambda i, k: (layer, k, 0)),
                  pl.BlockSpec((tm, d), lambda i, k: (i, 0)),
                  pl.BlockSpec((1, d), lambda i, k: (0, 0))],
        out_specs=pl.BlockSpec((tm, d), lambda i, k: (i, 0)),
        out_shape=jax.ShapeDtypeStruct((n, d), F32),
        compiler_params=_params(2),
        name="outproj",
    )(a3, w, x, norm_w.reshape(1, d))


def _ffn_kernel(x_ref, pre_ref, wg_ref, wu_ref, wd3_ref, post_ref, o_ref, h_scr, a_scr, *, tm, nf):
    f = pl.program_id(1)
    wd_ref = wd3_ref.at[0]

    @pl.when(f == 0)
    def _():
        _prenorm_rows(x_ref, pre_ref, h_scr, tm)
        o_ref[...] = jnp.zeros_like(o_ref)
        a_scr[...] = jnp.zeros_like(a_scr)

    a_prev = a_scr[...]
    h = h_scr[...]
    g = jnp.dot(h, wg_ref[0], preferred_element_type=F32)
    u = jnp.dot(h, wu_ref[0], preferred_element_type=F32)
    for n0 in range(0, o_ref.shape[1], DOWN_STRIP):
        o_ref[:, n0:n0 + DOWN_STRIP] += jnp.dot(a_prev, wd_ref[:, n0:n0 + DOWN_STRIP],
                                                preferred_element_type=F32)
    a_scr[...] = (jax.nn.silu(g) * u).astype(BF16)

    @pl.when(f == nf)
    def _():
        _postnorm_rows(o_ref, x_ref, post_ref, tm)


def ffn(x, pre_w, wg, wu, wd, layer, post_w, *, tm=512, tf=256):
    n, d = x.shape
    dff = wg.shape[2]
    nf = dff // tf
    return pl.pallas_call(
        functools.partial(_ffn_kernel, tm=tm, nf=nf),
        grid=(n // tm, nf + 1),
        in_specs=[pl.BlockSpec((tm, d), lambda i, f: (i, 0)),
                  pl.BlockSpec((1, d), lambda i, f: (0, 0)),
                  pl.BlockSpec((1, d, tf), lambda i, f: (layer, 0, jnp.minimum(f, nf - 1))),
                  pl.BlockSpec((1, d, tf), lambda i, f: (layer, 0, jnp.minimum(f, nf - 1))),
                  pl.BlockSpec((1, tf, d), lambda i, f: (layer, jnp.maximum(f - 1, 0), 0)),
                  pl.BlockSpec((1, d), lambda i, f: (0, 0))],
        out_specs=pl.BlockSpec((tm, d), lambda i, f: (i, 0)),
        out_shape=jax.ShapeDtypeStruct((n, d), F32),
        scratch_shapes=[pltpu.VMEM((tm, d), BF16), pltpu.VMEM((tm, tf), BF16)],
        compiler_params=_params(2),
        name="ffn",
    )(x, pre_w.reshape(1, d), wg, wu, wd, post_w.reshape(1, d))


def _gla_consts(nseq, seq_len):
    r = nseq * seq_len
    t = np.arange(r)[:, None]
    s = np.arange(r)[None, :]
    prefix = ((t // seq_len) == (s // seq_len)) & (s <= t)
    masks = []
    c = seq_len // 2
    while c >= 1:
        right = (t % (2 * c)) >= c
        masks.append(((t // (2 * c)) == (s // (2 * c))) & right & ((s % (2 * c)) < c))
        c //= 2
    m = np.stack(masks).astype(np.float32)
    return jnp.asarray(prefix.astype(np.float32), BF16), jnp.asarray(m, F32)


def _block_row(b, blk, pick, pos):
    r, w = b.shape
    if blk >= 8:
        return jnp.concatenate(
            [jnp.broadcast_to(b[s + pick:s + pick + 1, :], (blk, w)) for s in range(0, r, blk)], axis=0)
    out = b
    for p in range(blk):
        if p != pick:
            out = jnp.where(pos == p, pltpu.roll(b, (p - pick) % r, axis=0), out)
    return out


def _gla_kernel(*refs, mode, nseq, seq_len, hb, dk, dv, nchunks, unroll):
    s_scr, q_scr, k_scr, b_scr, dc_scr, sc_scr = refs[-6:]
    if mode == "hgrn":
        (q_ref, f_ref, v_ref, gt_ref, lb_ref, nw_ref, a_ref, m_ref, s0_ref, _, _,
         o_ref, so_ref) = refs[:-6]
    else:
        (q_ref, k_ref, v_ref, gt_ref, gl_ref, w2_ref, gb_ref, nw_ref, a_ref, m_ref, s0_ref, _, _,
         o_ref, so_ref) = refs[:-6]
    dkb, dvb = dk // LANE, dv // LANE
    r = nseq * seq_len
    nlev = int(math.log2(seq_len))
    c = pl.program_id(2)

    @pl.when(c == 0)
    def _():
        s_scr[...] = s0_ref[0]

    a_mat = a_ref[...]
    seq_shift = int(math.log2(seq_len))
    rowk = lax.broadcasted_iota(jnp.int32, (r, dk), 0)
    colk = lax.broadcasted_iota(jnp.int32, (dk, r), 1)
    seq_of_row = lax.broadcasted_iota(jnp.int32, (r, LANE), 0) >> seq_shift
    lane_id = lax.broadcasted_iota(jnp.int32, (r, LANE), 1)
    seqsel = (seq_of_row == lane_id).astype(BF16)
    nw = _cat(nw_ref, 0, dvb)

    def prepare(h, carry):
        qr = _cat(q_ref, h * dkb, dkb)
        if mode == "hgrn":
            lb = _cat(lb_ref, h * dkb, dkb)
            fg = lb + (1.0 - lb) * jax.nn.sigmoid(_cat(f_ref, h * dkb, dkb))
            g = jnp.log(jnp.maximum(fg, F_MIN))
            k = 1.0 - fg
            q = jax.nn.silu(qr) * (dk ** -0.5)
        else:
            k = _cat(k_ref, h * dkb, dkb)
            q = qr * (dk ** -0.5)
            lin = jnp.dot(gl_ref[0].astype(BF16), w2_ref[h], preferred_element_type=F32)
            g = _log_sigmoid(lin + _cat(gb_ref, h * dkb, dkb)) / GLA_GATE_NORMALIZER

        g3 = _split3(g)
        q_scr[h] = q
        k_scr[h] = k
        b_scr[h] = _sum3(jnp.dot(a_mat, g3, preferred_element_type=F32), dk)
        d3 = lax.dot_general(g3, seqsel, _TN, preferred_element_type=F32)
        dc_scr[h] = d3[0:dk] + d3[dk:2 * dk] + d3[2 * dk:3 * dk]
        return carry

    def intra(h, carry):
        q, k, b = q_scr[h], k_scr[h], b_scr[h]
        scores = jnp.zeros((r, r), F32)
        for l in range(nlev):
            half = seq_len >> (l + 1)
            pos = rowk & (2 * half - 1)
            right = pos >= half
            b_m = _block_row(b, 2 * half, half - 1, pos)
            x = (jnp.where(right, q, k) * jnp.exp(-jnp.abs(b - b_m))).astype(BF16)
            scores = scores + m_ref[l] * lax.dot_general(x, x, _NT, preferred_element_type=F32)
        sc_scr[h] = scores
        return carry

    def combine(h, carry):
        q, k, b = q_scr[h], k_scr[h], b_scr[h]
        v = _cat(v_ref, h * dvb, dvb)
        gate = _cat(gt_ref, h * dvb, dvb)
        vb = v.astype(BF16)
        o = jnp.dot(sc_scr[h].astype(BF16), vb, preferred_element_type=F32)
        o = o + jnp.sum(q * k, axis=1, keepdims=True) * v

        qb = q * jnp.exp(b)
        kb = k * jnp.exp(_block_row(b, seq_len, seq_len - 1, None) - b)
        dcol = dc_scr[h]
        if nseq > 1:
            kb_t = kb.T
        for s in range(nseq):
            st = s_scr[s, h]
            if nseq == 1:
                qs = qb
                upd = lax.dot_general(kb.astype(BF16), vb, _TN, preferred_element_type=F32)
            else:
                qs = jnp.where((rowk >> seq_shift) == s, qb, 0.0)
                ks_t = jnp.where((colk >> seq_shift) == s, kb_t, 0.0)
                upd = jnp.dot(ks_t.astype(BF16), vb, preferred_element_type=F32)
            o = o + jnp.dot(qs.astype(BF16), st.astype(BF16), preferred_element_type=F32)
            dec = jnp.exp(jnp.broadcast_to(dcol[:, s:s + 1], (dk, dv)))
            s_scr[s, h] = dec * st + upd

        ms = jnp.mean(o * o, axis=1, keepdims=True)
        y = o * lax.rsqrt(ms + EPS) * nw * jax.nn.silu(gate)
        for j in range(dvb):
            o_ref[h * dvb + j] = y[:, j * LANE:(j + 1) * LANE]
        return carry

    for stage in (prepare, intra, combine):
        lax.fori_loop(0, hb, stage, 0, unroll=unroll)

    @pl.when(c == nchunks - 1)
    def _():
        so_ref[0] = s_scr[...]
        for other in range(1, so_ref.shape[0]):
            so_ref[other] = jnp.zeros(so_ref.shape[1:], F32)


def _alias_or_dummy(arr, ins, specs, aliases, out_idx):
    if arr is None:
        ins.append(jnp.zeros((8, LANE), F32))
    else:
        ins.append(arr)
        aliases[len(ins) - 1] = out_idx
    specs.append(pl.BlockSpec(memory_space=pl.ANY))


def gla_mixer(mode, p3, o3, s0, s0_layer, st_all, layer, n_layers, params, *, row0, nbatch, t_len,
              nseq, hb, heads, dk, dv, offs, out_off, unroll):
    seq_len = min(CHUNK, t_len)
    nchunks = t_len // seq_len
    r = nseq * seq_len
    dkb, dvb = dk // LANE, dv // LANE
    nrb = nbatch // nseq
    rb0 = row0 // r
    a_mat, masks = _gla_consts(nseq, seq_len)
    n_rows = p3.shape[1]

    def slab(nblk, off):
        return pl.BlockSpec((nblk, r, LANE),
                            lambda i, hg, c: (off // nblk + hg, rb0 + i * nchunks + c, 0))

    def const(arr):
        nd = arr.ndim
        return pl.BlockSpec(arr.shape, lambda i, hg, c: (0,) * nd)

    def per_head(arr, nblk):
        return pl.BlockSpec((nblk, 1, LANE), lambda i, hg, c: (hg, 0, 0))

    if mode == "hgrn":
        lb, nw = params
        ins = [p3, p3, p3, p3, lb, nw, a_mat, masks, s0]
        specs = [slab(hb * dkb, offs[0]), slab(hb * dkb, offs[1]), slab(hb * dvb, offs[2]),
                 slab(hb * dvb, offs[3]), per_head(lb, hb * dkb), const(nw), const(a_mat),
                 const(masks)]
    else:
        w2, gb, nw = params
        ins = [p3, p3, p3, p3, p3, w2, gb, nw, a_mat, masks, s0]
        specs = [slab(hb * dkb, offs[0]), slab(hb * dkb, offs[1]), slab(hb * dvb, offs[2]),
                 slab(hb * dvb, offs[3]),
                 pl.BlockSpec((1, r, LANE), lambda i, hg, c: (offs[4], rb0 + i * nchunks + c, 0)),
                 pl.BlockSpec((hb, LANE, dk), lambda i, hg, c: (hg, 0, 0)),
                 per_head(gb, hb * dkb), const(nw), const(a_mat), const(masks)]
    specs.append(pl.BlockSpec((1, nseq, hb, dk, dv), lambda i, hg, c: (s0_layer, i, hg, 0, 0)))
    aliases = {}
    _alias_or_dummy(o3, ins, specs, aliases, 0)
    _alias_or_dummy(st_all, ins, specs, aliases, 1)
    assert st_all is not None or layer == 0
    slots = 1 if st_all is not None else n_layers

    out = pl.pallas_call(
        functools.partial(_gla_kernel, mode=mode, nseq=nseq, seq_len=seq_len, hb=hb, dk=dk, dv=dv,
                          nchunks=nchunks, unroll=unroll),
        grid=(nrb, heads // hb, nchunks),
        in_specs=specs,
        out_specs=[pl.BlockSpec((hb * dvb, r, LANE),
                                lambda i, hg, c: (out_off // (hb * dvb) + hg, rb0 + i * nchunks + c, 0)),
                   pl.BlockSpec((slots, nseq, hb, dk, dv), lambda i, hg, c: (layer, i, hg, 0, 0))],
        out_shape=[jax.ShapeDtypeStruct((32, n_rows, LANE), F32),
                   jax.ShapeDtypeStruct((n_layers, nbatch, heads, dk, dv), F32)],
        scratch_shapes=[pltpu.VMEM((nseq, hb, dk, dv), F32),
                        pltpu.VMEM((hb, r, dk), F32),
                        pltpu.VMEM((hb, r, dk), F32),
                        pltpu.VMEM((hb, r, dk), F32),
                        pltpu.VMEM((hb, dk, LANE), F32),
                        pltpu.VMEM((hb, r, r), F32)],
        input_output_aliases=aliases,
        compiler_params=_params(3),
        name=mode + ("_seq" if nchunks > 1 else "_step"),
    )(*ins)
    return out[0], out[1]


def _conv_block(ref, i, cbi, cw_ref, cb_ref, tail, ext, xcs, nseq, seq_len, act):
    w = cw_ref[cbi]
    b = cb_ref[cbi]
    for s in range(nseq):
        ext[0:8, :] = tail[s, cbi]
        ext[8:8 + seq_len, :] = ref[i, s * seq_len:(s + 1) * seq_len, :]
        y = (b + w[3:4] * ext[8:8 + seq_len, :] + w[2:3] * ext[7:7 + seq_len, :]
             + w[1:2] * ext[6:6 + seq_len, :] + w[0:1] * ext[5:5 + seq_len, :])
        xcs[cbi, s * seq_len:(s + 1) * seq_len, :] = act(y)
        tail[s, cbi] = ext[seq_len:seq_len + 8, :]


def _ssd_kernel(z_ref, xa_ref, xb_ref, xc_ref, dt_ref, cw_ref, cb_ref, dtb_ref, alog_ref, dsk_ref,
                nw_ref, t_ref, tt_ref, cbuf0_ref, h0_ref, _o3_any, _h_any, o_ref, hout_ref, cbout_ref,
                h_scr, tail, xcs, ext, *, nseq, seq_len, nchunks):
    r = nseq * seq_len
    c = pl.program_id(1)

    @pl.when(c == 0)
    def _():
        h_scr[...] = h0_ref[0]
        tail[...] = cbuf0_ref[...]

    for part, ref in enumerate((xa_ref, xb_ref, xc_ref)):
        def body(i, carry, part=part, ref=ref):
            _conv_block(ref, i, part * 8 + i, cw_ref, cb_ref, tail, ext, xcs, nseq, seq_len,
                        jax.nn.silu)
            return carry
        lax.fori_loop(0, 8, body, 0, unroll=True)

    lane = lax.broadcasted_iota(jnp.int32, (r, LANE), 1)
    lo = lane < SSM_HEADDIM
    row_lo = lax.broadcasted_iota(jnp.int32, (LANE, LANE), 0) < SSM_HEADDIM
    seq_of_row = lax.broadcasted_iota(jnp.int32, (r, LANE), 0) >> int(math.log2(seq_len))
    tmat = t_ref[...]
    ttmat = tt_ref[...]
    causal = tmat.astype(F32) > 0.0

    def bcast_col(arr, j):
        return jnp.broadcast_to(arr[:, j:j + 1], (r, LANE))

    def group(g, carry):
        dt = _softplus(dt_ref[g] + dtb_ref[g])
        dta = dt * (-jnp.exp(alog_ref[g]))
        d3 = _split3(dta)
        cum = _sum3(jnp.dot(tmat, d3, preferred_element_type=F32), LANE)
        ct3 = lax.dot_general(d3, ttmat, _TN, preferred_element_type=F32)
        cum_t = ct3[0:LANE] + ct3[LANE:2 * LANE] + ct3[2 * LANE:3 * LANE]
        b_g = xcs[16 + g]
        c_g = xcs[20 + g]
        b_b = b_g.astype(BF16)
        cb_m = lax.dot_general(c_g.astype(BF16), b_b, _NT, preferred_element_type=F32)

        def decay_mat(j):
            rel = (jnp.broadcast_to(cum[:, j:j + 1], (r, r))
                   - jnp.broadcast_to(cum_t[j:j + 1, :], (r, r)))
            dec = jnp.where(causal, jnp.exp(jnp.where(causal, rel, 0.0)), 0.0)
            return (cb_m * dec).astype(BF16)

        ys = []
        for jj in range(4):
            j0, j1 = 2 * jj, 2 * jj + 1
            cbi = g * 4 + jj
            x_cb = xcs[cbi]
            dt_e = jnp.where(lo, bcast_col(dt, j0), bcast_col(dt, j1))
            cum_e = jnp.where(lo, bcast_col(cum, j0), bcast_col(cum, j1))
            u = x_cb * dt_e
            u_b = u.astype(BF16)
            y = jnp.where(lo,
                          jnp.dot(decay_mat(j0), u_b, preferred_element_type=F32),
                          jnp.dot(decay_mat(j1), u_b, preferred_element_type=F32))
            y_in = jnp.zeros((r, LANE), F32)
            for s in range(nseq):
                rl = s * seq_len + seq_len - 1
                h_cb = h_scr[s, cbi]
                if nseq == 1:
                    c_s = c_g
                    rel = cum_e[rl:rl + 1, :] - cum_e
                    uw = u * jnp.exp(rel)
                else:
                    in_seq = seq_of_row == s
                    c_s = jnp.where(in_seq, c_g, 0.0)
                    rel = jnp.where(in_seq, cum_e[rl:rl + 1, :] - cum_e, 0.0)
                    uw = jnp.where(in_seq, u * jnp.exp(rel), 0.0)
                y_in = y_in + lax.dot_general(c_s.astype(BF16), h_cb.astype(BF16), _NT,
                                              preferred_element_type=F32)
                last = jnp.where(row_lo,
                                 jnp.broadcast_to(cum[rl:rl + 1, j0:j0 + 1], (LANE, LANE)),
                                 jnp.broadcast_to(cum[rl:rl + 1, j1:j1 + 1], (LANE, LANE)))
                h_scr[s, cbi] = jnp.exp(last) * h_cb + lax.dot_general(
                    uw.astype(BF16), b_b, _TN, preferred_element_type=F32)
            y = y + y_in * jnp.exp(cum_e) + dsk_ref[cbi] * x_cb
            ys.append(y * jax.nn.silu(z_ref[cbi]))
        ms = sum(jnp.sum(y * y, axis=1, keepdims=True) for y in ys) * (1.0 / (4 * LANE))
        rinv = lax.rsqrt(ms + EPS)
        for jj in range(4):
            o_ref[g * 4 + jj] = ys[jj] * rinv * nw_ref[g * 4 + jj]
        return carry

    lax.fori_loop(0, SSM_GROUPS, group, 0, unroll=True)

    @pl.when(c == nchunks - 1)
    def _():
        hout_ref[0] = h_scr[...]
        for other in range(1, hout_ref.shape[0]):
            hout_ref[other] = jnp.zeros(hout_ref.shape[1:], F32)
        cbout_ref[...] = tail[...]


def _tri_consts(nseq, seq_len):
    r = nseq * seq_len
    t = np.arange(r)[:, None]
    s = np.arange(r)[None, :]
    m = (((t // seq_len) == (s // seq_len)) & (s <= t)).astype(np.float32)
    return jnp.asarray(m, BF16), jnp.asarray(m.T, BF16)


def ssd_mixer(p3, o3, h0, h0_layer, h_all, layer, n_layers, cbuf0, params, *, row0, nbatch, t_len,
              nseq):
    seq_len = min(CHUNK, t_len)
    nchunks = t_len // seq_len
    r = nseq * seq_len
    nrb = nbatch // nseq
    rb0 = row0 // r
    cw, cbias, dtb, alog, dsk, nw = params
    tmat, ttmat = _tri_consts(nseq, seq_len)
    n_rows = p3.shape[1]

    def slab(nblk, blk_idx):
        return pl.BlockSpec((nblk, r, LANE), lambda i, c: (blk_idx, rb0 + i * nchunks + c, 0))

    def const(arr):
        nd = arr.ndim
        return pl.BlockSpec(arr.shape, lambda i, c: (0,) * nd)

    ins = [p3, p3, p3, p3, p3, cw, cbias, dtb, alog, dsk, nw, tmat, ttmat, cbuf0, h0]
    specs = [slab(16, 4), slab(8, 10), slab(8, 11), slab(8, 12), slab(4, 26),
             const(cw), const(cbias), const(dtb), const(alog), const(dsk), const(nw),
             const(tmat), const(ttmat),
             pl.BlockSpec((nseq, 24, 8, LANE), lambda i, c: (i, 0, 0, 0)),
             pl.BlockSpec((1, nseq, 16, LANE, LANE), lambda i, c: (h0_layer, i, 0, 0, 0))]
    aliases = {}
    _alias_or_dummy(o3, ins, specs, aliases, 0)
    _alias_or_dummy(h_all, ins, specs, aliases, 1)
    assert h_all is not None or layer == 0
    slots = 1 if h_all is not None else n_layers
    out = pl.pallas_call(
        functools.partial(_ssd_kernel, nseq=nseq, seq_len=seq_len, nchunks=nchunks),
        grid=(nrb, nchunks),
        in_specs=specs,
        out_specs=[pl.BlockSpec((16, r, LANE), lambda i, c: (1, rb0 + i * nchunks + c, 0)),
                   pl.BlockSpec((slots, nseq, 16, LANE, LANE), lambda i, c: (layer, i, 0, 0, 0)),
                   pl.BlockSpec((nseq, 24, 8, LANE), lambda i, c: (i, 0, 0, 0))],
        out_shape=[jax.ShapeDtypeStruct((32, n_rows, LANE), F32),
                   jax.ShapeDtypeStruct((n_layers, nbatch, 16, LANE, LANE), F32),
                   jax.ShapeDtypeStruct((nbatch, 24, 8, LANE), F32)],
        scratch_shapes=[pltpu.VMEM((nseq, 16, LANE, LANE), F32),
                        pltpu.VMEM((nseq, 24, 8, LANE), F32),
                        pltpu.VMEM((24, r, LANE), F32),
                        pltpu.VMEM((seq_len + 8, LANE), F32)],
        input_output_aliases=aliases,
        compiler_params=_params(2),
        name="ssd_seq" if nchunks > 1 else "ssd_step",
    )(*ins)
    return out[0], out[1], out[2]


def _lru_kernel(x_ref, y_ref, cw_ref, cb_ref, wa_ref, ba_ref, wx_ref, bx_ref, lam_ref, cbuf0_ref,
                h0_ref, _, o_ref, hout_ref, cbout_ref, h_scr, tail, xcs, ext,
                *, nseq, seq_len, nchunks):
    r = nseq * seq_len
    sub = min(seq_len, CHUNK)
    c = pl.program_id(2)

    @pl.when(c == 0)
    def _():
        h_scr[...] = h0_ref[0]
        tail[...] = cbuf0_ref[...]

    _conv_block(x_ref, 0, 0, cw_ref, cb_ref, tail, ext, xcs, nseq, seq_len, lambda v: v)
    xc = xcs[0]
    xc_b = xc.astype(BF16)
    rg = jax.nn.sigmoid(jnp.dot(xc_b, wa_ref[0], preferred_element_type=F32) + ba_ref[0])
    ig = jax.nn.sigmoid(jnp.dot(xc_b, wx_ref[0], preferred_element_type=F32) + bx_ref[0])
    log_a = -LRU_C * rg * _softplus(-lam_ref[0])
    a_cum = jnp.exp(log_a)
    th = jnp.tanh(log_a)
    u_cum = jnp.sqrt(-2.0 * th / (1.0 - th)) * (ig * xc)

    pos = lax.broadcasted_iota(jnp.int32, (r, LANE), 0) & (sub - 1)
    d = 1
    while d < sub:
        take = pos >= d
        a_prev = pltpu.roll(a_cum, d, axis=0)
        u_prev = pltpu.roll(u_cum, d, axis=0)
        u_cum = jnp.where(take, a_cum * u_prev + u_cum, u_cum)
        a_cum = jnp.where(take, a_cum * a_prev, a_cum)
        d *= 2

    gel = jax.nn.gelu(y_ref[0])
    for s in range(nseq):
        h_prev = h_scr[s]
        for sb in range(seq_len // sub):
            r0 = s * seq_len + sb * sub
            hs = a_cum[r0:r0 + sub] * h_prev + u_cum[r0:r0 + sub]
            o_ref[0, r0:r0 + sub, :] = hs * gel[r0:r0 + sub]
            h_prev = hs[sub - 1:sub]
        h_scr[s] = h_prev

    @pl.when(c == nchunks - 1)
    def _():
        hout_ref[0] = h_scr[...]
        cbout_ref[...] = tail[...]


def lru_mixer(p3, o3, h0, cbuf0, params, *, row0, nbatch, t_len, nseq, rows):
    seq_len = min(rows, t_len)
    nchunks = t_len // seq_len
    r = nseq * seq_len
    nrb = nbatch // nseq
    rb0 = row0 // r
    cw, cbias, wa, ba, wx, bx, lam = params
    n_rows = p3.shape[1]

    def slab(off):
        return pl.BlockSpec((1, r, LANE), lambda i, n, c: (off + n, rb0 + i * nchunks + c, 0))

    def per_blk(arr):
        shp = (1,) + arr.shape[1:]
        nd = arr.ndim
        return pl.BlockSpec(shp, lambda i, n, c: (n,) + (0,) * (nd - 1))

    ins = [p3, p3, cw, cbias, wa, ba, wx, bx, lam, cbuf0, h0, o3]
    specs = [slab(48), slab(64), per_blk(cw), per_blk(cbias), per_blk(wa), per_blk(ba),
             per_blk(wx), per_blk(bx), per_blk(lam),
             pl.BlockSpec((nseq, 1, 8, LANE), lambda i, n, c: (i, n, 0, 0)),
             pl.BlockSpec((1, nseq, 1, LANE), lambda i, n, c: (n, i, 0, 0)),
             pl.BlockSpec(memory_space=pl.ANY)]
    out = pl.pallas_call(
        functools.partial(_lru_kernel, nseq=nseq, seq_len=seq_len, nchunks=nchunks),
        grid=(nrb, 16, nchunks),
        in_specs=specs,
        out_specs=[pl.BlockSpec((1, r, LANE), lambda i, n, c: (16 + n, rb0 + i * nchunks + c, 0)),
                   pl.BlockSpec((1, nseq, 1, LANE), lambda i, n, c: (n, i, 0, 0)),
                   pl.BlockSpec((nseq, 1, 8, LANE), lambda i, n, c: (i, n, 0, 0))],
        out_shape=[jax.ShapeDtypeStruct((32, n_rows, LANE), F32),
                   jax.ShapeDtypeStruct((16, nbatch, 1, LANE), F32),
                   jax.ShapeDtypeStruct((nbatch, 16, 8, LANE), F32)],
        scratch_shapes=[pltpu.VMEM((nseq, 1, LANE), F32),
                        pltpu.VMEM((nseq, 1, 8, LANE), F32),
                        pltpu.VMEM((1, r, LANE), F32),
                        pltpu.VMEM((seq_len + 8, LANE), F32)],
        input_output_aliases={len(ins) - 1: 0},
        compiler_params=_params(3),
        name="lru_seq" if nchunks > 1 else "lru_step",
    )(*ins)
    return out[0], out[1], out[2]


def _slabs(v, nblk):
    return v.astype(F32).reshape(nblk, 1, LANE)


def _pad_lanes(a, width=LANE):
    return jnp.pad(a, [(0, 0)] * (a.ndim - 1) + [(0, width - a.shape[-1])])


def _conv_state_in(buf, nblk):
    b = buf.shape[0]
    t = jnp.transpose(buf.astype(F32).reshape(b, 3, nblk, LANE), (0, 2, 1, 3))
    return jnp.pad(t, ((0, 0), (0, 0), (5, 0), (0, 0)))


def _conv_state_out(t):
    b, nblk = t.shape[:2]
    return jnp.transpose(t[:, :, 5:8, :], (0, 2, 1, 3)).reshape(b, 3, nblk * LANE)


EVEN_MAIN = 13312
ODD_GLOW = (6144, 6160)
IN_TN = 1024


def _prep_w_in_even(w):
    nl, d, _ = w.shape
    dt = _pad_lanes(w[:, :, EVEN_MAIN:].reshape(nl, d, SSM_GROUPS, SSM_HPG))
    return w.astype(BF16), dt.reshape(nl, d, SSM_GROUPS * LANE).astype(BF16)


def _prep_w_in_odd(w):
    g0, g1 = ODD_GLOW
    main = jnp.concatenate([w[:, :, :g0], w[:, :, g1:]], axis=2).astype(BF16)
    return main, _pad_lanes(w[:, :, g0:g1]).astype(BF16)


def kernel(x_prompt, x_sample, state_hgrn, state_ssm, state_ssm_conv, state_gla, state_lru,
           state_lru_conv, norm_mix_pre, norm_mix_post, norm_ffn_pre, norm_ffn_post,
           w_in_even, w_out_even, hgrn_lb_logits, hgrn_norm_w,
           ssm_conv_w, ssm_conv_b, ssm_dt_bias, ssm_a_log, ssm_d, ssm_norm_w,
           w_in_odd, w_out_odd, gla_gate_w2, gla_gate_b, gla_norm_w,
           lru_conv_w, lru_conv_b, lru_wa, lru_ba, lru_wx, lru_bx, lru_lambda,
           ffn_w_gate, ffn_w_up, ffn_w_down):
    bp, tp, d = x_prompt.shape
    bs, ts, _ = x_sample.shape
    n_p, n_s = bp * tp, bs * ts
    depth = norm_mix_pre.shape[0]
    x = jnp.concatenate([x_prompt.reshape(n_p, d), x_sample.reshape(n_s, d)], axis=0)

    lb_soft = jax.nn.softmax(hgrn_lb_logits.astype(F32), axis=0)
    hgrn_lb = jnp.maximum(jnp.cumsum(lb_soft, axis=0) - lb_soft[0], 0.0)

    paths = (dict(row0=0, nbatch=bp, t_len=tp), dict(row0=n_p, nbatch=bs, t_len=ts))
    outs = {k: ([], []) for k in ("ssm_cb", "lru", "lru_cb")}
    big = {k: [None, None] for k in ("hg", "ssm", "gla")}
    n_even, n_odd = w_in_even.shape[0], w_in_odd.shape[0]

    we_main, we_dt = _prep_w_in_even(w_in_even)
    wo_main, wo_glow = _prep_w_in_odd(w_in_odd)
    w_out_e, w_out_o = w_out_even.astype(BF16), w_out_odd.astype(BF16)
    wg, wu, wd = ffn_w_gate.astype(BF16), ffn_w_up.astype(BF16), ffn_w_down.astype(BF16)
    st_hgrn = state_hgrn.astype(F32)
    st_ssm = state_ssm.astype(F32).reshape(n_even, bs, 16, LANE, LANE)
    st_gla = state_gla.astype(F32)
    zero_hg = jnp.zeros((1, bp) + state_hgrn.shape[2:], F32)
    zero_ssm = jnp.zeros((1, bp, 16, LANE, LANE), F32)
    zero_gla = jnp.zeros((1, bp) + state_gla.shape[2:], F32)
    o3_buf = jnp.zeros((32, n_p + n_s, LANE), F32)

    for l in range(depth):
        j = l // 2
        if l % 2 == 0:
            p3 = inproj(x, norm_mix_pre[l],
                        [(we_main, j, EVEN_MAIN // IN_TN, IN_TN), (we_dt, j, 1, SSM_GROUPS * LANE)],
                        tn=IN_TN)
            hg_params = (_slabs(hgrn_lb[j], 16), _slabs(hgrn_norm_w[j], 1))
            dsk = jnp.repeat(ssm_d[j].astype(F32), SSM_HEADDIM)
            ssd_params = (
                jnp.transpose(ssm_conv_w[j].astype(F32).reshape(4, 24, LANE), (1, 0, 2)),
                _slabs(ssm_conv_b[j], 24),
                _pad_lanes(ssm_dt_bias[j].astype(F32).reshape(SSM_GROUPS, 1, SSM_HPG)),
                _pad_lanes(ssm_a_log[j].astype(F32).reshape(SSM_GROUPS, 1, SSM_HPG)),
                _slabs(dsk, 16), _slabs(ssm_norm_w[j], 16))
            o3 = o3_buf
            for pi, path in enumerate(paths):
                nb = path["nbatch"]
                if pi == 0:
                    s_hg, s_ssm, s_layer = zero_hg, zero_ssm, 0
                    s_cb = jnp.zeros((nb, 24, 8, LANE), F32)
                    nseq_a, hb_a, unroll_a, nseq_b = 1, 16, 16, 1
                else:
                    s_hg, s_ssm, s_layer = st_hgrn, st_ssm, j
                    s_cb = _conv_state_in(state_ssm_conv[j], 24)
                    nseq_a, hb_a, unroll_a, nseq_b = 8, 8, 8, 4
                o3, big["hg"][pi] = gla_mixer(
                    "hgrn", p3, o3, s_hg, s_layer, big["hg"][pi], j, n_even, hg_params,
                    nseq=nseq_a, hb=hb_a, heads=16, dk=128, dv=128, offs=(0, 16, 32, 48), out_off=0,
                    unroll=unroll_a, **path)
                o3, big["ssm"][pi], cb_b = ssd_mixer(p3, o3, s_ssm, s_layer, big["ssm"][pi], j,
                                                     n_even, s_cb, ssd_params, nseq=nseq_b, **path)
                outs["ssm_cb"][pi].append(_conv_state_out(cb_b).astype(state_ssm_conv.dtype))
            x = outproj(o3, w_out_e, j, x, norm_mix_post[l])
            o3_buf = o3
        else:
            p3 = inproj(x, norm_mix_pre[l],
                        [(wo_main, j, wo_main.shape[2] // IN_TN, IN_TN), (wo_glow, j, 1, LANE)],
                        tn=IN_TN)
            w2 = jnp.pad(gla_gate_w2[j], ((0, LANE - gla_gate_w2.shape[1]), (0, 0)))
            w2 = jnp.transpose(w2.reshape(LANE, 4, 256), (1, 0, 2)).astype(BF16)
            gla_params = (w2, _slabs(gla_gate_b[j], 8), _slabs(gla_norm_w[j], 4))
            lru_params = (
                jnp.transpose(lru_conv_w[j].astype(F32).reshape(4, 16, LANE), (1, 0, 2)),
                _slabs(lru_conv_b[j], 16), lru_wa[j].astype(BF16), _slabs(lru_ba[j], 16),
                lru_wx[j].astype(BF16), _slabs(lru_bx[j], 16), _slabs(lru_lambda[j], 16))
            o3 = o3_buf
            for pi, path in enumerate(paths):
                nb = path["nbatch"]
                if pi == 0:
                    s_gla, s_layer = zero_gla, 0
                    s_lru = jnp.zeros((16, nb, 1, LANE), F32)
                    s_cb = jnp.zeros((nb, 16, 8, LANE), F32)
                    nseq_c, hb_c, nseq_d, rows_d = 1, 4, 1, 1024
                else:
                    s_gla, s_layer = st_gla, j
                    s_lru = jnp.transpose(state_lru[j].astype(F32).reshape(nb, 16, 1, LANE),
                                          (1, 0, 2, 3))
                    s_cb = _conv_state_in(state_lru_conv[j], 16)
                    nseq_c, hb_c, nseq_d, rows_d = 8, 1, 16, 8
                o3, big["gla"][pi] = gla_mixer(
                    "gla", p3, o3, s_gla, s_layer, big["gla"][pi], j, n_odd, gla_params,
                    nseq=nseq_c, hb=hb_c, heads=4, dk=256, dv=512, offs=(0, 8, 16, 32, 80),
                    out_off=0, unroll=min(4, hb_c), **path)
                o3, st_d, cb_d = lru_mixer(p3, o3, s_lru, s_cb, lru_params, nseq=nseq_d,
                                           rows=rows_d, **path)
                outs["lru"][pi].append(jnp.transpose(st_d, (1, 0, 2, 3)).reshape(nb, 16 * LANE)
                                       .astype(state_lru.dtype))
                outs["lru_cb"][pi].append(_conv_state_out(cb_d).astype(state_lru_conv.dtype))
            x = outproj(o3, w_out_o, j, x, norm_mix_post[l])
            o3_buf = o3
        x = ffn(x, norm_ffn_pre[l], wg, wu, wd, l, norm_ffn_post[l])

    y_prompt = x[:n_p].reshape(bp, tp, d).astype(x_prompt.dtype)
    y_sample = x[n_p:].reshape(bs, ts, d).astype(x_sample.dtype)
    ssm_shape = state_ssm.shape[2:]
    res = [y_prompt, y_sample]
    stacked = lambda key, pi: jnp.stack(outs[key][pi])
    res += [big["hg"][0].astype(state_hgrn.dtype), big["hg"][1].astype(state_hgrn.dtype),
            big["ssm"][0].reshape((n_even, bp) + ssm_shape).astype(state_ssm.dtype),
            big["ssm"][1].reshape((n_even, bs) + ssm_shape).astype(state_ssm.dtype),
            stacked("ssm_cb", 0), stacked("ssm_cb", 1),
            big["gla"][0].astype(state_gla.dtype), big["gla"][1].astype(state_gla.dtype),
            stacked("lru", 0), stacked("lru", 1), stacked("lru_cb", 0), stacked("lru_cb", 1)]
    return tuple(res)
```
